```python
import jax, jax.numpy as jnp
from jax import lax
import numpy as np

D_MODEL = 1024
BATCH = 4
SEQ = 4096
DEPTH = 1
DEC_BATCH = 32
DEC_SEQ = 1
PAST_LEN = 8192
PAGE_SIZE = 128

CONV_CH = 512
CONV_WIDTH = 31
N_HEADS = 8
HEAD_DIM = 64
ATTN_W = N_HEADS * HEAD_DIM
MOBA_BLOCK = 256
MOBA_TOPK = 3
Q_BLOCK = 32
N_MEM = 256
MEM_HEADS = 4
MEM_HEAD_DIM = 128
MEM_W = MEM_HEADS * MEM_HEAD_DIM
N_BRANCH = 3
D_FF = 2816
FFN_CONV_WIDTH = 3
EPS = 1e-6
N_IN = 2 * CONV_CH + 3 * ATTN_W + MEM_W + N_BRANCH * D_MODEL

kernel_name = 'hybrid_conformer_moba_memxattn_convffn_step'


def rmsnorm(x, g):
    xf = x.astype(jnp.float32)
    y = xf * lax.rsqrt(jnp.mean(xf * xf, axis=-1, keepdims=True) + EPS)
    return (y * g.astype(jnp.float32)).astype(x.dtype)


def layernorm(x, g, b):
    xf = x.astype(jnp.float32)
    mu = jnp.mean(xf, axis=-1, keepdims=True)
    xc = xf - mu
    var = jnp.mean(xc * xc, axis=-1, keepdims=True)
    y = xc * lax.rsqrt(var + EPS) * g.astype(jnp.float32) + b.astype(jnp.float32)
    return y.astype(x.dtype)


def alibi_slopes():
    return jnp.asarray(2.0 ** (-8.0 * np.arange(1, N_HEADS + 1) / N_HEADS), dtype=jnp.float32)


def causal_dwconv(u, prev, w, b):
    full = jnp.concatenate([prev.astype(u.dtype), u], axis=1)
    out = lax.conv_general_dilated(
        full, w[:, None, :].astype(u.dtype), window_strides=(1,), padding='VALID',
        dimension_numbers=('NWC', 'WIO', 'NWC'), feature_group_count=u.shape[-1])
    return out + b.astype(u.dtype), full[:, -(w.shape[0] - 1):]


def pre_mix(x, norm1_g, w_in, q_norm_g, k_norm_g, mq_norm_g):
    n, t = x.shape[:2]
    z = rmsnorm(x, norm1_g) @ w_in
    cuts = [int(c) for c in np.cumsum([2 * CONV_CH, ATTN_W, ATTN_W, ATTN_W, MEM_W])]
    glu, q, k, v, mq, gl = jnp.split(z, cuts, axis=-1)
    a, g = jnp.split(glu, 2, axis=-1)
    u = a * jax.nn.sigmoid(g)
    q = rmsnorm(q.reshape(n, t, N_HEADS, HEAD_DIM), q_norm_g)
    k = rmsnorm(k.reshape(n, t, N_HEADS, HEAD_DIM), k_norm_g)
    v = v.reshape(n, t, N_HEADS, HEAD_DIM)
    mq = rmsnorm(mq.reshape(n, t, MEM_HEADS, MEM_HEAD_DIM), mq_norm_g)
    return u, q, k, v, mq, gl


def conv_branch(u, prev, w_dw_a, b_dw_a, ln_a_g, ln_a_b, w_proj_a):
    c, new_prev = causal_dwconv(u, prev, w_dw_a, b_dw_a)
    c = jax.nn.silu(layernorm(c, ln_a_g, ln_a_b))
    return c @ w_proj_a, new_prev


def mem_kv(mem, mem_norm_g, w_mem_kv, mk_norm_g):
    n, m = mem.shape[:2]
    k, v = jnp.split(rmsnorm(mem, mem_norm_g) @ w_mem_kv, 2, axis=-1)
    k = rmsnorm(k.reshape(n, m, MEM_HEADS, MEM_HEAD_DIM), mk_norm_g)
    return k, v.reshape(n, m, MEM_HEADS, MEM_HEAD_DIM)


def mem_attend(mq, mk, mv, w_proj_c):
    n, t = mq.shape[:2]
    s = jnp.einsum('nthd,nmhd->nhtm', mq, mk.astype(mq.dtype)).astype(jnp.float32) * MEM_HEAD_DIM ** -0.5
    p = jax.nn.softmax(s, axis=-1).astype(mq.dtype)
    o = jnp.einsum('nhtm,nmhd->nthd', p, mv.astype(mq.dtype))
    return o.reshape(n, t, MEM_W) @ w_proj_c


def to_blocks(k):
    n, t, h, d = k.shape
    nb = -(-t // MOBA_BLOCK)
    k = jnp.pad(k, ((0, 0), (0, nb * MOBA_BLOCK - t), (0, 0), (0, 0)))
    return k.reshape(n, nb, MOBA_BLOCK, h, d).transpose(0, 3, 1, 2, 4)


def block_means(kb):
    return jnp.mean(kb.astype(jnp.float32), axis=3).astype(kb.dtype)


def moba_attend(q, t_pos, kb, vb, kmean, slopes):
    n, h, nb, bs, d = kb.shape
    q_len = q.shape[2]
    n_top = min(MOBA_TOPK, nb)
    cur = t_pos // bs
    past = jnp.arange(nb, dtype=jnp.int32)[None, :] < cur[:, None]
    gate = jnp.einsum('nhqd,nhbd->nhqb', q, kmean).astype(jnp.float32)
    gate = jnp.where(past[None, None], gate, -jnp.inf)
    _, top = lax.top_k(gate, n_top)
    own = jnp.broadcast_to(cur[None, None, :, None], (n, h, q_len, 1))
    blocks = jnp.concatenate([top.astype(jnp.int32), own], axis=-1)
    valid = jnp.concatenate([top < cur[None, None, :, None], jnp.ones((n, h, q_len, 1), bool)], axis=-1)
    flat = (jnp.arange(n, dtype=jnp.int32)[:, None, None, None] * h
            + jnp.arange(h, dtype=jnp.int32)[None, :, None, None]) * nb + blocks
    ksel = kb.reshape(n * h * nb, bs, d)[flat]
    vsel = vb.reshape(n * h * nb, bs, d)[flat]
    s_pos = blocks[..., None] * bs + jnp.arange(bs, dtype=jnp.int32)
    dist = (t_pos[None, None, :, None, None] - s_pos).astype(jnp.float32)
    logits = (jnp.einsum('nhqd,nhqsjd->nhqsj', q, ksel).astype(jnp.float32) * HEAD_DIM ** -0.5
              - slopes[None, :, None, None, None] * dist)
    logits = jnp.where(valid[..., None] & (dist >= 0), logits, -jnp.inf)
    p = jax.nn.softmax(logits.reshape(n, h, q_len, -1), axis=-1).reshape(logits.shape)
    return jnp.einsum('nhqsj,nhqsjd->nhqd', p.astype(vb.dtype), vsel)


def post_mix(x, ya, yb, yc, gl, b_gate, w_out, norm2_g, w_up, w_dw_f, b_dw_f, w_down, ffn_prev):
    n, t = x.shape[:2]
    gates = jax.nn.sigmoid((gl + b_gate).reshape(n, t, N_BRANCH, D_MODEL))
    merged = gates[..., 0, :] * ya + gates[..., 1, :] * yb + gates[..., 2, :] * yc
    x1 = x + merged @ w_out
    up, new_prev = causal_dwconv(rmsnorm(x1, norm2_g) @ w_up, ffn_prev, w_dw_f, b_dw_f)
    a, b = jnp.split(up, 2, axis=-1)
    return x1 + (jax.nn.silu(a) * b) @ w_down, new_prev


def setup_inputs(seed: int = 0) -> dict:
    key = jax.random.key(seed)
    ks = iter(jax.random.split(key, 40))

    def nrm(shape, scale=1.0):
        return jax.random.normal(next(ks), shape, jnp.float32) * scale

    def gain(m):
        return 1.0 + 0.05 * nrm((m,))

    n_pages = PAST_LEN // PAGE_SIZE
    n_pool = (DEC_BATCH * n_pages * 5) // 4
    perm = jax.random.permutation(next(ks), n_pool)[: DEC_BATCH * n_pages]
    page_table = perm.reshape(DEC_BATCH, n_pages).astype(jnp.int32)
    return {
        'x_prompt': nrm((BATCH, SEQ, D_MODEL)),
        'x_sample': nrm((DEC_BATCH, DEC_SEQ, D_MODEL)),
        'mem_prompt': nrm((BATCH, N_MEM, D_MODEL)),
        'cache_k': nrm((n_pool, PAGE_SIZE, N_HEADS, HEAD_DIM)),
        'cache_v': nrm((n_pool, PAGE_SIZE, N_HEADS, HEAD_DIM)),
        'page_table': page_table,
        'state_conv': nrm((DEC_BATCH, CONV_WIDTH - 1, CONV_CH), 0.5),
        'state_ffn_conv': nrm((DEC_BATCH, FFN_CONV_WIDTH - 1, 2 * D_FF)),
        'cache_mem_k': nrm((DEC_BATCH, N_MEM, MEM_HEADS, MEM_HEAD_DIM)),
        'cache_mem_v': nrm((DEC_BATCH, N_MEM, MEM_HEADS, MEM_HEAD_DIM)),
        'norm1_g': gain(D_MODEL),
        'w_in': nrm((D_MODEL, N_IN), D_MODEL ** -0.5),
        'b_gate': nrm((N_BRANCH * D_MODEL,), 0.01),
        'w_dw_a': nrm((CONV_WIDTH, CONV_CH), CONV_WIDTH ** -0.5),
        'b_dw_a': nrm((CONV_CH,), 0.01),
        'ln_a_g': gain(CONV_CH),
        'ln_a_b': nrm((CONV_CH,), 0.01),
        'w_proj_a': nrm((CONV_CH, D_MODEL), CONV_CH ** -0.5),
        'q_norm_g': gain(HEAD_DIM),
        'k_norm_g': gain(HEAD_DIM),
        'w_proj_b': nrm((ATTN_W, D_MODEL), ATTN_W ** -0.5),
        'mem_norm_g': gain(D_MODEL),
        'w_mem_kv': nrm((D_MODEL, 2 * MEM_W), D_MODEL ** -0.5),
        'mq_norm_g': gain(MEM_HEAD_DIM),
        'mk_norm_g': gain(MEM_HEAD_DIM),
        'w_proj_c': nrm((MEM_W, D_MODEL), MEM_W ** -0.5),
        'w_out': nrm((D_MODEL, D_MODEL), D_MODEL ** -0.5),
        'norm2_g': gain(D_MODEL),
        'w_up': nrm((D_MODEL, 2 * D_FF), D_MODEL ** -0.5),
        'w_dw_f': nrm((FFN_CONV_WIDTH, 2 * D_FF), FFN_CONV_WIDTH ** -0.5),
        'b_dw_f': nrm((2 * D_FF,), 0.01),
        'w_down': nrm((D_FF, D_MODEL), D_FF ** -0.5),
    }


def reference(x_prompt, x_sample, mem_prompt, cache_k, cache_v, page_table, state_conv,
              state_ffn_conv, cache_mem_k, cache_mem_v, norm1_g, w_in, b_gate, w_dw_a, b_dw_a,
              ln_a_g, ln_a_b, w_proj_a, q_norm_g, k_norm_g, w_proj_b, mem_norm_g, w_mem_kv,
              mq_norm_g, mk_norm_g, w_proj_c, w_out, norm2_g, w_up, w_dw_f, b_dw_f, w_down):
    slopes = alibi_slopes()
    h_p, h_s = x_prompt, x_sample
    for _ in range(DEPTH):
        n, t = h_p.shape[:2]
        u_p, q_p, k_p, v_p, mq_p, gl_p = pre_mix(h_p, norm1_g, w_in, q_norm_g, k_norm_g, mq_norm_g)
        ya_p, conv_p = conv_branch(u_p, jnp.zeros((n, CONV_WIDTH - 1, CONV_CH), u_p.dtype),
                                   w_dw_a, b_dw_a, ln_a_g, ln_a_b, w_proj_a)
        kb_p, vb_p = to_blocks(k_p), to_blocks(v_p)
        km_p = block_means(kb_p)
        n_chunks = t // Q_BLOCK
        qc = q_p.reshape(n, n_chunks, Q_BLOCK, N_HEADS, HEAD_DIM).transpose(1, 0, 3, 2, 4)
        pc = jnp.arange(t, dtype=jnp.int32).reshape(n_chunks, Q_BLOCK)
        oc = lax.map(lambda a: moba_attend(a[0], a[1], kb_p, vb_p, km_p, slopes), (qc, pc))
        yb_p = oc.transpose(1, 0, 3, 2, 4).reshape(n, t, ATTN_W) @ w_proj_b
        mk_p, mv_p = mem_kv(mem_prompt, mem_norm_g, w_mem_kv, mk_norm_g)
        yc_p = mem_attend(mq_p, mk_p, mv_p, w_proj_c)
        y_p, ffn_p = post_mix(h_p, ya_p, yb_p, yc_p, gl_p, b_gate, w_out, norm2_g, w_up, w_dw_f,
                              b_dw_f, w_down,
                              jnp.zeros((n, FFN_CONV_WIDTH - 1, 2 * D_FF), h_p.dtype))
        nd, td = h_s.shape[:2]
        past_len = page_table.shape[1] * PAGE_SIZE
        u_s, q_s, k_s, v_s, mq_s, gl_s = pre_mix(h_s, norm1_g, w_in, q_norm_g, k_norm_g, mq_norm_g)
        ya_s, conv_s = conv_branch(u_s, state_conv, w_dw_a, b_dw_a, ln_a_g, ln_a_b, w_proj_a)
        past_k = cache_k[page_table].reshape(nd, past_len, N_HEADS, HEAD_DIM).astype(k_s.dtype)
        past_v = cache_v[page_table].reshape(nd, past_len, N_HEADS, HEAD_DIM).astype(v_s.dtype)
        kb_s = to_blocks(jnp.concatenate([past_k, k_s], axis=1))
        vb_s = to_blocks(jnp.concatenate([past_v, v_s], axis=1))
        t_s = past_len + jnp.arange(td, dtype=jnp.int32)
        o_s = moba_attend(q_s.transpose(0, 2, 1, 3), t_s, kb_s, vb_s, block_means(kb_s), slopes)
        yb_s = o_s.transpose(0, 2, 1, 3).reshape(nd, td, ATTN_W) @ w_proj_b
        yc_s = mem_attend(mq_s, cache_mem_k, cache_mem_v, w_proj_c)
        y_s, ffn_s = post_mix(h_s, ya_s, yb_s, yc_s, gl_s, b_gate, w_out, norm2_g, w_up, w_dw_f,
                              b_dw_f, w_down, state_ffn_conv)
        h_p, h_s = y_p, y_s
    return (h_p, h_s, k_p, v_p, k_s, v_s, conv_p, conv_s, ffn_p, ffn_s, mk_p, mv_p)
```

```python
import functools

import numpy as np
import jax
import jax.numpy as jnp
from jax import lax
from jax.experimental import pallas as pl
from jax.experimental.pallas import tpu as pltpu

F32 = jnp.float32
BF16 = jnp.bfloat16

EPS = 1e-6
CONV_CH = 512
CONV_WIDTH = 31
N_HEADS = 8
HEAD_DIM = 64
ATTN_W = N_HEADS * HEAD_DIM
MOBA_BLOCK = 256
MOBA_TOPK = 3
MEM_HEADS = 4
MEM_HEAD_DIM = 128
MEM_W = MEM_HEADS * MEM_HEAD_DIM
FFN_CONV_WIDTH = 3
LANES = 128
SUBLANES = 8
TM = MOBA_BLOCK
CONV_HALO = 32
MASK_NEG = -float(2 ** 30)
GROUP = 16
VMEM_LIMIT = 56 * 1024 * 1024

X_SEL = HEAD_DIM
X_RQ = HEAD_DIM + GROUP
X_ONE = X_RQ + 1
X_TQ = X_RQ + 2
X_ONE2 = X_RQ + 3


def _dot(a, b):
    return jnp.dot(a, b, preferred_element_type=F32)


def _dot_nt(a, b):
    return lax.dot_general(a, b, (((1,), (1,)), ((), ())), preferred_element_type=F32)


def _rms_rows(x, g):
    return x * lax.rsqrt(jnp.mean(x * x, axis=-1, keepdims=True) + EPS) * g


def _group_rms(z, bd, g):
    sq = z * z
    hi = sq.astype(BF16)
    lo = (sq - hi.astype(F32)).astype(BF16)
    ms = _dot(hi, bd) + _dot(lo, bd)
    return z * lax.rsqrt(ms + EPS) * g


def _sigmoid(x):
    return 1.0 / (1.0 + jnp.exp(-x))


def _layernorm_silu(c, g, b):
    mu = jnp.mean(c, axis=-1, keepdims=True)
    xc = c - mu
    var = jnp.mean(xc * xc, axis=-1, keepdims=True)
    y = xc * lax.rsqrt(var + EPS) * g + b
    return y * _sigmoid(y)


def _premix_kernel(x_ref, g1_ref, w_ref, bd64_ref, bd128_ref, qg_ref, kg_ref, mqg_ref,
                   wdw_ref, bdw_ref, lng_ref, lnb_ref, slope_ref,
                   k_ref, v_ref, km_ref, qa_ref, ka_ref, vb_ref, mq_ref, cact_ref, utail_ref,
                   ubuf):
    t = pl.program_id(1)
    nt = pl.num_programs(1)
    c = CONV_CH
    hn = _rms_rows(x_ref[...], g1_ref[...]).astype(BF16)

    a = _dot(hn, w_ref[:, 0:c])
    g = _dot(hn, w_ref[:, c:2 * c])
    u = a * _sigmoid(g)

    @pl.when(t == 0)
    def _():
        ubuf[0:CONV_HALO, :] = jnp.zeros((CONV_HALO, c), F32)

    ubuf[CONV_HALO:CONV_HALO + TM, :] = u
    acc = jnp.broadcast_to(bdw_ref[...], (TM, c))
    base = CONV_HALO - (CONV_WIDTH - 1)
    for k in range(CONV_WIDTH):
        acc = acc + wdw_ref[k:k + 1, :] * ubuf[base + k:base + k + TM, :]
    ubuf[0:CONV_HALO, :] = ubuf[TM:TM + CONV_HALO, :]
    cact_ref[...] = _layernorm_silu(acc, lng_ref[...], lnb_ref[...]).astype(BF16)

    @pl.when(t == nt - 1)
    def _():
        utail_ref[0] = u[TM - CONV_HALO:, :]

    o = 2 * c
    qn = _group_rms(_dot(hn, w_ref[:, o:o + ATTN_W]), bd64_ref[...], qg_ref[...]) * (HEAD_DIM ** -0.5)
    o += ATTN_W
    kn = _group_rms(_dot(hn, w_ref[:, o:o + ATTN_W]), bd64_ref[...], kg_ref[...])
    o += ATTN_W
    zv = _dot(hn, w_ref[:, o:o + ATTN_W])
    o += ATTN_W
    mq = _group_rms(_dot(hn, w_ref[:, o:o + MEM_W]), bd128_ref[...], mqg_ref[...])

    k_ref[...] = kn
    v_ref[...] = zv
    vb_ref[...] = zv.astype(BF16)
    mq_ref[...] = mq.astype(BF16)
    km_ref[0] = jnp.mean(kn, axis=0, keepdims=True)

    lane = lax.broadcasted_iota(jnp.int32, (TM, LANES), 1)
    rowf = lax.broadcasted_iota(jnp.int32, (TM, LANES), 0).astype(F32)
    tf = t.astype(F32)
    zero = jnp.zeros((TM, LANES), F32)
    eq = jnp.where(lane == X_RQ, rowf,
                   jnp.where(lane == X_ONE, 1.0,
                             jnp.where(lane == X_TQ, tf,
                                       jnp.where(lane == X_ONE2, 1.0, zero))))
    for hp in range(N_HEADS // 2):
        xq = qn[:, LANES * hp:LANES * (hp + 1)]
        xk = kn[:, LANES * hp:LANES * (hp + 1)]
        for sub in range(2):
            h = 2 * hp + sub
            if sub == 1:
                xq = pltpu.roll(xq, HEAD_DIM, 1)
                xk = pltpu.roll(xk, HEAD_DIM, 1)
            sl = slope_ref[:, LANES * h:LANES * (h + 1)]
            ek = jnp.where(lane == X_SEL + t, MASK_NEG, zero)
            ek = jnp.where(lane == X_RQ, -sl,
                           jnp.where(lane == X_ONE, sl * rowf,
                                     jnp.where(lane == X_TQ, -sl * MOBA_BLOCK,
                                               jnp.where(lane == X_ONE2, sl * (MOBA_BLOCK * tf), ek))))
            qa_ref[:, LANES * h:LANES * (h + 1)] = jnp.where(lane < HEAD_DIM, xq, eq).astype(BF16)
            ka_ref[:, LANES * h:LANES * (h + 1)] = jnp.where(lane < HEAD_DIM, xk, ek).astype(BF16)


def _moba_kernel(qa_ref, ka_ref, vb_ref, kmbd_ref, o_ref):
    i = pl.program_id(1)
    lane = lax.broadcasted_iota(jnp.int32, (TM, LANES), 1)
    blk = lane & (GROUP - 1)
    past = blk < i

    gate = _dot(qa_ref[...], kmbd_ref[0])
    g = jnp.where(past, gate, -jnp.inf)
    rank = jnp.zeros((TM, LANES), F32)
    for s in range(1, GROUP):
        left = pltpu.roll(g, LANES - s, 1)
        right = pltpu.roll(g, GROUP - s, 1)
        wrapped = blk >= GROUP - s
        partner = jnp.where(wrapped, right, left)
        beats = jnp.where(wrapped, jnp.where(partner >= g, 1.0, 0.0), jnp.where(partner > g, 1.0, 0.0))
        rank = rank + beats
    nsel = jnp.where(past, jnp.where(rank >= MOBA_TOPK, 1.0, 0.0), 0.0)

    row = lax.broadcasted_iota(jnp.int32, (TM, TM), 0)
    col = lax.broadcasted_iota(jnp.int32, (TM, TM), 1)
    causal = col <= row
    own = pl.multiple_of(i * TM, TM)

    for hp in range(N_HEADS // 2):
        vcol = LANES * hp
        outs = []
        for sub in range(2):
            h = 2 * hp + sub
            hcol = LANES * h
            shift = (X_SEL - GROUP * h) % LANES
            sel = pltpu.roll(nsel, shift, 1) if shift else nsel
            qh = qa_ref[:, hcol:hcol + LANES].astype(F32)
            qh = jnp.where(lane >= X_SEL, jnp.where(lane < X_SEL + GROUP, sel, qh), qh).astype(BF16)

            s = _dot_nt(qh, ka_ref[pl.ds(own, TM), hcol:hcol + LANES])
            s = jnp.where(causal, s, -jnp.inf)
            m = jnp.max(s, axis=-1, keepdims=True)
            p = jnp.exp(s - m)
            l = jnp.sum(p, axis=-1, keepdims=True)
            acc = _dot(p.astype(BF16), vb_ref[pl.ds(own, TM), vcol:vcol + LANES])

            def body(j, carry, qh=qh, hcol=hcol, vcol=vcol):
                m, l, acc = carry
                off = pl.multiple_of(j * TM, TM)
                s = _dot_nt(qh, ka_ref[pl.ds(off, TM), hcol:hcol + LANES])
                m_new = jnp.maximum(m, jnp.max(s, axis=-1, keepdims=True))
                alpha = jnp.exp(m - m_new)
                p = jnp.exp(s - m_new)
                l = alpha * l + jnp.sum(p, axis=-1, keepdims=True)
                acc = alpha * acc + _dot(p.astype(BF16), vb_ref[pl.ds(off, TM), vcol:vcol + LANES])
                return m_new, l, acc

            m, l, acc = lax.fori_loop(0, i, body, (m, l, acc))
            outs.append(acc / l)
        o_ref[:, vcol:vcol + LANES] = jnp.where(lane < HEAD_DIM, outs[0], outs[1]).astype(BF16)


def _memkv_kernel(mem_ref, g_ref, w_ref, bd128_ref, mkg_ref, mk_ref, mv_ref, mkb_ref, mvb_ref):
    hn = _rms_rows(mem_ref[...], g_ref[...]).astype(BF16)
    mk = _group_rms(_dot(hn, w_ref[:, 0:MEM_W]), bd128_ref[...], mkg_ref[...])
    mv = _dot(hn, w_ref[:, MEM_W:2 * MEM_W])
    mk_ref[...] = mk
    mv_ref[...] = mv
    mkb_ref[...] = mk.astype(BF16)
    mvb_ref[...] = mv.astype(BF16)


def _mem_attend_rows(mq, mk, mv):
    outs = []
    for hh in range(MEM_HEADS):
        sl = slice(MEM_HEAD_DIM * hh, MEM_HEAD_DIM * (hh + 1))
        s = _dot_nt(mq[:, sl], mk[:, sl]) * (MEM_HEAD_DIM ** -0.5)
        m = jnp.max(s, axis=-1, keepdims=True)
        p = jnp.exp(s - m)
        l = jnp.sum(p, axis=-1, keepdims=True)
        outs.append(_dot(p.astype(BF16), mv[:, sl]) / l)
    return jnp.concatenate(outs, axis=-1)


def _merge_out(x, gl, ya, yb, yc, wout):
    d = x.shape[-1]
    merged = (_sigmoid(gl[:, 0:d]) * ya + _sigmoid(gl[:, d:2 * d]) * yb + _sigmoid(gl[:, 2 * d:3 * d]) * yc)
    return x + _dot(merged.astype(BF16), wout)


def _postmix_kernel(x_ref, cact_ref, ob_ref, mq_ref, mk_ref, mv_ref, g1_ref, wg_ref, bg_ref,
                    wpa_ref, wpb_ref, wpc_ref, wout_ref, x1_ref):
    x = x_ref[...]
    d = x.shape[-1]
    hn = _rms_rows(x, g1_ref[...]).astype(BF16)
    oc = _mem_attend_rows(mq_ref[...], mk_ref[...], mv_ref[...]).astype(BF16)
    branches = (_dot(cact_ref[...], wpa_ref[...]), _dot(ob_ref[...], wpb_ref[...]), _dot(oc, wpc_ref[...]))
    merged = jnp.zeros_like(x)
    for br in range(3):
        gl = _dot(hn, wg_ref[:, d * br:d * (br + 1)]) + bg_ref[:, d * br:d * (br + 1)]
        merged = merged + _sigmoid(gl) * branches[br]
    x1_ref[...] = x + _dot(merged.astype(BF16), wout_ref[...])


def _ffn_kernel(x1_ref, g2_ref, wup_ref, wdw_ref, bdw_ref, wdown_ref, y_ref, tail_ref, upbuf, *, chunk):
    t = pl.program_id(1)
    nt = pl.num_programs(1)
    x1 = x1_ref[...]
    dff = wdown_ref.shape[0]
    hn = _rms_rows(x1, g2_ref[...]).astype(BF16)

    @pl.when(t == 0)
    def _():
        upbuf[0:SUBLANES, :] = jnp.zeros((SUBLANES, 2 * dff), F32)

    for c in range(0, 2 * dff, chunk):
        upbuf[SUBLANES:SUBLANES + TM, c:c + chunk] = _dot(hn, wup_ref[:, c:c + chunk])

    def conv(c):
        out = bdw_ref[:, c:c + chunk]
        for k in range(FFN_CONV_WIDTH):
            r0 = SUBLANES - (FFN_CONV_WIDTH - 1) + k
            out = out + wdw_ref[k:k + 1, c:c + chunk] * upbuf[r0:r0 + TM, c:c + chunk]
        return out

    y = x1
    for c in range(0, dff, chunk):
        a = conv(c)
        b = conv(dff + c)
        act = (a * _sigmoid(a) * b).astype(BF16)
        y = y + _dot(act, wdown_ref[c:c + chunk, :])
    y_ref[...] = y

    @pl.when(t == nt - 1)
    def _():
        tail_ref[0] = upbuf[TM:TM + SUBLANES, :]

    upbuf[0:SUBLANES, :] = upbuf[TM:TM + SUBLANES, :]


def _premix_s_kernel(x_ref, g1_ref, w_ref, bd64_ref, bd128_ref, qg_ref, kg_ref, mqg_ref,
                     st_ref, wdw_ref, bdw_ref, lng_ref, lnb_ref,
                     u_ref, q_ref, k_ref, v_ref, mq_ref, gl_ref, cact_ref):
    c = CONV_CH
    hn = _rms_rows(x_ref[...], g1_ref[...]).astype(BF16)
    a = _dot(hn, w_ref[:, 0:c])
    g = _dot(hn, w_ref[:, c:2 * c])
    u = a * _sigmoid(g)
    u_ref[...] = u
    acc = bdw_ref[...] + wdw_ref[CONV_WIDTH - 1:CONV_WIDTH, :] * u
    for k in range(CONV_WIDTH - 1):
        acc = acc + wdw_ref[k:k + 1, :] * st_ref[k]
    cact_ref[...] = _layernorm_silu(acc, lng_ref[...], lnb_ref[...]).astype(BF16)

    o = 2 * c
    q_ref[...] = _group_rms(_dot(hn, w_ref[:, o:o + ATTN_W]), bd64_ref[...], qg_ref[...])
    o += ATTN_W
    k_ref[...] = _group_rms(_dot(hn, w_ref[:, o:o + ATTN_W]), bd64_ref[...], kg_ref[...])
    o += ATTN_W
    v_ref[...] = _dot(hn, w_ref[:, o:o + ATTN_W])
    o += ATTN_W
    mq_ref[...] = _group_rms(_dot(hn, w_ref[:, o:o + MEM_W]), bd128_ref[...], mqg_ref[...])
    o += MEM_W
    gl_ref[...] = _dot(hn, w_ref[:, o:])


def _block_gate_kernel(pt_ref, q_ref, *refs, pages_per_step, n_blocks):
    page_refs = refs[:pages_per_step]
    sel_ref = refs[pages_per_step]
    km_scr = refs[pages_per_step + 1]
    n = pl.program_id(0)
    s = pl.program_id(1)
    pages_per_block = 2
    rows = []
    for r in range(0, pages_per_step, pages_per_block):
        tot = jnp.sum(page_refs[r][0], axis=0, keepdims=True) + jnp.sum(page_refs[r + 1][0], axis=0, keepdims=True)
        rows.append(tot * (1.0 / MOBA_BLOCK))
    blocks_per_step = pages_per_step // pages_per_block
    off = pl.multiple_of(s * blocks_per_step, blocks_per_step)
    for r, kmean in enumerate(rows):
        km_scr[pl.ds(off + r, 1), :] = kmean

    @pl.when(s == pl.num_programs(1) - 1)
    def _():
        lane = lax.broadcasted_iota(jnp.int32, (N_HEADS, ATTN_W), 1)
        head = lax.broadcasted_iota(jnp.int32, (N_HEADS, ATTN_W), 0)
        qrow = jnp.broadcast_to(q_ref[pl.ds(n, 1), :], (N_HEADS, ATTN_W))
        qbd = jnp.where((lane >> 6) == head, qrow, 0.0).astype(BF16)
        g = _dot_nt(qbd, km_scr[...].astype(BF16))
        bl = lax.broadcasted_iota(jnp.int32, (N_HEADS, n_blocks), 1)
        rank = jnp.zeros((N_HEADS, n_blocks), F32)
        for b in range(n_blocks):
            other = g[:, b:b + 1]
            rank = rank + jnp.where(bl > b, jnp.where(other >= g, 1.0, 0.0), jnp.where(other > g, 1.0, 0.0))
        lane_o = lax.broadcasted_iota(jnp.int32, (N_HEADS, LANES), 1)
        out = jnp.zeros((N_HEADS, LANES), F32)
        blf = bl.astype(F32)
        for r in range(MOBA_TOPK):
            idx = jnp.sum(jnp.where(rank == float(r), blf, 0.0), axis=-1, keepdims=True)
            out = jnp.where(lane_o == r, idx, out)
        sel_ref[0] = out.astype(jnp.int32)


def _decode_attn_kernel(pt_ref, sel_ref, slope_ref, q_ref, kown_ref, vown_ref, *refs, past_len):
    n_tiles = 2 * MOBA_TOPK * 2
    k_refs = refs[:n_tiles]
    v_refs = refs[n_tiles:2 * n_tiles]
    o_ref = refs[2 * n_tiles]
    n = pl.program_id(0)
    hp = pl.program_id(1)
    lane = lax.broadcasted_iota(jnp.int32, (1, LANES), 1)
    lanef = lane.astype(F32)
    qpair = q_ref[0] * (HEAD_DIM ** -0.5)
    kown = kown_ref[0].astype(BF16).astype(F32)
    vown = vown_ref[0].astype(BF16).astype(F32)
    outs = []
    for sub in range(2):
        h = 2 * hp + sub
        slope = slope_ref[h]
        mine = (lane < HEAD_DIM) if sub == 0 else (lane >= HEAD_DIM)
        qz = jnp.where(mine, qpair, 0.0).astype(BF16)
        q8 = jnp.broadcast_to(qz, (SUBLANES, LANES))
        logits = []
        for r in range(MOBA_TOPK):
            blk = sel_ref[(n * N_HEADS + h) * MOBA_TOPK + r]
            for half in range(2):
                kt = k_refs[(sub * MOBA_TOPK + r) * 2 + half][0].astype(BF16)
                srow = _dot_nt(q8, kt)[0:1, :]
                pos0 = (blk * MOBA_BLOCK + half * LANES).astype(F32)
                dist = float(past_len) - (pos0 + lanef)
                logits.append(srow - slope * dist)
        s_own = jnp.sum(qz.astype(F32) * kown, axis=-1, keepdims=True)
        m = s_own
        for lg in logits:
            m = jnp.maximum(m, jnp.max(lg, axis=-1, keepdims=True))
        p_own = jnp.exp(s_own - m)
        l = p_own
        acc = p_own.astype(BF16).astype(F32) * vown
        for idx, lg in enumerate(logits):
            p = jnp.exp(lg - m)
            l = l + jnp.sum(p, axis=-1, keepdims=True)
            p8 = jnp.broadcast_to(p.astype(BF16), (SUBLANES, LANES))
            acc = acc + _dot(p8, v_refs[idx + sub * 2 * MOBA_TOPK][0].astype(BF16))[0:1, :]
        outs.append(acc / l)
    o_ref[0] = jnp.where(lane < HEAD_DIM, outs[0], outs[1])


def _memattn_s_kernel(mq_ref, mk_ref, mv_ref, o_ref):
    mq = jnp.broadcast_to(mq_ref[0], (SUBLANES, MEM_W)).astype(BF16)
    o = _mem_attend_rows(mq, mk_ref[0].astype(BF16), mv_ref[0].astype(BF16))
    o_ref[0] = o[0:1, :]


def _post_s_kernel(x_ref, cact_ref, ob_ref, oc_ref, gl_ref, bg_ref, wpa_ref, wpb_ref, wpc_ref, wout_ref,
                   g2_ref, wup_ref, st_ref, wdw_ref, bdw_ref, wdown_ref, y_ref, up_ref):
    x = x_ref[...]
    dff = wdown_ref.shape[0]
    ya = _dot(cact_ref[...], wpa_ref[...])
    yb = _dot(ob_ref[...].astype(BF16), wpb_ref[...])
    yc = _dot(oc_ref[...].astype(BF16), wpc_ref[...])
    x1 = _merge_out(x, gl_ref[...] + bg_ref[...], ya, yb, yc, wout_ref[...])
    up = _dot(_rms_rows(x1, g2_ref[...]).astype(BF16), wup_ref[...])
    up_ref[...] = up
    cv = bdw_ref[...] + wdw_ref[FFN_CONV_WIDTH - 1:FFN_CONV_WIDTH, :] * up
    for k in range(FFN_CONV_WIDTH - 1):
        cv = cv + wdw_ref[k:k + 1, :] * st_ref[k]
    a = cv[:, 0:dff]
    b = cv[:, dff:]
    act = (a * _sigmoid(a) * b).astype(BF16)
    y_ref[...] = x1 + _dot(act, wdown_ref[...])


def _resident(shape):
    nd = len(shape)
    return pl.BlockSpec(shape, lambda *_: (0,) * nd, pipeline_mode=pl.Buffered(1))


def _params(n_axes):
    return pltpu.CompilerParams(dimension_semantics=("arbitrary",) * n_axes, vmem_limit_bytes=VMEM_LIMIT)


def kernel(x_prompt, x_sample, mem_prompt, cache_k, cache_v, page_table, state_conv, state_ffn_conv, cache_mem_k, cache_mem_v, norm1_g, w_in, b_gate, w_dw_a, b_dw_a, ln_a_g, ln_a_b, w_proj_a, q_norm_g, k_norm_g, w_proj_b, mem_norm_g, w_mem_kv, mq_norm_g, mk_norm_g, w_proj_c, w_out, norm2_g, w_up, w_dw_f, b_dw_f, w_down):
    n, t, d = x_prompt.shape
    nd, td, _ = x_sample.shape
    n_mem = mem_prompt.shape[1]
    n_pool, page_size = cache_k.shape[:2]
    n_pages = page_table.shape[1]
    past_len = n_pages * page_size
    dff = w_down.shape[0]
    nt = t // TM
    n_mix = 2 * CONV_CH + 3 * ATTN_W + MEM_W
    assert t % TM == 0 and nt <= GROUP and td == 1
    assert past_len % MOBA_BLOCK == 0 and MOBA_BLOCK == 2 * page_size and page_size == LANES
    n_blocks = past_len // MOBA_BLOCK

    row = lambda v: v.reshape(1, -1).astype(F32)
    w_in_b = w_in.astype(BF16)
    w_mix, w_gate = w_in_b[:, :n_mix], w_in_b[:, n_mix:]
    wpa, wpb, wpc = w_proj_a.astype(BF16), w_proj_b.astype(BF16), w_proj_c.astype(BF16)
    wout, wup, wdown = w_out.astype(BF16), w_up.astype(BF16), w_down.astype(BF16)
    g1, g2 = row(norm1_g), row(norm2_g)
    qg = row(jnp.tile(q_norm_g, N_HEADS))
    kg = row(jnp.tile(k_norm_g, N_HEADS))
    mqg = row(jnp.tile(mq_norm_g, MEM_HEADS))
    mkg = row(jnp.tile(mk_norm_g, MEM_HEADS))
    bdw_a, lng, lnb, bg, bdw_f = row(b_dw_a), row(ln_a_g), row(ln_a_b), row(b_gate), row(b_dw_f)
    grp = np.arange(ATTN_W)
    bd64 = jnp.asarray((grp[:, None] // HEAD_DIM == grp[None, :] // HEAD_DIM) / HEAD_DIM, BF16)
    bd128 = jnp.asarray((grp[:, None] // MEM_HEAD_DIM == grp[None, :] // MEM_HEAD_DIM) / MEM_HEAD_DIM, BF16)
    slopes = 2.0 ** (-8.0 * np.arange(1, N_HEADS + 1) / N_HEADS)
    slope_lanes = jnp.asarray(np.repeat(slopes, LANES)[None, :], F32)

    xp = x_prompt.reshape(n * t, d)
    tile = lambda w: pl.BlockSpec((TM, w), lambda b, i: (b * nt + i, 0))
    aw = N_HEADS * LANES
    k_p, v_p, km, qa, ka, vb, mq, cact, utail = pl.pallas_call(
        _premix_kernel,
        grid=(n, nt),
        in_specs=[tile(d), _resident((1, d)), _resident((d, n_mix)), _resident((ATTN_W, ATTN_W)),
                  _resident((MEM_W, MEM_W)), _resident((1, ATTN_W)), _resident((1, ATTN_W)),
                  _resident((1, MEM_W)), _resident((CONV_WIDTH, CONV_CH)), _resident((1, CONV_CH)),
                  _resident((1, CONV_CH)), _resident((1, CONV_CH)), _resident((1, aw))],
        out_specs=[tile(ATTN_W), tile(ATTN_W),
                   pl.BlockSpec((1, 1, ATTN_W), lambda b, i: (b * nt + i, 0, 0)),
                   tile(aw), tile(aw), tile(ATTN_W), tile(MEM_W), tile(CONV_CH),
                   pl.BlockSpec((1, CONV_HALO, CONV_CH), lambda b, i: (b, 0, 0))],
        out_shape=[jax.ShapeDtypeStruct((n * t, ATTN_W), F32), jax.ShapeDtypeStruct((n * t, ATTN_W), F32),
                   jax.ShapeDtypeStruct((n * nt, 1, ATTN_W), F32),
                   jax.ShapeDtypeStruct((n * t, aw), BF16), jax.ShapeDtypeStruct((n * t, aw), BF16),
                   jax.ShapeDtypeStruct((n * t, ATTN_W), BF16), jax.ShapeDtypeStruct((n * t, MEM_W), BF16),
                   jax.ShapeDtypeStruct((n * t, CONV_CH), BF16),
                   jax.ShapeDtypeStruct((n, CONV_HALO, CONV_CH), F32)],
        scratch_shapes=[pltpu.VMEM((CONV_HALO + TM, CONV_CH), F32)],
        compiler_params=_params(2),
        name="premix",
    )(xp, g1, w_mix, bd64, bd128, qg, kg, mqg, w_dw_a, bdw_a, lng, lnb, slope_lanes)

    km4 = km.reshape(n, nt, N_HEADS, HEAD_DIM).transpose(0, 2, 3, 1)
    km4 = jnp.pad(km4, ((0, 0), (0, 0), (0, LANES - HEAD_DIM), (0, GROUP - nt)))
    kmbd = (km4[:, :, :, None, :] * jnp.eye(N_HEADS, dtype=F32)[None, :, None, :, None])
    kmbd = kmbd.reshape(n, aw, N_HEADS * GROUP).astype(BF16)

    seq = lambda w: pl.BlockSpec((t, w), lambda b, i: (b, 0))
    ob = pl.pallas_call(
        _moba_kernel,
        grid=(n, nt),
        in_specs=[tile(aw), seq(aw), seq(ATTN_W), pl.BlockSpec((1, aw, N_HEADS * GROUP), lambda b, i: (b, 0, 0))],
        out_specs=tile(ATTN_W),
        out_shape=jax.ShapeDtypeStruct((n * t, ATTN_W), BF16),
        compiler_params=_params(2),
        name="moba",
    )(qa, ka, vb, kmbd)

    mem = mem_prompt.reshape(n * n_mem, d)
    mtile = lambda w: pl.BlockSpec((n_mem, w), lambda b: (b, 0))
    mk_p, mv_p, mkb, mvb = pl.pallas_call(
        _memkv_kernel,
        grid=(n,),
        in_specs=[mtile(d), _resident((1, d)), _resident((d, 2 * MEM_W)), _resident((MEM_W, MEM_W)),
                  _resident((1, MEM_W))],
        out_specs=[mtile(MEM_W)] * 4,
        out_shape=[jax.ShapeDtypeStruct((n * n_mem, MEM_W), F32)] * 2
        + [jax.ShapeDtypeStruct((n * n_mem, MEM_W), BF16)] * 2,
        compiler_params=_params(1),
        name="memkv",
    )(mem, row(mem_norm_g), w_mem_kv.astype(BF16), bd128, mkg)

    memb = pl.BlockSpec((n_mem, MEM_W), lambda b, i: (b, 0))
    x1 = pl.pallas_call(
        _postmix_kernel,
        grid=(n, nt),
        in_specs=[tile(d), tile(CONV_CH), tile(ATTN_W), tile(MEM_W), memb, memb, _resident((1, d)),
                  _resident((d, 3 * d)), _resident((1, 3 * d)), _resident((CONV_CH, d)), _resident((ATTN_W, d)),
                  _resident((MEM_W, d)), _resident((d, d))],
        out_specs=tile(d),
        out_shape=jax.ShapeDtypeStruct((n * t, d), F32),
        compiler_params=_params(2),
        name="postmix",
    )(xp, cact, ob, mq, mkb, mvb, g1, w_gate, bg, wpa, wpb, wpc, wout)

    chunk = dff // 2
    assert chunk % LANES == 0
    y_p, ftail = pl.pallas_call(
        functools.partial(_ffn_kernel, chunk=chunk),
        grid=(n, nt),
        in_specs=[tile(d), _resident((1, d)), _resident((d, 2 * dff)), _resident((FFN_CONV_WIDTH, 2 * dff)),
                  _resident((1, 2 * dff)), _resident((dff, d))],
        out_specs=[tile(d), pl.BlockSpec((1, SUBLANES, 2 * dff), lambda b, i: (b, 0, 0))],
        out_shape=[jax.ShapeDtypeStruct((n * t, d), F32), jax.ShapeDtypeStruct((n, SUBLANES, 2 * dff), F32)],
        scratch_shapes=[pltpu.VMEM((SUBLANES + TM, 2 * dff), F32)],
        compiler_params=_params(2),
        name="ffn",
    )(x1, g2, wup, w_dw_f, bdw_f, wdown)

    xs = x_sample.reshape(nd, d)
    st_conv = state_conv.transpose(1, 0, 2)
    st_ffn = state_ffn_conv.transpose(1, 0, 2)
    vm = pltpu.CompilerParams(vmem_limit_bytes=VMEM_LIMIT)
    u_s, q_s, k_s, v_s, mq_s, gl_s, cact_s = pl.pallas_call(
        _premix_s_kernel,
        out_shape=[jax.ShapeDtypeStruct((nd, CONV_CH), F32), jax.ShapeDtypeStruct((nd, ATTN_W), F32),
                   jax.ShapeDtypeStruct((nd, ATTN_W), F32), jax.ShapeDtypeStruct((nd, ATTN_W), F32),
                   jax.ShapeDtypeStruct((nd, MEM_W), F32), jax.ShapeDtypeStruct((nd, 3 * d), F32),
                   jax.ShapeDtypeStruct((nd, CONV_CH), BF16)],
        compiler_params=vm,
        name="premix_s",
    )(xs, g1, w_in_b, bd64, bd128, qg, kg, mqg, st_conv, w_dw_a, bdw_a, lng, lnb)

    ck = cache_k.reshape(n_pool, page_size, ATTN_W)
    cv = cache_v.reshape(n_pool, page_size, ATTN_W)
    pt_flat = page_table.reshape(-1).astype(jnp.int32)
    pages_per_step = 16
    assert n_pages % pages_per_step == 0
    page_spec = lambda r: pl.BlockSpec(
        (1, page_size, ATTN_W), lambda b, s, pt: (pt[b * n_pages + s * pages_per_step + r], 0, 0))
    sel = pl.pallas_call(
        functools.partial(_block_gate_kernel, pages_per_step=pages_per_step, n_blocks=n_blocks),
        grid_spec=pltpu.PrefetchScalarGridSpec(
            num_scalar_prefetch=1,
            grid=(nd, n_pages // pages_per_step),
            in_specs=[pl.BlockSpec((nd, ATTN_W), lambda b, s, pt: (0, 0))]
            + [page_spec(r) for r in range(pages_per_step)],
            out_specs=pl.BlockSpec((1, N_HEADS, LANES), lambda b, s, pt: (b, 0, 0)),
            scratch_shapes=[pltpu.VMEM((n_blocks, ATTN_W), F32)]),
        out_shape=jax.ShapeDtypeStruct((nd, N_HEADS, LANES), jnp.int32),
        compiler_params=_params(2),
        name="block_gate",
    )(pt_flat, q_s, *([ck] * pages_per_step))
    sel_flat = sel[:, :, :MOBA_TOPK].reshape(-1)

    n_pairs = N_HEADS // 2
    pair = lambda a: a.reshape(nd * n_pairs, 1, LANES)
    pair_spec = pl.BlockSpec((1, 1, LANES), lambda b, hp, pt, sl: (b * n_pairs + hp, 0, 0))

    def kv_spec(sub, r, half):
        def index(b, hp, pt, sl):
            blk = sl[(b * N_HEADS + 2 * hp + sub) * MOBA_TOPK + r]
            return (pt[b * n_pages + 2 * blk + half], 0, hp)
        return pl.BlockSpec((1, page_size, LANES), index)

    kv_specs = [kv_spec(sub, r, half) for sub in range(2) for r in range(MOBA_TOPK) for half in range(2)]
    ob_s = pl.pallas_call(
        functools.partial(_decode_attn_kernel, past_len=past_len),
        grid_spec=pltpu.PrefetchScalarGridSpec(
            num_scalar_prefetch=2,
            grid=(nd, n_pairs),
            in_specs=[pl.BlockSpec(memory_space=pltpu.SMEM), pair_spec, pair_spec, pair_spec]
            + kv_specs + kv_specs,
            out_specs=pair_spec),
        out_shape=jax.ShapeDtypeStruct((nd * n_pairs, 1, LANES), F32),
        compiler_params=_params(2),
        name="decode_attn",
    )(pt_flat, sel_flat, jnp.asarray(slopes, F32), pair(q_s), pair(k_s), pair(v_s),
      *([ck] * len(kv_specs)), *([cv] * len(kv_specs)))
    ob_s = ob_s.reshape(nd, ATTN_W)

    cm_spec = pl.BlockSpec((1, n_mem, MEM_W), lambda b: (b, 0, 0))
    one_spec = pl.BlockSpec((1, 1, MEM_W), lambda b: (b, 0, 0))
    oc_s = pl.pallas_call(
        _memattn_s_kernel,
        grid=(nd,),
        in_specs=[one_spec, cm_spec, cm_spec],
        out_specs=one_spec,
        out_shape=jax.ShapeDtypeStruct((nd, 1, MEM_W), F32),
        compiler_params=_params(1),
        name="memattn_s",
    )(mq_s.reshape(nd, 1, MEM_W), cache_mem_k.reshape(nd, n_mem, MEM_W), cache_mem_v.reshape(nd, n_mem, MEM_W))
    oc_s = oc_s.reshape(nd, MEM_W)

    y_s, up_s = pl.pallas_call(
        _post_s_kernel,
        out_shape=[jax.ShapeDtypeStruct((nd, d), F32), jax.ShapeDtypeStruct((nd, 2 * dff), F32)],
        compiler_params=vm,
        name="post_s",
    )(xs, cact_s, ob_s, oc_s, gl_s, bg, wpa, wpb, wpc, wout, g2, wup, st_ffn, w_dw_f, bdw_f, wdown)

    heads = lambda a, b, s: a.reshape(b, s, N_HEADS, HEAD_DIM)
    conv_p = utail[:, CONV_HALO - (CONV_WIDTH - 1):, :]
    conv_s = jnp.concatenate([state_conv[:, 1:, :], u_s[:, None, :]], axis=1)
    ffn_p = ftail[:, SUBLANES - (FFN_CONV_WIDTH - 1):, :]
    ffn_s = jnp.concatenate([state_ffn_conv[:, 1:, :], up_s[:, None, :]], axis=1)
    return (y_p.reshape(n, t, d), y_s.reshape(nd, td, d),
            heads(k_p, n, t), heads(v_p, n, t), heads(k_s, nd, td), heads(v_s, nd, td),
            conv_p, conv_s, ffn_p, ffn_s,
            mk_p.reshape(n, n_mem, MEM_HEADS, MEM_HEAD_DIM), mv_p.reshape(n, n_mem, MEM_HEADS, MEM_HEAD_DIM))
```

```python
import functools

import numpy as np
import jax
import jax.numpy as jnp
from jax import lax
from jax.experimental import pallas as pl
from jax.experimental.pallas import tpu as pltpu

F32 = jnp.float32
BF16 = jnp.bfloat16

EPS = 1e-6
CONV_CH = 512
CONV_WIDTH = 31
N_HEADS = 8
HEAD_DIM = 64
ATTN_W = N_HEADS * HEAD_DIM
MOBA_BLOCK = 256
MOBA_TOPK = 3
MEM_HEADS = 4
MEM_HEAD_DIM = 128
MEM_W = MEM_HEADS * MEM_HEAD_DIM
FFN_CONV_WIDTH = 3
LANES = 128
SUBLANES = 8
TM = MOBA_BLOCK
CONV_HALO = 32
MASK_NEG = -float(2 ** 30)
GROUP = 16
V_ROWS = HEAD_DIM + 16
VMEM_LIMIT = 56 * 1024 * 1024

X_SEL = HEAD_DIM
X_RQ = HEAD_DIM + GROUP
X_ONE = X_RQ + 1
X_TQ = X_RQ + 2
X_ONE2 = X_RQ + 3


def _dot(a, b):
    return jnp.dot(a, b, preferred_element_type=F32)


def _dot_nt(a, b):
    return lax.dot_general(a, b, (((1,), (1,)), ((), ())), preferred_element_type=F32)


def _rms_rows(x, g):
    return x * lax.rsqrt(jnp.mean(x * x, axis=-1, keepdims=True) + EPS) * g


def _group_rms(z, bd, g):
    sq = z * z
    hi = sq.astype(BF16)
    lo = (sq - hi.astype(F32)).astype(BF16)
    ms = _dot(hi, bd) + _dot(lo, bd)
    return z * lax.rsqrt(ms + EPS) * g


def _sigmoid(x):
    return 1.0 / (1.0 + jnp.exp(-x))


def _layernorm_silu(c, g, b):
    mu = jnp.mean(c, axis=-1, keepdims=True)
    xc = c - mu
    var = jnp.mean(xc * xc, axis=-1, keepdims=True)
    y = xc * lax.rsqrt(var + EPS) * g + b
    return y * _sigmoid(y)


def _premix_kernel(x_ref, g1_ref, w_ref, bd64_ref, bd128_ref, qg_ref, kg_ref, mqg_ref,
                   wdw_ref, bdw_ref, lng_ref, lnb_ref, slope_ref,
                   k_ref, v_ref, km_ref, qt_ref, ka_ref, vt_ref, mq_ref, cact_ref, utail_ref,
                   ubuf):
    t = pl.program_id(1)
    nt = pl.num_programs(1)
    c = CONV_CH
    hn = _rms_rows(x_ref[...], g1_ref[...]).astype(BF16)

    a = _dot(hn, w_ref[:, 0:c])
    g = _dot(hn, w_ref[:, c:2 * c])
    u = a * _sigmoid(g)

    @pl.when(t == 0)
    def _():
        ubuf[0:CONV_HALO, :] = jnp.zeros((CONV_HALO, c), F32)

    ubuf[CONV_HALO:CONV_HALO + TM, :] = u
    acc = jnp.broadcast_to(bdw_ref[...], (TM, c))
    base = CONV_HALO - (CONV_WIDTH - 1)
    for k in range(CONV_WIDTH):
        acc = acc + wdw_ref[k:k + 1, :] * ubuf[base + k:base + k + TM, :]
    ubuf[0:CONV_HALO, :] = ubuf[TM:TM + CONV_HALO, :]
    cact_ref[...] = _layernorm_silu(acc, lng_ref[...], lnb_ref[...]).astype(BF16)

    @pl.when(t == nt - 1)
    def _():
        utail_ref[0] = u[TM - CONV_HALO:, :]

    o = 2 * c
    qn = _group_rms(_dot(hn, w_ref[:, o:o + ATTN_W]), bd64_ref[...], qg_ref[...]) * (HEAD_DIM ** -0.5)
    o += ATTN_W
    kn = _group_rms(_dot(hn, w_ref[:, o:o + ATTN_W]), bd64_ref[...], kg_ref[...])
    o += ATTN_W
    zv = _dot(hn, w_ref[:, o:o + ATTN_W])
    o += ATTN_W
    mq = _group_rms(_dot(hn, w_ref[:, o:o + MEM_W]), bd128_ref[...], mqg_ref[...])

    ones_rows = jnp.where(lax.broadcasted_iota(jnp.int32, (V_ROWS - HEAD_DIM, TM), 0) == 0, 1.0, 0.0).astype(BF16)
    for cb in range(ATTN_W // LANES):
        k_ref[LANES * cb:LANES * (cb + 1), :] = kn[:, LANES * cb:LANES * (cb + 1)].T
        vt = zv[:, LANES * cb:LANES * (cb + 1)].T
        v_ref[LANES * cb:LANES * (cb + 1), :] = vt
        for sub in range(2):
            r0 = V_ROWS * (2 * cb + sub)
            vt_ref[0, r0:r0 + HEAD_DIM, :] = vt[HEAD_DIM * sub:HEAD_DIM * (sub + 1), :].astype(BF16)
            vt_ref[0, r0 + HEAD_DIM:r0 + V_ROWS, :] = ones_rows
    mq_ref[...] = mq.astype(BF16)
    km_ref[0] = jnp.mean(kn, axis=0, keepdims=True)

    lane = lax.broadcasted_iota(jnp.int32, (TM, LANES), 1)
    rowf = lax.broadcasted_iota(jnp.int32, (TM, LANES), 0).astype(F32)
    tf = t.astype(F32)
    zero = jnp.zeros((TM, LANES), F32)
    eq = jnp.where(lane == X_RQ, rowf,
                   jnp.where(lane == X_ONE, 1.0,
                             jnp.where(lane == X_TQ, tf,
                                       jnp.where(lane == X_ONE2, 1.0, zero))))
    for hp in range(N_HEADS // 2):
        xq = qn[:, LANES * hp:LANES * (hp + 1)]
        xk = kn[:, LANES * hp:LANES * (hp + 1)]
        for sub in range(2):
            h = 2 * hp + sub
            if sub == 1:
                xq = pltpu.roll(xq, HEAD_DIM, 1)
                xk = pltpu.roll(xk, HEAD_DIM, 1)
            sl = slope_ref[:, LANES * h:LANES * (h + 1)]
            ek = jnp.where(lane == X_SEL + t, MASK_NEG, zero)
            ek = jnp.where(lane == X_RQ, -sl,
                           jnp.where(lane == X_ONE, sl * rowf,
                                     jnp.where(lane == X_TQ, -sl * MOBA_BLOCK,
                                               jnp.where(lane == X_ONE2, sl * (MOBA_BLOCK * tf), ek))))
            qt_ref[LANES * h:LANES * (h + 1), :] = jnp.where(lane < HEAD_DIM, xq, eq).T.astype(BF16)
            ka_ref[0, :, LANES * h:LANES * (h + 1)] = jnp.where(lane < HEAD_DIM, xk, ek).astype(BF16)


def _moba_kernel(qt_ref, ka_ref, vt_ref, kmt_ref, o_ref, qh_scr, s_scr, m_scr, acc_scr):
    i = pl.program_id(1)
    grow = lax.broadcasted_iota(jnp.int32, (N_HEADS * GROUP, TM), 0)
    blk = grow & (GROUP - 1)
    past = blk < i

    gate = _dot(kmt_ref[0], qt_ref[...])
    g = jnp.where(past, gate, -jnp.inf)
    rank = jnp.zeros(g.shape, F32)
    for s in range(1, GROUP):
        up = pltpu.roll(g, N_HEADS * GROUP - s, 0)
        down = pltpu.roll(g, GROUP - s, 0)
        wrapped = blk >= GROUP - s
        partner = jnp.where(wrapped, down, up)
        beats = jnp.where(wrapped, jnp.where(partner >= g, 1.0, 0.0), jnp.where(partner > g, 1.0, 0.0))
        rank = rank + beats
    nsel = jnp.where(past, jnp.where(rank >= MOBA_TOPK, 1.0, 0.0), 0.0).astype(BF16)
    for h in range(N_HEADS):
        r0 = LANES * h
        qh_scr[h, 0:X_SEL, :] = qt_ref[r0:r0 + X_SEL, :]
        qh_scr[h, X_SEL:X_RQ, :] = nsel[GROUP * h:GROUP * (h + 1), :]
        qh_scr[h, X_RQ:LANES, :] = qt_ref[r0 + X_RQ:r0 + LANES, :]

    key = lax.broadcasted_iota(jnp.int32, (TM, TM), 0)
    qry = lax.broadcasted_iota(jnp.int32, (TM, TM), 1)
    causal = key <= qry

    def key_block(j, first):
        tile_max = []
        for h in range(N_HEADS):
            s = _dot(ka_ref[j, :, LANES * h:LANES * (h + 1)], qh_scr[h])
            if first:
                s = jnp.where(causal, s, -jnp.inf)
            s_scr[h] = s
            tile_max.append(jnp.max(s, axis=0, keepdims=True))
        for h in range(N_HEADS):
            vt = vt_ref[j, V_ROWS * h:V_ROWS * (h + 1), :]
            if first:
                m_new = tile_max[h]
                acc_scr[h] = _dot(vt, jnp.exp(s_scr[h] - m_new).astype(BF16))
            else:
                m_old = m_scr[h]
                m_new = jnp.maximum(m_old, tile_max[h])
                alpha = jnp.exp(m_old - m_new)
                acc_scr[h] = alpha * acc_scr[h] + _dot(vt, jnp.exp(s_scr[h] - m_new).astype(BF16))
            m_scr[h] = m_new

    key_block(i, True)

    def body(j, carry):
        key_block(j, False)
        return carry

    lax.fori_loop(0, i, body, 0)
    outs = []
    for h in range(N_HEADS):
        acc = acc_scr[h]
        outs.append(acc[0:HEAD_DIM, :] / acc[HEAD_DIM:HEAD_DIM + 1, :])
    o_ref[...] = jnp.concatenate(outs, axis=0).T.astype(BF16)


def _memkv_kernel(mem_ref, g_ref, w_ref, bd128_ref, mkg_ref, mk_ref, mv_ref, mkb_ref, mvb_ref):
    hn = _rms_rows(mem_ref[...], g_ref[...]).astype(BF16)
    mk = _group_rms(_dot(hn, w_ref[:, 0:MEM_W]), bd128_ref[...], mkg_ref[...])
    mv = _dot(hn, w_ref[:, MEM_W:2 * MEM_W])
    mk_ref[...] = mk
    mv_ref[...] = mv
    mkb_ref[...] = mk.astype(BF16)
    mvb_ref[...] = mv.astype(BF16)


def _mem_attend_rows(mq, mk, mv):
    outs = []
    for hh in range(MEM_HEADS):
        sl = slice(MEM_HEAD_DIM * hh, MEM_HEAD_DIM * (hh + 1))
        s = _dot_nt(mq[:, sl], mk[:, sl]) * (MEM_HEAD_DIM ** -0.5)
        m = jnp.max(s, axis=-1, keepdims=True)
        p = jnp.exp(s - m)
        l = jnp.sum(p, axis=-1, keepdims=True)
        outs.append(_dot(p.astype(BF16), mv[:, sl]) / l)
    return jnp.concatenate(outs, axis=-1)


def _merge_out(x, gl, ya, yb, yc, wout):
    d = x.shape[-1]
    merged = (_sigmoid(gl[:, 0:d]) * ya + _sigmoid(gl[:, d:2 * d]) * yb + _sigmoid(gl[:, 2 * d:3 * d]) * yc)
    return x + _dot(merged.astype(BF16), wout)


def _postmix_kernel(x_ref, cact_ref, ob_ref, mq_ref, mk_ref, mv_ref, g1_ref, wg_ref, bg_ref,
                    wpa_ref, wpb_ref, wpc_ref, wout_ref, x1_ref):
    x = x_ref[...]
    d = x.shape[-1]
    hn = _rms_rows(x, g1_ref[...]).astype(BF16)
    oc = _mem_attend_rows(mq_ref[...], mk_ref[...], mv_ref[...]).astype(BF16)
    branches = (_dot(cact_ref[...], wpa_ref[...]), _dot(ob_ref[...], wpb_ref[...]), _dot(oc, wpc_ref[...]))
    merged = jnp.zeros_like(x)
    for br in range(3):
        gl = _dot(hn, wg_ref[:, d * br:d * (br + 1)]) + bg_ref[:, d * br:d * (br + 1)]
        merged = merged + _sigmoid(gl) * branches[br]
    x1_ref[...] = x + _dot(merged.astype(BF16), wout_ref[...])


def _ffn_kernel(x1_ref, g2_ref, wup_ref, wdw_ref, bdw_ref, wdown_ref, y_ref, tail_ref, upbuf, *, chunk):
    t = pl.program_id(1)
    nt = pl.num_programs(1)
    x1 = x1_ref[...]
    dff = wdown_ref.shape[0]
    hn = _rms_rows(x1, g2_ref[...]).astype(BF16)

    @pl.when(t == 0)
    def _():
        upbuf[0:SUBLANES, :] = jnp.zeros((SUBLANES, 2 * dff), F32)

    for c in range(0, 2 * dff, chunk):
        upbuf[SUBLANES:SUBLANES + TM, c:c + chunk] = _dot(hn, wup_ref[:, c:c + chunk])

    def conv(c):
        out = bdw_ref[:, c:c + chunk]
        for k in range(FFN_CONV_WIDTH):
            r0 = SUBLANES - (FFN_CONV_WIDTH - 1) + k
            out = out + wdw_ref[k:k + 1, c:c + chunk] * upbuf[r0:r0 + TM, c:c + chunk]
        return out

    y = x1
    for c in range(0, dff, chunk):
        a = conv(c)
        b = conv(dff + c)
        act = (a * _sigmoid(a) * b).astype(BF16)
        y = y + _dot(act, wdown_ref[c:c + chunk, :])
    y_ref[...] = y

    @pl.when(t == nt - 1)
    def _():
        tail_ref[0] = upbuf[TM:TM + SUBLANES, :]

    upbuf[0:SUBLANES, :] = upbuf[TM:TM + SUBLANES, :]


def _premix_s_kernel(x_ref, g1_ref, w_ref, bd64_ref, bd128_ref, qg_ref, kg_ref, mqg_ref,
                     st_ref, wdw_ref, bdw_ref, lng_ref, lnb_ref,
                     u_ref, q_ref, k_ref, v_ref, mq_ref, gl_ref, cact_ref):
    c = CONV_CH
    hn = _rms_rows(x_ref[...], g1_ref[...]).astype(BF16)
    a = _dot(hn, w_ref[:, 0:c])
    g = _dot(hn, w_ref[:, c:2 * c])
    u = a * _sigmoid(g)
    u_ref[...] = u
    acc = bdw_ref[...] + wdw_ref[CONV_WIDTH - 1:CONV_WIDTH, :] * u
    for k in range(CONV_WIDTH - 1):
        acc = acc + wdw_ref[k:k + 1, :] * st_ref[k]
    cact_ref[...] = _layernorm_silu(acc, lng_ref[...], lnb_ref[...]).astype(BF16)

    o = 2 * c
    q_ref[...] = _group_rms(_dot(hn, w_ref[:, o:o + ATTN_W]), bd64_ref[...], qg_ref[...])
    o += ATTN_W
    k_ref[...] = _group_rms(_dot(hn, w_ref[:, o:o + ATTN_W]), bd64_ref[...], kg_ref[...])
    o += ATTN_W
    v_ref[...] = _dot(hn, w_ref[:, o:o + ATTN_W])
    o += ATTN_W
    mq_ref[...] = _group_rms(_dot(hn, w_ref[:, o:o + MEM_W]), bd128_ref[...], mqg_ref[...])
    o += MEM_W
    gl_ref[...] = _dot(hn, w_ref[:, o:])


def _block_gate_kernel(pt_ref, q_ref, *refs, pages_per_step, n_blocks):
    page_refs = refs[:pages_per_step]
    sel_ref = refs[pages_per_step]
    gate_scr = refs[pages_per_step + 1]
    s = pl.program_id(1)
    pages_per_block = 2
    blocks_per_step = pages_per_step // pages_per_block
    lane = lax.broadcasted_iota(jnp.int32, (N_HEADS, LANES), 1)
    head = lax.broadcasted_iota(jnp.int32, (N_HEADS, LANES), 0)

    @pl.when(s == 0)
    def _():
        gate_scr[...] = jnp.zeros((N_HEADS, LANES), F32)

    g = gate_scr[...]
    for r in range(0, pages_per_step, pages_per_block):
        bidx = s * blocks_per_step + r // pages_per_block
        for h in range(N_HEADS):
            qc = q_ref[0, h]
            prod = page_refs[r][0, h] * qc + page_refs[r + 1][0, h] * qc
            tot = jnp.sum(jnp.sum(prod, axis=0, keepdims=True), axis=1, keepdims=True) * (1.0 / MOBA_BLOCK)
            g = jnp.where(lane == bidx, jnp.where(head == h, tot, g), g)
    gate_scr[...] = g

    @pl.when(s == pl.num_programs(1) - 1)
    def _():
        g = gate_scr[:, 0:n_blocks]
        bl = lax.broadcasted_iota(jnp.int32, (N_HEADS, n_blocks), 1)
        rank = jnp.zeros((N_HEADS, n_blocks), F32)
        for b in range(n_blocks):
            other = g[:, b:b + 1]
            rank = rank + jnp.where(bl > b, jnp.where(other >= g, 1.0, 0.0), jnp.where(other > g, 1.0, 0.0))
        lane_o = lax.broadcasted_iota(jnp.int32, (N_HEADS, LANES), 1)
        out = jnp.zeros((N_HEADS, LANES), F32)
        blf = bl.astype(F32)
        for r in range(MOBA_TOPK):
            idx = jnp.sum(jnp.where(rank == float(r), blf, 0.0), axis=-1, keepdims=True)
            out = jnp.where(lane_o == r, idx, out)
        sel_ref[0] = out.astype(jnp.int32)


def _decode_attn_kernel(pt_ref, sel_ref, slope_ref, q_ref, kown_ref, vown_ref, *refs, past_len):
    n_tiles = 2 * MOBA_TOPK * 2
    k_refs = refs[:n_tiles]
    v_refs = refs[n_tiles:2 * n_tiles]
    o_ref = refs[2 * n_tiles]
    n = pl.program_id(0)
    hp = pl.program_id(1)
    lane = lax.broadcasted_iota(jnp.int32, (1, LANES), 1)
    lanef = lane.astype(F32)
    qpair = q_ref[0] * (HEAD_DIM ** -0.5)
    kown = kown_ref[0].astype(BF16).astype(F32)
    vown = vown_ref[0].astype(BF16).astype(F32)
    zeros = jnp.zeros((HEAD_DIM, LANES), BF16)
    out = jnp.zeros((1, LANES), F32)
    for sub in range(2):
        h = 2 * hp + sub
        slope = slope_ref[h]
        mine = (lane < HEAD_DIM) if sub == 0 else (lane >= HEAD_DIM)
        qz = jnp.where(mine, qpair, 0.0).astype(BF16)
        q8 = jnp.broadcast_to(qz, (SUBLANES, LANES))
        logits = []
        for r in range(MOBA_TOPK):
            blk = sel_ref[(n * N_HEADS + h) * MOBA_TOPK + r]
            for half in range(2):
                kt = k_refs[(sub * MOBA_TOPK + r) * 2 + half][0, 0].astype(BF16)
                kt = jnp.concatenate([kt, zeros] if sub == 0 else [zeros, kt], axis=0)
                srow = _dot(q8, kt)[0:1, :]
                pos0 = (blk * MOBA_BLOCK + half * LANES).astype(F32)
                dist = float(past_len) - (pos0 + lanef)
                logits.append(srow - slope * dist)
        s_own = jnp.sum(qz.astype(F32) * kown, axis=-1, keepdims=True)
        m = s_own
        for lg in logits:
            m = jnp.maximum(m, jnp.max(lg, axis=-1, keepdims=True))
        p_own = jnp.exp(s_own - m)
        l = p_own
        acc = p_own.astype(BF16).astype(F32) * jnp.where(mine, vown, 0.0)
        for idx, lg in enumerate(logits):
            p = jnp.exp(lg - m)
            l = l + jnp.sum(p, axis=-1, keepdims=True)
            p8 = jnp.broadcast_to(p.astype(BF16), (SUBLANES, LANES))
            vt = v_refs[idx + sub * 2 * MOBA_TOPK][0, 0].astype(BF16)
            vt = jnp.concatenate([vt, zeros] if sub == 0 else [zeros, vt], axis=0)
            acc = acc + _dot_nt(p8, vt)[0:1, :]
        out = out + acc / l
    o_ref[0] = out


def _memattn_s_kernel(mq_ref, mk_ref, mv_ref, o_ref):
    mq = jnp.broadcast_to(mq_ref[0], (SUBLANES, MEM_W)).astype(BF16)
    o = _mem_attend_rows(mq, mk_ref[0].astype(BF16), mv_ref[0].astype(BF16))
    o_ref[0] = o[0:1, :]


def _post_s_kernel(x_ref, cact_ref, ob_ref, oc_ref, gl_ref, bg_ref, wpa_ref, wpb_ref, wpc_ref, wout_ref,
                   g2_ref, wup_ref, st_ref, wdw_ref, bdw_ref, wdown_ref, y_ref, up_ref):
    x = x_ref[...]
    dff = wdown_ref.shape[0]
    ya = _dot(cact_ref[...], wpa_ref[...])
    yb = _dot(ob_ref[...].astype(BF16), wpb_ref[...])
    yc = _dot(oc_ref[...].astype(BF16), wpc_ref[...])
    x1 = _merge_out(x, gl_ref[...] + bg_ref[...], ya, yb, yc, wout_ref[...])
    up = _dot(_rms_rows(x1, g2_ref[...]).astype(BF16), wup_ref[...])
    up_ref[...] = up
    cv = bdw_ref[...] + wdw_ref[FFN_CONV_WIDTH - 1:FFN_CONV_WIDTH, :] * up
    for k in range(FFN_CONV_WIDTH - 1):
        cv = cv + wdw_ref[k:k + 1, :] * st_ref[k]
    a = cv[:, 0:dff]
    b = cv[:, dff:]
    act = (a * _sigmoid(a) * b).astype(BF16)
    y_ref[...] = x1 + _dot(act, wdown_ref[...])


def _resident(shape):
    nd = len(shape)
    return pl.BlockSpec(shape, lambda *_: (0,) * nd, pipeline_mode=pl.Buffered(1))


def _params(n_axes):
    return pltpu.CompilerParams(dimension_semantics=("arbitrary",) * n_axes, vmem_limit_bytes=VMEM_LIMIT)


def kernel(x_prompt, x_sample, mem_prompt, cache_k, cache_v, page_table, state_conv, state_ffn_conv, cache_mem_k, cache_mem_v, norm1_g, w_in, b_gate, w_dw_a, b_dw_a, ln_a_g, ln_a_b, w_proj_a, q_norm_g, k_norm_g, w_proj_b, mem_norm_g, w_mem_kv, mq_norm_g, mk_norm_g, w_proj_c, w_out, norm2_g, w_up, w_dw_f, b_dw_f, w_down):
    n, t, d = x_prompt.shape
    nd, td, _ = x_sample.shape
    n_mem = mem_prompt.shape[1]
    n_pool, page_size = cache_k.shape[:2]
    n_pages = page_table.shape[1]
    past_len = n_pages * page_size
    dff = w_down.shape[0]
    nt = t // TM
    n_mix = 2 * CONV_CH + 3 * ATTN_W + MEM_W
    assert t % TM == 0 and nt <= GROUP and td == 1
    assert past_len % MOBA_BLOCK == 0 and MOBA_BLOCK == 2 * page_size and page_size == LANES
    n_blocks = past_len // MOBA_BLOCK

    row = lambda v: v.reshape(1, -1).astype(F32)
    w_in_b = w_in.astype(BF16)
    w_mix, w_gate = w_in_b[:, :n_mix], w_in_b[:, n_mix:]
    wpa, wpb, wpc = w_proj_a.astype(BF16), w_proj_b.astype(BF16), w_proj_c.astype(BF16)
    wout, wup, wdown = w_out.astype(BF16), w_up.astype(BF16), w_down.astype(BF16)
    g1, g2 = row(norm1_g), row(norm2_g)
    qg = row(jnp.tile(q_norm_g, N_HEADS))
    kg = row(jnp.tile(k_norm_g, N_HEADS))
    mqg = row(jnp.tile(mq_norm_g, MEM_HEADS))
    mkg = row(jnp.tile(mk_norm_g, MEM_HEADS))
    bdw_a, lng, lnb, bg, bdw_f = row(b_dw_a), row(ln_a_g), row(ln_a_b), row(b_gate), row(b_dw_f)
    grp = np.arange(ATTN_W)
    bd64 = jnp.asarray((grp[:, None] // HEAD_DIM == grp[None, :] // HEAD_DIM) / HEAD_DIM, BF16)
    bd128 = jnp.asarray((grp[:, None] // MEM_HEAD_DIM == grp[None, :] // MEM_HEAD_DIM) / MEM_HEAD_DIM, BF16)
    slopes = 2.0 ** (-8.0 * np.arange(1, N_HEADS + 1) / N_HEADS)
    slope_lanes = jnp.asarray(np.repeat(slopes, LANES)[None, :], F32)

    xp = x_prompt.reshape(n * t, d)
    tile = lambda w: pl.BlockSpec((TM, w), lambda b, i: (b * nt + i, 0))
    aw = N_HEADS * LANES
    vw = N_HEADS * V_ROWS
    blk3 = lambda r, c: pl.BlockSpec((1, r, c), lambda b, i: (b * nt + i, 0, 0))
    k_p, v_p, km, qt, ka, vt, mq, cact, utail = pl.pallas_call(
        _premix_kernel,
        grid=(n, nt),
        in_specs=[tile(d), _resident((1, d)), _resident((d, n_mix)), _resident((ATTN_W, ATTN_W)),
                  _resident((MEM_W, MEM_W)), _resident((1, ATTN_W)), _resident((1, ATTN_W)),
                  _resident((1, MEM_W)), _resident((CONV_WIDTH, CONV_CH)), _resident((1, CONV_CH)),
                  _resident((1, CONV_CH)), _resident((1, CONV_CH)), _resident((1, aw))],
        out_specs=[pl.BlockSpec((ATTN_W, TM), lambda b, i: (b, i)), pl.BlockSpec((ATTN_W, TM), lambda b, i: (b, i)),
                   pl.BlockSpec((1, 1, ATTN_W), lambda b, i: (b * nt + i, 0, 0)),
                   pl.BlockSpec((aw, TM), lambda b, i: (b, i)), blk3(TM, aw), blk3(vw, TM),
                   tile(MEM_W), tile(CONV_CH),
                   pl.BlockSpec((1, CONV_HALO, CONV_CH), lambda b, i: (b, 0, 0))],
        out_shape=[jax.ShapeDtypeStruct((n * ATTN_W, t), F32), jax.ShapeDtypeStruct((n * ATTN_W, t), F32),
                   jax.ShapeDtypeStruct((n * nt, 1, ATTN_W), F32),
                   jax.ShapeDtypeStruct((n * aw, t), BF16), jax.ShapeDtypeStruct((n * nt, TM, aw), BF16),
                   jax.ShapeDtypeStruct((n * nt, vw, TM), BF16), jax.ShapeDtypeStruct((n * t, MEM_W), BF16),
                   jax.ShapeDtypeStruct((n * t, CONV_CH), BF16),
                   jax.ShapeDtypeStruct((n, CONV_HALO, CONV_CH), F32)],
        scratch_shapes=[pltpu.VMEM((CONV_HALO + TM, CONV_CH), F32)],
        compiler_params=_params(2),
        name="premix",
    )(xp, g1, w_mix, bd64, bd128, qg, kg, mqg, w_dw_a, bdw_a, lng, lnb, slope_lanes)

    km4 = km.reshape(n, nt, N_HEADS, HEAD_DIM).transpose(0, 2, 1, 3)
    km4 = jnp.pad(km4, ((0, 0), (0, 0), (0, GROUP - nt), (0, LANES - HEAD_DIM)))
    kmt = (km4[:, :, :, None, :] * jnp.eye(N_HEADS, dtype=F32)[None, :, None, :, None])
    kmt = kmt.reshape(n, N_HEADS * GROUP, aw).astype(BF16)

    seq3 = lambda r, c: pl.BlockSpec((nt, r, c), lambda b, i: (b, 0, 0))
    ob = pl.pallas_call(
        _moba_kernel,
        grid=(n, nt),
        in_specs=[pl.BlockSpec((aw, TM), lambda b, i: (b, i)), seq3(TM, aw), seq3(vw, TM),
                  pl.BlockSpec((1, N_HEADS * GROUP, aw), lambda b, i: (b, 0, 0))],
        out_specs=tile(ATTN_W),
        out_shape=jax.ShapeDtypeStruct((n * t, ATTN_W), BF16),
        scratch_shapes=[pltpu.VMEM((N_HEADS, LANES, TM), BF16), pltpu.VMEM((N_HEADS, TM, TM), F32),
                        pltpu.VMEM((N_HEADS, 1, TM), F32), pltpu.VMEM((N_HEADS, V_ROWS, TM), F32)],
        compiler_params=_params(2),
        name="moba",
    )(qt, ka, vt, kmt)

    mem = mem_prompt.reshape(n * n_mem, d)
    mtile = lambda w: pl.BlockSpec((n_mem, w), lambda b: (b, 0))
    mk_p, mv_p, mkb, mvb = pl.pallas_call(
        _memkv_kernel,
        grid=(n,),
        in_specs=[mtile(d), _resident((1, d)), _resident((d, 2 * MEM_W)), _resident((MEM_W, MEM_W)),
                  _resident((1, MEM_W))],
        out_specs=[mtile(MEM_W)] * 4,
        out_shape=[jax.ShapeDtypeStruct((n * n_mem, MEM_W), F32)] * 2
        + [jax.ShapeDtypeStruct((n * n_mem, MEM_W), BF16)] * 2,
        compiler_params=_params(1),
        name="memkv",
    )(mem, row(mem_norm_g), w_mem_kv.astype(BF16), bd128, mkg)

    memb = pl.BlockSpec((n_mem, MEM_W), lambda b, i: (b, 0))
    x1 = pl.pallas_call(
        _postmix_kernel,
        grid=(n, nt),
        in_specs=[tile(d), tile(CONV_CH), tile(ATTN_W), tile(MEM_W), memb, memb, _resident((1, d)),
                  _resident((d, 3 * d)), _resident((1, 3 * d)), _resident((CONV_CH, d)), _resident((ATTN_W, d)),
                  _resident((MEM_W, d)), _resident((d, d))],
        out_specs=tile(d),
        out_shape=jax.ShapeDtypeStruct((n * t, d), F32),
        compiler_params=_params(2),
        name="postmix",
    )(xp, cact, ob, mq, mkb, mvb, g1, w_gate, bg, wpa, wpb, wpc, wout)

    chunk = dff // 2
    assert chunk % LANES == 0
    y_p, ftail = pl.pallas_call(
        functools.partial(_ffn_kernel, chunk=chunk),
        grid=(n, nt),
        in_specs=[tile(d), _resident((1, d)), _resident((d, 2 * dff)), _resident((FFN_CONV_WIDTH, 2 * dff)),
                  _resident((1, 2 * dff)), _resident((dff, d))],
        out_specs=[tile(d), pl.BlockSpec((1, SUBLANES, 2 * dff), lambda b, i: (b, 0, 0))],
        out_shape=[jax.ShapeDtypeStruct((n * t, d), F32), jax.ShapeDtypeStruct((n, SUBLANES, 2 * dff), F32)],
        scratch_shapes=[pltpu.VMEM((SUBLANES + TM, 2 * dff), F32)],
        compiler_params=_params(2),
        name="ffn",
    )(x1, g2, wup, w_dw_f, bdw_f, wdown)

    xs = x_sample.reshape(nd, d)
    st_conv = state_conv.transpose(1, 0, 2)
    st_ffn = state_ffn_conv.transpose(1, 0, 2)
    vm = pltpu.CompilerParams(vmem_limit_bytes=VMEM_LIMIT)
    u_s, q_s, k_s, v_s, mq_s, gl_s, cact_s = pl.pallas_call(
        _premix_s_kernel,
        out_shape=[jax.ShapeDtypeStruct((nd, CONV_CH), F32), jax.ShapeDtypeStruct((nd, ATTN_W), F32),
                   jax.ShapeDtypeStruct((nd, ATTN_W), F32), jax.ShapeDtypeStruct((nd, ATTN_W), F32),
                   jax.ShapeDtypeStruct((nd, MEM_W), F32), jax.ShapeDtypeStruct((nd, 3 * d), F32),
                   jax.ShapeDtypeStruct((nd, CONV_CH), BF16)],
        compiler_params=vm,
        name="premix_s",
    )(xs, g1, w_in_b, bd64, bd128, qg, kg, mqg, st_conv, w_dw_a, bdw_a, lng, lnb)

    ck = cache_k.transpose(0, 2, 3, 1)
    cv = cache_v.transpose(0, 2, 3, 1)
    pt_flat = page_table.reshape(-1).astype(jnp.int32)
    pages_per_step = 16
    assert n_pages % pages_per_step == 0 and n_blocks <= LANES
    page_spec = lambda r: pl.BlockSpec(
        (1, N_HEADS, HEAD_DIM, page_size),
        lambda b, s, pt: (pt[b * n_pages + s * pages_per_step + r], 0, 0, 0))
    q_cols = jnp.broadcast_to(q_s.reshape(nd, N_HEADS, HEAD_DIM, 1), (nd, N_HEADS, HEAD_DIM, page_size))
    sel = pl.pallas_call(
        functools.partial(_block_gate_kernel, pages_per_step=pages_per_step, n_blocks=n_blocks),
        grid_spec=pltpu.PrefetchScalarGridSpec(
            num_scalar_prefetch=1,
            grid=(nd, n_pages // pages_per_step),
            in_specs=[pl.BlockSpec((1, N_HEADS, HEAD_DIM, page_size), lambda b, s, pt: (b, 0, 0, 0))]
            + [page_spec(r) for r in range(pages_per_step)],
            out_specs=pl.BlockSpec((1, N_HEADS, LANES), lambda b, s, pt: (b, 0, 0)),
            scratch_shapes=[pltpu.VMEM((N_HEADS, LANES), F32)]),
        out_shape=jax.ShapeDtypeStruct((nd, N_HEADS, LANES), jnp.int32),
        compiler_params=_params(2),
        name="block_gate",
    )(pt_flat, q_cols, *([ck] * pages_per_step))
    sel_flat = sel[:, :, :MOBA_TOPK].reshape(-1)

    n_pairs = N_HEADS // 2
    pair = lambda a: a.reshape(nd * n_pairs, 1, LANES)
    pair_spec = pl.BlockSpec((1, 1, LANES), lambda b, hp, pt, sl: (b * n_pairs + hp, 0, 0))

    def kv_spec(sub, r, half):
        def index(b, hp, pt, sl):
            blk = sl[(b * N_HEADS + 2 * hp + sub) * MOBA_TOPK + r]
            return (pt[b * n_pages + 2 * blk + half], 2 * hp + sub, 0, 0)
        return pl.BlockSpec((1, 1, HEAD_DIM, page_size), index)

    kv_specs = [kv_spec(sub, r, half) for sub in range(2) for r in range(MOBA_TOPK) for half in range(2)]
    ob_s = pl.pallas_call(
        functools.partial(_decode_attn_kernel, past_len=past_len),
        grid_spec=pltpu.PrefetchScalarGridSpec(
            num_scalar_prefetch=2,
            grid=(nd, n_pairs),
            in_specs=[pl.BlockSpec(memory_space=pltpu.SMEM), pair_spec, pair_spec, pair_spec]
            + kv_specs + kv_specs,
            out_specs=pair_spec),
        out_shape=jax.ShapeDtypeStruct((nd * n_pairs, 1, LANES), F32),
        compiler_params=_params(2),
        name="decode_attn",
    )(pt_flat, sel_flat, jnp.asarray(slopes, F32), pair(q_s), pair(k_s), pair(v_s),
      *([ck] * len(kv_specs)), *([cv] * len(kv_specs)))
    ob_s = ob_s.reshape(nd, ATTN_W)

    cm_spec = pl.BlockSpec((1, n_mem, MEM_W), lambda b: (b, 0, 0))
    one_spec = pl.BlockSpec((1, 1, MEM_W), lambda b: (b, 0, 0))
    oc_s = pl.pallas_call(
        _memattn_s_kernel,
        grid=(nd,),
        in_specs=[one_spec, cm_spec, cm_spec],
        out_specs=one_spec,
        out_shape=jax.ShapeDtypeStruct((nd, 1, MEM_W), F32),
        compiler_params=_params(1),
        name="memattn_s",
    )(mq_s.reshape(nd, 1, MEM_W), cache_mem_k.reshape(nd, n_mem, MEM_W), cache_mem_v.reshape(nd, n_mem, MEM_W))
    oc_s = oc_s.reshape(nd, MEM_W)

    y_s, up_s = pl.pallas_call(
        _post_s_kernel,
        out_shape=[jax.ShapeDtypeStruct((nd, d), F32), jax.ShapeDtypeStruct((nd, 2 * dff), F32)],
        compiler_params=vm,
        name="post_s",
    )(xs, cact_s, ob_s, oc_s, gl_s, bg, wpa, wpb, wpc, wout, g2, wup, st_ffn, w_dw_f, bdw_f, wdown)

    heads = lambda a, b, s: a.reshape(b, s, N_HEADS, HEAD_DIM)
    from_t = lambda a: a.reshape(n, N_HEADS, HEAD_DIM, t).transpose(0, 3, 1, 2)
    conv_p = utail[:, CONV_HALO - (CONV_WIDTH - 1):, :]
    conv_s = jnp.concatenate([state_conv[:, 1:, :], u_s[:, None, :]], axis=1)
    ffn_p = ftail[:, SUBLANES - (FFN_CONV_WIDTH - 1):, :]
    ffn_s = jnp.concatenate([state_ffn_conv[:, 1:, :], up_s[:, None, :]], axis=1)
    return (y_p.reshape(n, t, d), y_s.reshape(nd, td, d),
            from_t(k_p), from_t(v_p), heads(k_s, nd, td), heads(v_s, nd, td),
            conv_p, conv_s, ffn_p, ffn_s,
            mk_p.reshape(n, n_mem, MEM_HEADS, MEM_HEAD_DIM), mv_p.reshape(n, n_mem, MEM_HEADS, MEM_HEAD_DIM))
```

```python
import functools

import numpy as np
import jax
import jax.numpy as jnp
from jax import lax
from jax.experimental import pallas as pl
from jax.experimental.pallas import tpu as pltpu

F32 = jnp.float32
BF16 = jnp.bfloat16

EPS = 1e-6
CONV_CH = 512
CONV_WIDTH = 31
N_HEADS = 8
HEAD_DIM = 64
ATTN_W = N_HEADS * HEAD_DIM
MOBA_BLOCK = 256
MOBA_TOPK = 3
MEM_HEADS = 4
MEM_HEAD_DIM = 128
MEM_W = MEM_HEADS * MEM_HEAD_DIM
FFN_CONV_WIDTH = 3
LANES = 128
SUBLANES = 8
TM = MOBA_BLOCK
CONV_HALO = 32
MASK_NEG = -float(2 ** 30)
GROUP = 16
V_ROWS = HEAD_DIM + 16
VMEM_LIMIT = 56 * 1024 * 1024

X_SEL = HEAD_DIM
X_RQ = HEAD_DIM + GROUP
X_ONE = X_RQ + 1
X_TQ = X_RQ + 2
X_ONE2 = X_RQ + 3


def _dot(a, b):
    return jnp.dot(a, b, preferred_element_type=F32)


def _dot_nt(a, b):
    return lax.dot_general(a, b, (((1,), (1,)), ((), ())), preferred_element_type=F32)


def _rms_rows(x, g):
    return x * lax.rsqrt(jnp.mean(x * x, axis=-1, keepdims=True) + EPS) * g


def _group_rms(z, bd, g):
    sq = z * z
    hi = sq.astype(BF16)
    lo = (sq - hi.astype(F32)).astype(BF16)
    ms = _dot(hi, bd) + _dot(lo, bd)
    return z * lax.rsqrt(ms + EPS) * g


def _sigmoid(x):
    return 1.0 / (1.0 + jnp.exp(-x))


def _layernorm_silu(c, g, b):
    mu = jnp.mean(c, axis=-1, keepdims=True)
    xc = c - mu
    var = jnp.mean(xc * xc, axis=-1, keepdims=True)
    y = xc * lax.rsqrt(var + EPS) * g + b
    return y * _sigmoid(y)


def _premix_kernel(x_ref, g1_ref, w_ref, bd64_ref, bd128_ref, qg_ref, kg_ref, mqg_ref,
                   wdw_ref, bdw_ref, lng_ref, lnb_ref, slope_ref,
                   k_ref, v_ref, km_ref, qt_ref, ka_ref, vt_ref, mq_ref, cact_ref, utail_ref,
                   ubuf):
    t = pl.program_id(1)
    nt = pl.num_programs(1)
    c = CONV_CH
    hn = _rms_rows(x_ref[...], g1_ref[...]).astype(BF16)

    a = _dot(hn, w_ref[:, 0:c])
    g = _dot(hn, w_ref[:, c:2 * c])
    u = a * _sigmoid(g)

    @pl.when(t == 0)
    def _():
        ubuf[0:CONV_HALO, :] = jnp.zeros((CONV_HALO, c), F32)
        ubuf[CONV_HALO + TM:, :] = jnp.zeros((SUBLANES, c), F32)

    ubuf[CONV_HALO:CONV_HALO + TM, :] = u
    acc = jnp.broadcast_to(bdw_ref[...], (TM, c))
    base = CONV_HALO - (CONV_WIDTH - 1)
    span = TM + 2 * SUBLANES
    for b in range(SUBLANES):
        part = None
        for k in range(b, CONV_WIDTH, SUBLANES):
            term = wdw_ref[k:k + 1, :] * ubuf[k - b:k - b + span, :]
            part = term if part is None else part + term
        acc = acc + part[base + b:base + b + TM, :]
    ubuf[0:CONV_HALO, :] = ubuf[TM:TM + CONV_HALO, :]
    cact_ref[...] = _layernorm_silu(acc, lng_ref[...], lnb_ref[...]).astype(BF16)

    @pl.when(t == nt - 1)
    def _():
        utail_ref[0] = u[TM - CONV_HALO:, :]

    o = 2 * c
    qn = _group_rms(_dot(hn, w_ref[:, o:o + ATTN_W]), bd64_ref[...], qg_ref[...]) * (HEAD_DIM ** -0.5)
    o += ATTN_W
    kn = _group_rms(_dot(hn, w_ref[:, o:o + ATTN_W]), bd64_ref[...], kg_ref[...])
    o += ATTN_W
    zv = _dot(hn, w_ref[:, o:o + ATTN_W])
    o += ATTN_W
    mq = _group_rms(_dot(hn, w_ref[:, o:o + MEM_W]), bd128_ref[...], mqg_ref[...])

    ones_rows = jnp.where(lax.broadcasted_iota(jnp.int32, (V_ROWS - HEAD_DIM, TM), 0) == 0, 1.0, 0.0).astype(BF16)
    for cb in range(ATTN_W // LANES):
        k_ref[LANES * cb:LANES * (cb + 1), :] = kn[:, LANES * cb:LANES * (cb + 1)].T
        vt = zv[:, LANES * cb:LANES * (cb + 1)].T
        v_ref[LANES * cb:LANES * (cb + 1), :] = vt
        for sub in range(2):
            r0 = V_ROWS * (2 * cb + sub)
            vt_ref[0, r0:r0 + HEAD_DIM, :] = vt[HEAD_DIM * sub:HEAD_DIM * (sub + 1), :].astype(BF16)
            vt_ref[0, r0 + HEAD_DIM:r0 + V_ROWS, :] = ones_rows
    mq_ref[...] = mq.astype(BF16)
    km_ref[0] = jnp.mean(kn, axis=0, keepdims=True)

    lane = lax.broadcasted_iota(jnp.int32, (TM, LANES), 1)
    rowf = lax.broadcasted_iota(jnp.int32, (TM, LANES), 0).astype(F32)
    tf = t.astype(F32)
    zero = jnp.zeros((TM, LANES), F32)
    eq = jnp.where(lane == X_RQ, rowf,
                   jnp.where(lane == X_ONE, 1.0,
                             jnp.where(lane == X_TQ, tf,
                                       jnp.where(lane == X_ONE2, 1.0, zero))))
    for hp in range(N_HEADS // 2):
        xq = qn[:, LANES * hp:LANES * (hp + 1)]
        xk = kn[:, LANES * hp:LANES * (hp + 1)]
        for sub in range(2):
            h = 2 * hp + sub
            if sub == 1:
                xq = pltpu.roll(xq, HEAD_DIM, 1)
                xk = pltpu.roll(xk, HEAD_DIM, 1)
            sl = slope_ref[:, LANES * h:LANES * (h + 1)]
            ek = jnp.where(lane == X_SEL + t, MASK_NEG, zero)
            ek = jnp.where(lane == X_RQ, -sl,
                           jnp.where(lane == X_ONE, sl * rowf,
                                     jnp.where(lane == X_TQ, -sl * MOBA_BLOCK,
                                               jnp.where(lane == X_ONE2, sl * (MOBA_BLOCK * tf), ek))))
            qt_ref[LANES * h:LANES * (h + 1), :] = jnp.where(lane < HEAD_DIM, xq, eq).T.astype(BF16)
            ka_ref[0, :, LANES * h:LANES * (h + 1)] = jnp.where(lane < HEAD_DIM, xk, ek).astype(BF16)


def _moba_kernel(qt_ref, ka_ref, vt_ref, kmt_ref, o_ref, qh_scr, m_scr, acc_scr, *stage):
    i = pl.program_id(1)
    s_scr, mx_scr = stage[:N_HEADS], stage[N_HEADS:]
    key = lax.broadcasted_iota(jnp.int32, (TM, TM), 0)
    qry = lax.broadcasted_iota(jnp.int32, (TM, TM), 1)
    causal = key <= qry

    def scores(j, h, diagonal):
        q = qt_ref[LANES * h:LANES * (h + 1), :] if diagonal else qh_scr[h]
        s = _dot(ka_ref[j, :, LANES * h:LANES * (h + 1)], q)
        if diagonal:
            s = jnp.where(causal, s, -jnp.inf)
        s_scr[h][...] = s
        mx_scr[h][...] = jnp.max(s, axis=0, keepdims=True)

    for h in range(N_HEADS):
        scores(i, h, True)

    grow = lax.broadcasted_iota(jnp.int32, (N_HEADS * GROUP, TM), 0)
    blk = grow & (GROUP - 1)
    past = blk < i

    gate = _dot(kmt_ref[0], qt_ref[...])
    g = jnp.where(past, gate, -jnp.inf)
    rank = jnp.zeros(g.shape, F32)
    for s in range(1, GROUP):
        up = pltpu.roll(g, N_HEADS * GROUP - s, 0)
        down = pltpu.roll(g, GROUP - s, 0)
        wrapped = blk >= GROUP - s
        partner = jnp.where(wrapped, down, up)
        beats = jnp.where(wrapped, jnp.where(partner >= g, 1.0, 0.0), jnp.where(partner > g, 1.0, 0.0))
        rank = rank + beats
    nsel = jnp.where(past, jnp.where(rank >= MOBA_TOPK, 1.0, 0.0), 0.0).astype(BF16)
    for h in range(N_HEADS):
        r0 = LANES * h
        qh_scr[h, 0:X_SEL, :] = qt_ref[r0:r0 + X_SEL, :]
        qh_scr[h, X_SEL:X_RQ, :] = nsel[GROUP * h:GROUP * (h + 1), :]
        qh_scr[h, X_RQ:LANES, :] = qt_ref[r0 + X_RQ:r0 + LANES, :]

    def accumulate(j, h):
        m_old = m_scr[h]
        m_new = jnp.maximum(m_old, mx_scr[h][...])
        alpha = jnp.exp(m_old - m_new)
        p = jnp.exp(s_scr[h][...] - m_new).astype(BF16)
        acc_scr[h] = alpha * acc_scr[h] + _dot(vt_ref[j, V_ROWS * h:V_ROWS * (h + 1), :], p)
        m_scr[h] = m_new

    m_scr[...] = jnp.full(m_scr.shape, -jnp.inf, F32)
    acc_scr[...] = jnp.zeros(acc_scr.shape, F32)

    def body(j, carry):
        prev = jnp.where(j == 0, i, j - 1)
        for h in range(N_HEADS):
            accumulate(prev, h)
            scores(j, h, False)
        return carry

    lax.fori_loop(0, i, body, 0)
    last = jnp.where(i == 0, i, i - 1)
    for h in range(N_HEADS):
        accumulate(last, h)
    outs = []
    for h in range(N_HEADS):
        acc = acc_scr[h]
        outs.append(acc[0:HEAD_DIM, :] / acc[HEAD_DIM:HEAD_DIM + 1, :])
    o_ref[...] = jnp.concatenate(outs, axis=0).T.astype(BF16)


def _memkv_kernel(mem_ref, g_ref, w_ref, bd128_ref, mkg_ref, mk_ref, mv_ref, mkb_ref, mvb_ref):
    hn = _rms_rows(mem_ref[...], g_ref[...]).astype(BF16)
    mk = _group_rms(_dot(hn, w_ref[:, 0:MEM_W]), bd128_ref[...], mkg_ref[...])
    mv = _dot(hn, w_ref[:, MEM_W:2 * MEM_W])
    mk_ref[...] = mk
    mv_ref[...] = mv
    mkb_ref[...] = mk.astype(BF16)
    mvb_ref[...] = mv.astype(BF16)


def _mem_attend_rows(mq, mk, mv):
    outs = []
    for hh in range(MEM_HEADS):
        sl = slice(MEM_HEAD_DIM * hh, MEM_HEAD_DIM * (hh + 1))
        s = _dot_nt(mq[:, sl], mk[:, sl]) * (MEM_HEAD_DIM ** -0.5)
        m = jnp.max(s, axis=-1, keepdims=True)
        p = jnp.exp(s - m)
        l = jnp.sum(p, axis=-1, keepdims=True)
        outs.append(_dot(p.astype(BF16), mv[:, sl]) / l)
    return jnp.concatenate(outs, axis=-1)


def _merge_out(x, gl, ya, yb, yc, wout):
    d = x.shape[-1]
    merged = (_sigmoid(gl[:, 0:d]) * ya + _sigmoid(gl[:, d:2 * d]) * yb + _sigmoid(gl[:, 2 * d:3 * d]) * yc)
    return x + _dot(merged.astype(BF16), wout)


def _postmix_kernel(x_ref, cact_ref, ob_ref, mq_ref, mk_ref, mv_ref, g1_ref, wg_ref, bg_ref,
                    wpa_ref, wpb_ref, wpc_ref, wout_ref, x1_ref):
    x = x_ref[...]
    d = x.shape[-1]
    hn = _rms_rows(x, g1_ref[...]).astype(BF16)
    oc = _mem_attend_rows(mq_ref[...], mk_ref[...], mv_ref[...]).astype(BF16)
    branches = (_dot(cact_ref[...], wpa_ref[...]), _dot(ob_ref[...], wpb_ref[...]), _dot(oc, wpc_ref[...]))
    merged = jnp.zeros_like(x)
    for br in range(3):
        gl = _dot(hn, wg_ref[:, d * br:d * (br + 1)]) + bg_ref[:, d * br:d * (br + 1)]
        merged = merged + _sigmoid(gl) * branches[br]
    x1_ref[...] = x + _dot(merged.astype(BF16), wout_ref[...])


def _ffn_kernel(x1_ref, g2_ref, wup_ref, wdw_ref, bdw_ref, wdown_ref, y_ref, tail_ref, upbuf, *, chunk):
    t = pl.program_id(1)
    nt = pl.num_programs(1)
    x1 = x1_ref[...]
    dff = wdown_ref.shape[0]
    hn = _rms_rows(x1, g2_ref[...]).astype(BF16)

    @pl.when(t == 0)
    def _():
        upbuf[0:SUBLANES, :] = jnp.zeros((SUBLANES, 2 * dff), F32)

    for c in range(0, 2 * dff, chunk):
        upbuf[SUBLANES:SUBLANES + TM, c:c + chunk] = _dot(hn, wup_ref[:, c:c + chunk])

    def conv(c):
        out = bdw_ref[:, c:c + chunk]
        for k in range(FFN_CONV_WIDTH):
            r0 = SUBLANES - (FFN_CONV_WIDTH - 1) + k
            out = out + wdw_ref[k:k + 1, c:c + chunk] * upbuf[r0:r0 + TM, c:c + chunk]
        return out

    y = x1
    for c in range(0, dff, chunk):
        a = conv(c)
        b = conv(dff + c)
        act = (a * _sigmoid(a) * b).astype(BF16)
        y = y + _dot(act, wdown_ref[c:c + chunk, :])
    y_ref[...] = y

    @pl.when(t == nt - 1)
    def _():
        tail_ref[0] = upbuf[TM:TM + SUBLANES, :]

    upbuf[0:SUBLANES, :] = upbuf[TM:TM + SUBLANES, :]


def _premix_s_kernel(x_ref, g1_ref, w_ref, bd64_ref, bd128_ref, qg_ref, kg_ref, mqg_ref,
                     st_ref, wdw_ref, bdw_ref, lng_ref, lnb_ref,
                     u_ref, q_ref, k_ref, v_ref, mq_ref, gl_ref, cact_ref):
    c = CONV_CH
    hn = _rms_rows(x_ref[...], g1_ref[...]).astype(BF16)
    a = _dot(hn, w_ref[:, 0:c])
    g = _dot(hn, w_ref[:, c:2 * c])
    u = a * _sigmoid(g)
    u_ref[...] = u
    acc = bdw_ref[...] + wdw_ref[CONV_WIDTH - 1:CONV_WIDTH, :] * u
    for k in range(CONV_WIDTH - 1):
        acc = acc + wdw_ref[k:k + 1, :] * st_ref[k]
    cact_ref[...] = _layernorm_silu(acc, lng_ref[...], lnb_ref[...]).astype(BF16)

    o = 2 * c
    q_ref[...] = _group_rms(_dot(hn, w_ref[:, o:o + ATTN_W]), bd64_ref[...], qg_ref[...])
    o += ATTN_W
    k_ref[...] = _group_rms(_dot(hn, w_ref[:, o:o + ATTN_W]), bd64_ref[...], kg_ref[...])
    o += ATTN_W
    v_ref[...] = _dot(hn, w_ref[:, o:o + ATTN_W])
    o += ATTN_W
    mq_ref[...] = _group_rms(_dot(hn, w_ref[:, o:o + MEM_W]), bd128_ref[...], mqg_ref[...])
    o += MEM_W
    gl_ref[...] = _dot(hn, w_ref[:, o:])


def _block_gate_kernel(pt_ref, q_ref, *refs, pages_per_step, n_blocks):
    page_refs = refs[:pages_per_step]
    sel_ref = refs[pages_per_step]
    gate_scr = refs[pages_per_step + 1]
    s = pl.program_id(1)
    pages_per_block = 2
    blocks_per_step = pages_per_step // pages_per_block
    lane = lax.broadcasted_iota(jnp.int32, (N_HEADS, LANES), 1)
    head = lax.broadcasted_iota(jnp.int32, (N_HEADS, LANES), 0)

    @pl.when(s == 0)
    def _():
        gate_scr[...] = jnp.zeros((N_HEADS, LANES), F32)

    g = gate_scr[...]
    for r in range(0, pages_per_step, pages_per_block):
        bidx = s * blocks_per_step + r // pages_per_block
        for h in range(N_HEADS):
            qc = q_ref[0, h]
            prod = page_refs[r][0, h] * qc + page_refs[r + 1][0, h] * qc
            tot = jnp.sum(jnp.sum(prod, axis=0, keepdims=True), axis=1, keepdims=True) * (1.0 / MOBA_BLOCK)
            g = jnp.where(lane == bidx, jnp.where(head == h, tot, g), g)
    gate_scr[...] = g

    @pl.when(s == pl.num_programs(1) - 1)
    def _():
        g = gate_scr[:, 0:n_blocks]
        bl = lax.broadcasted_iota(jnp.int32, (N_HEADS, n_blocks), 1)
        rank = jnp.zeros((N_HEADS, n_blocks), F32)
        for b in range(n_blocks):
            other = g[:, b:b + 1]
            rank = rank + jnp.where(bl > b, jnp.where(other >= g, 1.0, 0.0), jnp.where(other > g, 1.0, 0.0))
        lane_o = lax.broadcasted_iota(jnp.int32, (N_HEADS, LANES), 1)
        out = jnp.zeros((N_HEADS, LANES), F32)
        blf = bl.astype(F32)
        for r in range(MOBA_TOPK):
            idx = jnp.sum(jnp.where(rank == float(r), blf, 0.0), axis=-1, keepdims=True)
            out = jnp.where(lane_o == r, idx, out)
        sel_ref[0] = out.astype(jnp.int32)


def _decode_attn_kernel(pt_ref, sel_ref, slope_ref, q_ref, kown_ref, vown_ref, *refs, past_len):
    n_tiles = 2 * MOBA_TOPK * 2
    k_refs = refs[:n_tiles]
    v_refs = refs[n_tiles:2 * n_tiles]
    o_ref = refs[2 * n_tiles]
    n = pl.program_id(0)
    hp = pl.program_id(1)
    lane = lax.broadcasted_iota(jnp.int32, (1, LANES), 1)
    lanef = lane.astype(F32)
    qpair = q_ref[0] * (HEAD_DIM ** -0.5)
    kown = kown_ref[0].astype(BF16).astype(F32)
    vown = vown_ref[0].astype(BF16).astype(F32)
    zeros = jnp.zeros((HEAD_DIM, 2 * MOBA_TOPK * LANES), BF16)
    out = jnp.zeros((1, LANES), F32)
    for sub in range(2):
        h = 2 * hp + sub
        slope = slope_ref[h]
        mine = (lane < HEAD_DIM) if sub == 0 else (lane >= HEAD_DIM)
        qz = jnp.where(mine, qpair, 0.0).astype(BF16)
        q8 = jnp.broadcast_to(qz, (SUBLANES, LANES))
        first = sub * 2 * MOBA_TOPK
        kt = jnp.concatenate([k_refs[first + idx][0, 0] for idx in range(2 * MOBA_TOPK)], axis=1).astype(BF16)
        vt = jnp.concatenate([v_refs[first + idx][0, 0] for idx in range(2 * MOBA_TOPK)], axis=1).astype(BF16)
        kt = jnp.concatenate([kt, zeros] if sub == 0 else [zeros, kt], axis=0)
        vt = jnp.concatenate([vt, zeros] if sub == 0 else [zeros, vt], axis=0)
        dist = []
        for r in range(MOBA_TOPK):
            blk = sel_ref[(n * N_HEADS + h) * MOBA_TOPK + r]
            for half in range(2):
                pos0 = (blk * MOBA_BLOCK + half * LANES).astype(F32)
                dist.append(float(past_len) - (pos0 + lanef))
        logits = _dot(q8, kt)[0:1, :] - slope * jnp.concatenate(dist, axis=1)
        s_own = jnp.sum(qz.astype(F32) * kown, axis=-1, keepdims=True)
        m = jnp.maximum(s_own, jnp.max(logits, axis=-1, keepdims=True))
        p_own = jnp.exp(s_own - m)
        p = jnp.exp(logits - m)
        l = p_own + jnp.sum(p, axis=-1, keepdims=True)
        p8 = jnp.broadcast_to(p.astype(BF16), (SUBLANES, p.shape[1]))
        acc = p_own.astype(BF16).astype(F32) * jnp.where(mine, vown, 0.0) + _dot_nt(p8, vt)[0:1, :]
        out = out + acc / l
    o_ref[0] = out


def _memattn_s_kernel(mq_ref, mk_ref, mv_ref, o_ref):
    mq = jnp.broadcast_to(mq_ref[0], (SUBLANES, MEM_W)).astype(BF16)
    o = _mem_attend_rows(mq, mk_ref[0].astype(BF16), mv_ref[0].astype(BF16))
    o_ref[0] = o[0:1, :]


def _post_s_kernel(x_ref, cact_ref, ob_ref, oc_ref, gl_ref, bg_ref, wpa_ref, wpb_ref, wpc_ref, wout_ref,
                   g2_ref, wup_ref, st_ref, wdw_ref, bdw_ref, wdown_ref, y_ref, up_ref):
    x = x_ref[...]
    dff = wdown_ref.shape[0]
    ya = _dot(cact_ref[...], wpa_ref[...])
    yb = _dot(ob_ref[...].astype(BF16), wpb_ref[...])
    yc = _dot(oc_ref[...].astype(BF16), wpc_ref[...])
    x1 = _merge_out(x, gl_ref[...] + bg_ref[...], ya, yb, yc, wout_ref[...])
    up = _dot(_rms_rows(x1, g2_ref[...]).astype(BF16), wup_ref[...])
    up_ref[...] = up
    cv = bdw_ref[...] + wdw_ref[FFN_CONV_WIDTH - 1:FFN_CONV_WIDTH, :] * up
    for k in range(FFN_CONV_WIDTH - 1):
        cv = cv + wdw_ref[k:k + 1, :] * st_ref[k]
    a = cv[:, 0:dff]
    b = cv[:, dff:]
    act = (a * _sigmoid(a) * b).astype(BF16)
    y_ref[...] = x1 + _dot(act, wdown_ref[...])


def _resident(shape):
    nd = len(shape)
    return pl.BlockSpec(shape, lambda *_: (0,) * nd, pipeline_mode=pl.Buffered(1))


def _params(n_axes):
    return pltpu.CompilerParams(dimension_semantics=("arbitrary",) * n_axes, vmem_limit_bytes=VMEM_LIMIT)


def kernel(x_prompt, x_sample, mem_prompt, cache_k, cache_v, page_table, state_conv, state_ffn_conv, cache_mem_k, cache_mem_v, norm1_g, w_in, b_gate, w_dw_a, b_dw_a, ln_a_g, ln_a_b, w_proj_a, q_norm_g, k_norm_g, w_proj_b, mem_norm_g, w_mem_kv, mq_norm_g, mk_norm_g, w_proj_c, w_out, norm2_g, w_up, w_dw_f, b_dw_f, w_down):
    n, t, d = x_prompt.shape
    nd, td, _ = x_sample.shape
    n_mem = mem_prompt.shape[1]
    n_pool, page_size = cache_k.shape[:2]
    n_pages = page_table.shape[1]
    past_len = n_pages * page_size
    dff = w_down.shape[0]
    nt = t // TM
    n_mix = 2 * CONV_CH + 3 * ATTN_W + MEM_W
    assert t % TM == 0 and nt <= GROUP and td == 1
    assert past_len % MOBA_BLOCK == 0 and MOBA_BLOCK == 2 * page_size and page_size == LANES
    n_blocks = past_len // MOBA_BLOCK

    row = lambda v: v.reshape(1, -1).astype(F32)
    w_in_b = w_in.astype(BF16)
    w_mix, w_gate = w_in_b[:, :n_mix], w_in_b[:, n_mix:]
    wpa, wpb, wpc = w_proj_a.astype(BF16), w_proj_b.astype(BF16), w_proj_c.astype(BF16)
    wout, wup, wdown = w_out.astype(BF16), w_up.astype(BF16), w_down.astype(BF16)
    g1, g2 = row(norm1_g), row(norm2_g)
    qg = row(jnp.tile(q_norm_g, N_HEADS))
    kg = row(jnp.tile(k_norm_g, N_HEADS))
    mqg = row(jnp.tile(mq_norm_g, MEM_HEADS))
    mkg = row(jnp.tile(mk_norm_g, MEM_HEADS))
    bdw_a, lng, lnb, bg, bdw_f = row(b_dw_a), row(ln_a_g), row(ln_a_b), row(b_gate), row(b_dw_f)
    grp = np.arange(ATTN_W)
    bd64 = jnp.asarray((grp[:, None] // HEAD_DIM == grp[None, :] // HEAD_DIM) / HEAD_DIM, BF16)
    bd128 = jnp.asarray((grp[:, None] // MEM_HEAD_DIM == grp[None, :] // MEM_HEAD_DIM) / MEM_HEAD_DIM, BF16)
    slopes = 2.0 ** (-8.0 * np.arange(1, N_HEADS + 1) / N_HEADS)
    slope_lanes = jnp.asarray(np.repeat(slopes, LANES)[None, :], F32)

    xp = x_prompt.reshape(n * t, d)
    tile = lambda w: pl.BlockSpec((TM, w), lambda b, i: (b * nt + i, 0))
    aw = N_HEADS * LANES
    vw = N_HEADS * V_ROWS
    blk3 = lambda r, c: pl.BlockSpec((1, r, c), lambda b, i: (b * nt + i, 0, 0))
    k_p, v_p, km, qt, ka, vt, mq, cact, utail = pl.pallas_call(
        _premix_kernel,
        grid=(n, nt),
        in_specs=[tile(d), _resident((1, d)), _resident((d, n_mix)), _resident((ATTN_W, ATTN_W)),
                  _resident((MEM_W, MEM_W)), _resident((1, ATTN_W)), _resident((1, ATTN_W)),
                  _resident((1, MEM_W)), _resident((CONV_WIDTH, CONV_CH)), _resident((1, CONV_CH)),
                  _resident((1, CONV_CH)), _resident((1, CONV_CH)), _resident((1, aw))],
        out_specs=[pl.BlockSpec((ATTN_W, TM), lambda b, i: (b, i)), pl.BlockSpec((ATTN_W, TM), lambda b, i: (b, i)),
                   pl.BlockSpec((1, 1, ATTN_W), lambda b, i: (b * nt + i, 0, 0)),
                   pl.BlockSpec((aw, TM), lambda b, i: (b, i)), blk3(TM, aw), blk3(vw, TM),
                   tile(MEM_W), tile(CONV_CH),
                   pl.BlockSpec((1, CONV_HALO, CONV_CH), lambda b, i: (b, 0, 0))],
        out_shape=[jax.ShapeDtypeStruct((n * ATTN_W, t), F32), jax.ShapeDtypeStruct((n * ATTN_W, t), F32),
                   jax.ShapeDtypeStruct((n * nt, 1, ATTN_W), F32),
                   jax.ShapeDtypeStruct((n * aw, t), BF16), jax.ShapeDtypeStruct((n * nt, TM, aw), BF16),
                   jax.ShapeDtypeStruct((n * nt, vw, TM), BF16), jax.ShapeDtypeStruct((n * t, MEM_W), BF16),
                   jax.ShapeDtypeStruct((n * t, CONV_CH), BF16),
                   jax.ShapeDtypeStruct((n, CONV_HALO, CONV_CH), F32)],
        scratch_shapes=[pltpu.VMEM((CONV_HALO + TM + SUBLANES, CONV_CH), F32)],
        compiler_params=_params(2),
        name="premix",
    )(xp, g1, w_mix, bd64, bd128, qg, kg, mqg, w_dw_a, bdw_a, lng, lnb, slope_lanes)

    km4 = km.reshape(n, nt, N_HEADS, HEAD_DIM).transpose(0, 2, 1, 3)
    km4 = jnp.pad(km4, ((0, 0), (0, 0), (0, GROUP - nt), (0, LANES - HEAD_DIM)))
    kmt = (km4[:, :, :, None, :] * jnp.eye(N_HEADS, dtype=F32)[None, :, None, :, None])
    kmt = kmt.reshape(n, N_HEADS * GROUP, aw).astype(BF16)

    seq3 = lambda r, c: pl.BlockSpec((nt, r, c), lambda b, i: (b, 0, 0))
    ob = pl.pallas_call(
        _moba_kernel,
        grid=(n, nt),
        in_specs=[pl.BlockSpec((aw, TM), lambda b, i: (b, i)), seq3(TM, aw), seq3(vw, TM),
                  pl.BlockSpec((1, N_HEADS * GROUP, aw), lambda b, i: (b, 0, 0))],
        out_specs=tile(ATTN_W),
        out_shape=jax.ShapeDtypeStruct((n * t, ATTN_W), BF16),
        scratch_shapes=[pltpu.VMEM((N_HEADS, LANES, TM), BF16), pltpu.VMEM((N_HEADS, 1, TM), F32),
                        pltpu.VMEM((N_HEADS, V_ROWS, TM), F32)]
        + [pltpu.VMEM((TM, TM), F32)] * N_HEADS + [pltpu.VMEM((1, TM), F32)] * N_HEADS,
        compiler_params=_params(2),
        name="moba",
    )(qt, ka, vt, kmt)

    mem = mem_prompt.reshape(n * n_mem, d)
    mtile = lambda w: pl.BlockSpec((n_mem, w), lambda b: (b, 0))
    mk_p, mv_p, mkb, mvb = pl.pallas_call(
        _memkv_kernel,
        grid=(n,),
        in_specs=[mtile(d), _resident((1, d)), _resident((d, 2 * MEM_W)), _resident((MEM_W, MEM_W)),
                  _resident((1, MEM_W))],
        out_specs=[mtile(MEM_W)] * 4,
        out_shape=[jax.ShapeDtypeStruct((n * n_mem, MEM_W), F32)] * 2
        + [jax.ShapeDtypeStruct((n * n_mem, MEM_W), BF16)] * 2,
        compiler_params=_params(1),
        name="memkv",
    )(mem, row(mem_norm_g), w_mem_kv.astype(BF16), bd128, mkg)

    memb = pl.BlockSpec((n_mem, MEM_W), lambda b, i: (b, 0))
    x1 = pl.pallas_call(
        _postmix_kernel,
        grid=(n, nt),
        in_specs=[tile(d), tile(CONV_CH), tile(ATTN_W), tile(MEM_W), memb, memb, _resident((1, d)),
                  _resident((d, 3 * d)), _resident((1, 3 * d)), _resident((CONV_CH, d)), _resident((ATTN_W, d)),
                  _resident((MEM_W, d)), _resident((d, d))],
        out_specs=tile(d),
        out_shape=jax.ShapeDtypeStruct((n * t, d), F32),
        compiler_params=_params(2),
        name="postmix",
    )(xp, cact, ob, mq, mkb, mvb, g1, w_gate, bg, wpa, wpb, wpc, wout)

    chunk = dff // 2
    assert chunk % LANES == 0
    y_p, ftail = pl.pallas_call(
        functools.partial(_ffn_kernel, chunk=chunk),
        grid=(n, nt),
        in_specs=[tile(d), _resident((1, d)), _resident((d, 2 * dff)), _resident((FFN_CONV_WIDTH, 2 * dff)),
                  _resident((1, 2 * dff)), _resident((dff, d))],
        out_specs=[tile(d), pl.BlockSpec((1, SUBLANES, 2 * dff), lambda b, i: (b, 0, 0))],
        out_shape=[jax.ShapeDtypeStruct((n * t, d), F32), jax.ShapeDtypeStruct((n, SUBLANES, 2 * dff), F32)],
        scratch_shapes=[pltpu.VMEM((SUBLANES + TM, 2 * dff), F32)],
        compiler_params=_params(2),
        name="ffn",
    )(x1, g2, wup, w_dw_f, bdw_f, wdown)

    xs = x_sample.reshape(nd, d)
    st_conv = state_conv.transpose(1, 0, 2)
    st_ffn = state_ffn_conv.transpose(1, 0, 2)
    vm = pltpu.CompilerParams(vmem_limit_bytes=VMEM_LIMIT)
    u_s, q_s, k_s, v_s, mq_s, gl_s, cact_s = pl.pallas_call(
        _premix_s_kernel,
        out_shape=[jax.ShapeDtypeStruct((nd, CONV_CH), F32), jax.ShapeDtypeStruct((nd, ATTN_W), F32),
                   jax.ShapeDtypeStruct((nd, ATTN_W), F32), jax.ShapeDtypeStruct((nd, ATTN_W), F32),
                   jax.ShapeDtypeStruct((nd, MEM_W), F32), jax.ShapeDtypeStruct((nd, 3 * d), F32),
                   jax.ShapeDtypeStruct((nd, CONV_CH), BF16)],
        compiler_params=vm,
        name="premix_s",
    )(xs, g1, w_in_b, bd64, bd128, qg, kg, mqg, st_conv, w_dw_a, bdw_a, lng, lnb)

    ck = cache_k.transpose(0, 2, 3, 1)
    cv = cache_v.transpose(0, 2, 3, 1)
    pt_flat = page_table.reshape(-1).astype(jnp.int32)
    pages_per_step = 16
    assert n_pages % pages_per_step == 0 and n_blocks <= LANES
    page_spec = lambda r: pl.BlockSpec(
        (1, N_HEADS, HEAD_DIM, page_size),
        lambda b, s, pt: (pt[b * n_pages + s * pages_per_step + r], 0, 0, 0))
    q_cols = jnp.broadcast_to(q_s.reshape(nd, N_HEADS, HEAD_DIM, 1), (nd, N_HEADS, HEAD_DIM, page_size))
    sel = pl.pallas_call(
        functools.partial(_block_gate_kernel, pages_per_step=pages_per_step, n_blocks=n_blocks),
        grid_spec=pltpu.PrefetchScalarGridSpec(
            num_scalar_prefetch=1,
            grid=(nd, n_pages // pages_per_step),
            in_specs=[pl.BlockSpec((1, N_HEADS, HEAD_DIM, page_size), lambda b, s, pt: (b, 0, 0, 0))]
            + [page_spec(r) for r in range(pages_per_step)],
            out_specs=pl.BlockSpec((1, N_HEADS, LANES), lambda b, s, pt: (b, 0, 0)),
            scratch_shapes=[pltpu.VMEM((N_HEADS, LANES), F32)]),
        out_shape=jax.ShapeDtypeStruct((nd, N_HEADS, LANES), jnp.int32),
        compiler_params=_params(2),
        name="block_gate",
    )(pt_flat, q_cols, *([ck] * pages_per_step))
    sel_flat = sel[:, :, :MOBA_TOPK].reshape(-1)

    n_pairs = N_HEADS // 2
    pair = lambda a: a.reshape(nd * n_pairs, 1, LANES)
    pair_spec = pl.BlockSpec((1, 1, LANES), lambda b, hp, pt, sl: (b * n_pairs + hp, 0, 0))

    def kv_spec(sub, r, half):
        def index(b, hp, pt, sl):
            blk = sl[(b * N_HEADS + 2 * hp + sub) * MOBA_TOPK + r]
            return (pt[b * n_pages + 2 * blk + half], 2 * hp + sub, 0, 0)
        return pl.BlockSpec((1, 1, HEAD_DIM, page_size), index)

    kv_specs = [kv_spec(sub, r, half) for sub in range(2) for r in range(MOBA_TOPK) for half in range(2)]
    ob_s = pl.pallas_call(
        functools.partial(_decode_attn_kernel, past_len=past_len),
        grid_spec=pltpu.PrefetchScalarGridSpec(
            num_scalar_prefetch=2,
            grid=(nd, n_pairs),
            in_specs=[pl.BlockSpec(memory_space=pltpu.SMEM), pair_spec, pair_spec, pair_spec]
            + kv_specs + kv_specs,
            out_specs=pair_spec),
        out_shape=jax.ShapeDtypeStruct((nd * n_pairs, 1, LANES), F32),
        compiler_params=_params(2),
        name="decode_attn",
    )(pt_flat, sel_flat, jnp.asarray(slopes, F32), pair(q_s), pair(k_s), pair(v_s),
      *([ck] * len(kv_specs)), *([cv] * len(kv_specs)))
    ob_s = ob_s.reshape(nd, ATTN_W)

    cm_spec = pl.BlockSpec((1, n_mem, MEM_W), lambda b: (b, 0, 0))
    one_spec = pl.BlockSpec((1, 1, MEM_W), lambda b: (b, 0, 0))
    oc_s = pl.pallas_call(
        _memattn_s_kernel,
        grid=(nd,),
        in_specs=[one_spec, cm_spec, cm_spec],
        out_specs=one_spec,
        out_shape=jax.ShapeDtypeStruct((nd, 1, MEM_W), F32),
        compiler_params=_params(1),
        name="memattn_s",
    )(mq_s.reshape(nd, 1, MEM_W), cache_mem_k.reshape(nd, n_mem, MEM_W), cache_mem_v.reshape(nd, n_mem, MEM_W))
    oc_s = oc_s.reshape(nd, MEM_W)

    y_s, up_s = pl.pallas_call(
        _post_s_kernel,
        out_shape=[jax.ShapeDtypeStruct((nd, d), F32), jax.ShapeDtypeStruct((nd, 2 * dff), F32)],
        compiler_params=vm,
        name="post_s",
    )(xs, cact_s, ob_s, oc_s, gl_s, bg, wpa, wpb, wpc, wout, g2, wup, st_ffn, w_dw_f, bdw_f, wdown)

    heads = lambda a, b, s: a.reshape(b, s, N_HEADS, HEAD_DIM)
    from_t = lambda a: a.reshape(n, N_HEADS, HEAD_DIM, t).transpose(0, 3, 1, 2)
    conv_p = utail[:, CONV_HALO - (CONV_WIDTH - 1):, :]
    conv_s = jnp.concatenate([state_conv[:, 1:, :], u_s[:, None, :]], axis=1)
    ffn_p = ftail[:, SUBLANES - (FFN_CONV_WIDTH - 1):, :]
    ffn_s = jnp.concatenate([state_ffn_conv[:, 1:, :], up_s[:, None, :]], axis=1)
    return (y_p.reshape(n, t, d), y_s.reshape(nd, td, d),
            from_t(k_p), from_t(v_p), heads(k_s, nd, td), heads(v_s, nd, td),
            conv_p, conv_s, ffn_p, ffn_s,
            mk_p.reshape(n, n_mem, MEM_HEADS, MEM_HEAD_DIM), mv_p.reshape(n, n_mem, MEM_HEADS, MEM_HEAD_DIM))
```

```python
import functools

import numpy as np
import jax
import jax.numpy as jnp
from jax import lax
from jax.experimental import pallas as pl
from jax.experimental.pallas import tpu as pltpu

F32 = jnp.float32
BF16 = jnp.bfloat16

EPS = 1e-6
CONV_CH = 512
CONV_WIDTH = 31
N_HEADS = 8
HEAD_DIM = 64
ATTN_W = N_HEADS * HEAD_DIM
MOBA_BLOCK = 256
MOBA_TOPK = 3
MEM_HEADS = 4
MEM_HEAD_DIM = 128
MEM_W = MEM_HEADS * MEM_HEAD_DIM
FFN_CONV_WIDTH = 3
LANES = 128
SUBLANES = 8
TM = MOBA_BLOCK
CONV_HALO = 32
MASK_NEG = -float(2 ** 30)
GROUP = 16
V_ROWS = HEAD_DIM + 16
VMEM_LIMIT = 56 * 1024 * 1024

X_SEL = HEAD_DIM
X_RQ = HEAD_DIM + GROUP
X_ONE = X_RQ + 1
X_TQ = X_RQ + 2
X_ONE2 = X_RQ + 3


def _dot(a, b):
    return jnp.dot(a, b, preferred_element_type=F32)


def _dot_nt(a, b):
    return lax.dot_general(a, b, (((1,), (1,)), ((), ())), preferred_element_type=F32)


def _rms_rows(x, g):
    return x * lax.rsqrt(jnp.mean(x * x, axis=-1, keepdims=True) + EPS) * g


def _group_rms(z, bd, g):
    sq = z * z
    hi = sq.astype(BF16)
    lo = (sq - hi.astype(F32)).astype(BF16)
    ms = _dot(hi, bd) + _dot(lo, bd)
    return z * lax.rsqrt(ms + EPS) * g


def _exact_zero(v):
    bits = pltpu.bitcast(v, jnp.uint32)
    half = jnp.uint32(16)
    return pltpu.bitcast(lax.shift_right_logical(lax.shift_right_logical(bits, half), half), F32)


def _sigmoid(x):
    return 1.0 / (1.0 + jnp.exp(-x))


def _layernorm_silu(c, g, b):
    mu = jnp.mean(c, axis=-1, keepdims=True)
    xc = c - mu
    var = jnp.mean(xc * xc, axis=-1, keepdims=True)
    y = xc * lax.rsqrt(var + EPS) * g + b
    return y * _sigmoid(y)


def _premix_kernel(x_ref, g1_ref, w_ref, bd64_ref, bd128_ref, qg_ref, kg_ref, mqg_ref,
                   wdw_ref, bdw_ref, lng_ref, lnb_ref, slope_ref,
                   k_ref, v_ref, km_ref, qt_ref, ka_ref, vt_ref, mq_ref, cact_ref, utail_ref,
                   ubuf):
    t = pl.program_id(1)
    nt = pl.num_programs(1)
    c = CONV_CH
    hn = _rms_rows(x_ref[...], g1_ref[...]).astype(BF16)

    a = _dot(hn, w_ref[:, 0:c])
    g = _dot(hn, w_ref[:, c:2 * c])
    u = a * _sigmoid(g)

    @pl.when(t == 0)
    def _():
        ubuf[0:CONV_HALO, :] = jnp.zeros((CONV_HALO, c), F32)
        ubuf[CONV_HALO + TM:, :] = jnp.zeros((SUBLANES, c), F32)

    ubuf[CONV_HALO:CONV_HALO + TM, :] = u
    acc = jnp.broadcast_to(bdw_ref[...], (TM, c))
    base = CONV_HALO - (CONV_WIDTH - 1)
    span = TM + 2 * SUBLANES
    for b in range(SUBLANES):
        part = None
        for k in range(b, CONV_WIDTH, SUBLANES):
            term = wdw_ref[k:k + 1, :] * ubuf[k - b:k - b + span, :]
            part = term if part is None else part + term
        acc = acc + part[base + b:base + b + TM, :]
    ubuf[0:CONV_HALO, :] = ubuf[TM:TM + CONV_HALO, :]
    cact_ref[...] = _layernorm_silu(acc, lng_ref[...], lnb_ref[...]).astype(BF16)

    @pl.when(t == nt - 1)
    def _():
        utail_ref[0] = u[TM - CONV_HALO:, :]

    o = 2 * c
    qn = _group_rms(_dot(hn, w_ref[:, o:o + ATTN_W]), bd64_ref[...], qg_ref[...]) * (HEAD_DIM ** -0.5)
    o += ATTN_W
    kn = _group_rms(_dot(hn, w_ref[:, o:o + ATTN_W]), bd64_ref[...], kg_ref[...])
    o += ATTN_W
    zv = _dot(hn, w_ref[:, o:o + ATTN_W])
    o += ATTN_W
    mq = _group_rms(_dot(hn, w_ref[:, o:o + MEM_W]), bd128_ref[...], mqg_ref[...])

    ones_rows = jnp.where(lax.broadcasted_iota(jnp.int32, (V_ROWS - HEAD_DIM, TM), 0) == 0, 1.0, 0.0).astype(BF16)
    for cb in range(ATTN_W // LANES):
        k_ref[LANES * cb:LANES * (cb + 1), :] = kn[:, LANES * cb:LANES * (cb + 1)].T
        vt = zv[:, LANES * cb:LANES * (cb + 1)].T
        v_ref[LANES * cb:LANES * (cb + 1), :] = vt
        for sub in range(2):
            r0 = V_ROWS * (2 * cb + sub)
            vt_ref[0, r0:r0 + HEAD_DIM, :] = vt[HEAD_DIM * sub:HEAD_DIM * (sub + 1), :].astype(BF16)
            vt_ref[0, r0 + HEAD_DIM:r0 + V_ROWS, :] = ones_rows
    mq_ref[...] = mq.astype(BF16)
    km_ref[0] = jnp.mean(kn, axis=0, keepdims=True)

    lane = lax.broadcasted_iota(jnp.int32, (TM, LANES), 1)
    rowf = lax.broadcasted_iota(jnp.int32, (TM, LANES), 0).astype(F32)
    tf = t.astype(F32)
    zero = jnp.zeros((TM, LANES), F32)
    eq = jnp.where(lane == X_RQ, rowf,
                   jnp.where(lane == X_ONE, 1.0,
                             jnp.where(lane == X_TQ, tf,
                                       jnp.where(lane == X_ONE2, 1.0, zero))))
    for hp in range(N_HEADS // 2):
        xq = qn[:, LANES * hp:LANES * (hp + 1)]
        xk = kn[:, LANES * hp:LANES * (hp + 1)]
        for sub in range(2):
            h = 2 * hp + sub
            if sub == 1:
                xq = pltpu.roll(xq, HEAD_DIM, 1)
                xk = pltpu.roll(xk, HEAD_DIM, 1)
            sl = slope_ref[:, LANES * h:LANES * (h + 1)]
            ek = jnp.where(lane == X_SEL + t, MASK_NEG, zero)
            ek = jnp.where(lane == X_RQ, -sl,
                           jnp.where(lane == X_ONE, sl * rowf,
                                     jnp.where(lane == X_TQ, -sl * MOBA_BLOCK,
                                               jnp.where(lane == X_ONE2, sl * (MOBA_BLOCK * tf), ek))))
            qt_ref[LANES * h:LANES * (h + 1), :] = jnp.where(lane < HEAD_DIM, xq, eq).T.astype(BF16)
            ka_ref[0, :, LANES * h:LANES * (h + 1)] = jnp.where(lane < HEAD_DIM, xk, ek).astype(BF16)


def _moba_kernel(qt_ref, ka_ref, vt_ref, kmt_ref, o_ref, qh_scr, m_scr, acc_scr, *stage):
    i = pl.program_id(1)
    s_scr, mx_scr = stage[:N_HEADS], stage[N_HEADS:]
    key = lax.broadcasted_iota(jnp.int32, (TM, TM), 0)
    qry = lax.broadcasted_iota(jnp.int32, (TM, TM), 1)
    causal = key <= qry

    def scores(j, h, diagonal):
        q = qt_ref[LANES * h:LANES * (h + 1), :] if diagonal else qh_scr[h]
        s = _dot(ka_ref[j, :, LANES * h:LANES * (h + 1)], q)
        if diagonal:
            s = jnp.where(causal, s, -jnp.inf)
        s_scr[h][...] = s
        mx_scr[h][...] = jnp.max(s, axis=0, keepdims=True)

    for h in range(N_HEADS):
        scores(i, h, True)

    grow = lax.broadcasted_iota(jnp.int32, (N_HEADS * GROUP, TM), 0)
    blk = grow & (GROUP - 1)
    past = blk < i

    gate = _dot(kmt_ref[0], qt_ref[...])
    g = jnp.where(past, gate, -jnp.inf)
    rank = jnp.zeros(g.shape, F32)
    for s in range(1, GROUP):
        up = pltpu.roll(g, N_HEADS * GROUP - s, 0)
        down = pltpu.roll(g, GROUP - s, 0)
        wrapped = blk >= GROUP - s
        partner = jnp.where(wrapped, down, up)
        beats = jnp.where(wrapped, jnp.where(partner >= g, 1.0, 0.0), jnp.where(partner > g, 1.0, 0.0))
        rank = rank + beats
    nsel = jnp.where(past, jnp.where(rank >= MOBA_TOPK, 1.0, 0.0), 0.0).astype(BF16)
    for h in range(N_HEADS):
        r0 = LANES * h
        qh_scr[h, 0:X_SEL, :] = qt_ref[r0:r0 + X_SEL, :]
        qh_scr[h, X_SEL:X_RQ, :] = nsel[GROUP * h:GROUP * (h + 1), :]
        qh_scr[h, X_RQ:LANES, :] = qt_ref[r0 + X_RQ:r0 + LANES, :]

    def accumulate(j, h):
        m_old = m_scr[h]
        m_new = jnp.maximum(m_old, mx_scr[h][...])
        alpha = jnp.exp(m_old - m_new)
        p = jnp.exp(s_scr[h][...] - m_new).astype(BF16)
        acc_scr[h] = alpha * acc_scr[h] + _dot(vt_ref[j, V_ROWS * h:V_ROWS * (h + 1), :], p)
        m_scr[h] = m_new

    m_scr[...] = jnp.full(m_scr.shape, -jnp.inf, F32)
    acc_scr[...] = jnp.zeros(acc_scr.shape, F32)

    def body(j, carry):
        prev = jnp.where(j == 0, i, j - 1)
        for h in range(N_HEADS):
            accumulate(prev, h)
            scores(j, h, False)
        return carry

    lax.fori_loop(0, i, body, 0)
    last = jnp.where(i == 0, i, i - 1)
    for h in range(N_HEADS):
        accumulate(last, h)
    outs = []
    for h in range(N_HEADS):
        acc = acc_scr[h]
        outs.append(acc[0:HEAD_DIM, :] / acc[HEAD_DIM:HEAD_DIM + 1, :])
    o_ref[...] = jnp.concatenate(outs, axis=0).T.astype(BF16)


def _memkv_kernel(mem_ref, g_ref, w_ref, bd128_ref, mkg_ref, mk_ref, mv_ref, mkb_ref, mvb_ref):
    hn = _rms_rows(mem_ref[...], g_ref[...]).astype(BF16)
    mk = _group_rms(_dot(hn, w_ref[:, 0:MEM_W]), bd128_ref[...], mkg_ref[...])
    mv = _dot(hn, w_ref[:, MEM_W:2 * MEM_W])
    mk_ref[...] = mk
    mv_ref[...] = mv
    mkb_ref[...] = mk.astype(BF16)
    mvb_ref[...] = mv.astype(BF16)


def _mem_attend_rows(mq, mk, mv):
    outs = []
    for hh in range(MEM_HEADS):
        sl = slice(MEM_HEAD_DIM * hh, MEM_HEAD_DIM * (hh + 1))
        s = _dot_nt(mq[:, sl], mk[:, sl]) * (MEM_HEAD_DIM ** -0.5)
        m = jnp.max(s, axis=-1, keepdims=True)
        p = jnp.exp(s - m)
        l = jnp.sum(p, axis=-1, keepdims=True)
        outs.append(_dot(p.astype(BF16), mv[:, sl]) / l)
    return jnp.concatenate(outs, axis=-1)


def _merge_out(x, gl, ya, yb, yc, wout):
    d = x.shape[-1]
    merged = (_sigmoid(gl[:, 0:d]) * ya + _sigmoid(gl[:, d:2 * d]) * yb + _sigmoid(gl[:, 2 * d:3 * d]) * yc)
    return x + _dot(merged.astype(BF16), wout)


def _postmix_kernel(pt_ref, x_ref, cact_ref, ob_ref, mq_ref, mk_ref, mv_ref, g1_ref, wg_ref, bg_ref,
                    wpa_ref, wpb_ref, wpc_ref, wout_ref, q_ref, *refs, pages_per_step, n_blocks):
    page_refs = refs[:pages_per_step]
    x1_ref, sel_ref, gate_scr, merged_scr = refs[pages_per_step:]
    step = pl.program_id(0) * pl.num_programs(1) + pl.program_id(1)

    @pl.when(step == 0)
    def _():
        gate_scr[...] = jnp.zeros(gate_scr.shape, F32)

    blocks_per_step = pages_per_step // PAGES_PER_BLOCK
    steps_per_sample = n_blocks // blocks_per_step
    share = step % steps_per_sample
    n_chunks = 4
    chunk = pages_per_step // n_chunks
    assert chunk % PAGES_PER_BLOCK == 0 and chunk * n_chunks == pages_per_step

    def gate_chunk(g, c):
        first = share * blocks_per_step + c * (chunk // PAGES_PER_BLOCK)
        return _gate_scores(g, first, q_ref, page_refs[c * chunk:(c + 1) * chunk])

    g = jnp.where(share == 0, 0.0, gate_scr[...])
    x = x_ref[...]
    d = x.shape[-1]
    hn = _rms_rows(x, g1_ref[...]).astype(BF16)
    oc = _mem_attend_rows(mq_ref[...], mk_ref[...], mv_ref[...]).astype(BF16)
    width = d // n_chunks
    for c in range(n_chunks):
        g = gate_chunk(g, c)
        tie = jnp.concatenate([_exact_zero(g)[0:1, :]] * (width // LANES), axis=1)
        merged = None
        for br, (src, w_ref) in enumerate(((cact_ref[...], wpa_ref), (ob_ref[...], wpb_ref), (oc, wpc_ref))):
            cols = slice(d * br + width * c, d * br + width * (c + 1))
            gl = _dot(hn, wg_ref[:, cols]) + (bg_ref[:, cols] + tie)
            term = _sigmoid(gl) * _dot(src, w_ref[:, width * c:width * (c + 1)])
            merged = term if merged is None else merged + term
        merged_scr[:, width * c:width * (c + 1)] = merged.astype(BF16)
    x1_ref[...] = x + _dot(merged_scr[...], wout_ref[...])
    gate_scr[...] = g
    _gate_select(share == steps_per_sample - 1, gate_scr, sel_ref, n_blocks)


def _ffn_kernel(x1_ref, g2_ref, wup_ref, wdw_ref, bdw_ref, wdown_ref, y_ref, tail_ref, upbuf, *, chunk):
    t = pl.program_id(1)
    nt = pl.num_programs(1)
    x1 = x1_ref[...]
    dff = wdown_ref.shape[0]
    hn = _rms_rows(x1, g2_ref[...]).astype(BF16)

    @pl.when(t == 0)
    def _():
        upbuf[0:SUBLANES, :] = jnp.zeros((SUBLANES, 2 * dff), F32)

    for c in range(0, 2 * dff, chunk):
        upbuf[SUBLANES:SUBLANES + TM, c:c + chunk] = _dot(hn, wup_ref[:, c:c + chunk])

    def conv(c):
        out = bdw_ref[:, c:c + chunk]
        for k in range(FFN_CONV_WIDTH):
            r0 = SUBLANES - (FFN_CONV_WIDTH - 1) + k
            out = out + wdw_ref[k:k + 1, c:c + chunk] * upbuf[r0:r0 + TM, c:c + chunk]
        return out

    y = x1
    for c in range(0, dff, chunk):
        a = conv(c)
        b = conv(dff + c)
        act = (a * _sigmoid(a) * b).astype(BF16)
        y = y + _dot(act, wdown_ref[c:c + chunk, :])
    y_ref[...] = y

    @pl.when(t == nt - 1)
    def _():
        tail_ref[0] = upbuf[TM:TM + SUBLANES, :]

    upbuf[0:SUBLANES, :] = upbuf[TM:TM + SUBLANES, :]


def _premix_s_kernel(x_ref, g1_ref, w_ref, bd64_ref, bd128_ref, qg_ref, kg_ref, mqg_ref,
                     st_ref, wdw_ref, bdw_ref, lng_ref, lnb_ref,
                     u_ref, q_ref, k_ref, v_ref, mq_ref, gl_ref, cact_ref):
    c = CONV_CH
    hn = _rms_rows(x_ref[...], g1_ref[...]).astype(BF16)
    a = _dot(hn, w_ref[:, 0:c])
    g = _dot(hn, w_ref[:, c:2 * c])
    u = a * _sigmoid(g)
    u_ref[...] = u
    acc = bdw_ref[...] + wdw_ref[CONV_WIDTH - 1:CONV_WIDTH, :] * u
    for k in range(CONV_WIDTH - 1):
        acc = acc + wdw_ref[k:k + 1, :] * st_ref[k]
    cact_ref[...] = _layernorm_silu(acc, lng_ref[...], lnb_ref[...]).astype(BF16)

    o = 2 * c
    q_ref[...] = _group_rms(_dot(hn, w_ref[:, o:o + ATTN_W]), bd64_ref[...], qg_ref[...])
    o += ATTN_W
    k_ref[...] = _group_rms(_dot(hn, w_ref[:, o:o + ATTN_W]), bd64_ref[...], kg_ref[...])
    o += ATTN_W
    v_ref[...] = _dot(hn, w_ref[:, o:o + ATTN_W])
    o += ATTN_W
    mq_ref[...] = _group_rms(_dot(hn, w_ref[:, o:o + MEM_W]), bd128_ref[...], mqg_ref[...])
    o += MEM_W
    gl_ref[...] = _dot(hn, w_ref[:, o:])


PAGES_PER_BLOCK = MOBA_BLOCK // LANES


def _gate_scores(g, first_block, q_ref, page_refs):
    lane = lax.broadcasted_iota(jnp.int32, (N_HEADS, LANES), 1)
    head = lax.broadcasted_iota(jnp.int32, (N_HEADS, LANES), 0)
    for r in range(0, len(page_refs), PAGES_PER_BLOCK):
        bidx = first_block + r // PAGES_PER_BLOCK
        for h in range(N_HEADS):
            qc = q_ref[0, h]
            prod = page_refs[r][0, h] * qc + page_refs[r + 1][0, h] * qc
            tot = jnp.sum(jnp.sum(prod, axis=0, keepdims=True), axis=1, keepdims=True) * (1.0 / MOBA_BLOCK)
            g = jnp.where(lane == bidx, jnp.where(head == h, tot, g), g)
    return g


def _gate_select(last, gate_scr, sel_ref, n_blocks):
    @pl.when(last)
    def _():
        g = gate_scr[:, 0:n_blocks]
        bl = lax.broadcasted_iota(jnp.int32, (N_HEADS, n_blocks), 1)
        rank = jnp.zeros((N_HEADS, n_blocks), F32)
        for b in range(n_blocks):
            other = g[:, b:b + 1]
            rank = rank + jnp.where(bl > b, jnp.where(other >= g, 1.0, 0.0), jnp.where(other > g, 1.0, 0.0))
        lane_o = lax.broadcasted_iota(jnp.int32, (N_HEADS, LANES), 1)
        out = jnp.zeros((N_HEADS, LANES), F32)
        blf = bl.astype(F32)
        for r in range(MOBA_TOPK):
            idx = jnp.sum(jnp.where(rank == float(r), blf, 0.0), axis=-1, keepdims=True)
            out = jnp.where(lane_o == r, idx, out)
        sel_ref[0] = out.astype(jnp.int32)


def _decode_attn_kernel(pt_ref, sel_ref, slope_ref, q_ref, kown_ref, vown_ref, *refs, past_len):
    pages_per_block = MOBA_BLOCK // LANES
    tiles_per_head = MOBA_TOPK * pages_per_block
    n_tiles = N_HEADS * tiles_per_head
    k_refs = refs[:n_tiles]
    v_refs = refs[n_tiles:2 * n_tiles]
    o_ref = refs[2 * n_tiles]
    n = pl.program_id(0)
    lane = lax.broadcasted_iota(jnp.int32, (1, LANES), 1)
    lanef = lane.astype(F32)
    zeros = jnp.zeros((HEAD_DIM, tiles_per_head * LANES), BF16)
    scale = HEAD_DIM ** -0.5

    def head_tiles(tile_refs, h):
        t = jnp.concatenate([tile_refs[h * tiles_per_head + idx][0, 0] for idx in range(tiles_per_head)], axis=1)
        t = t.astype(BF16)
        return jnp.concatenate([t, zeros] if h % 2 == 0 else [zeros, t], axis=0)

    def own_half(h):
        return (lane < HEAD_DIM) if h % 2 == 0 else (lane >= HEAD_DIM)

    def pair_lanes(ref, h):
        return ref[0][:, LANES * (h // 2):LANES * (h // 2 + 1)]

    qz, raw = [], []
    for h in range(N_HEADS):
        qz.append(jnp.where(own_half(h), pair_lanes(q_ref, h) * scale, 0.0).astype(BF16))
        raw.append(_dot(jnp.broadcast_to(qz[h], (SUBLANES, LANES)), head_tiles(k_refs, h))[0:1, :])
    probs = []
    for h in range(N_HEADS):
        dist = []
        for r in range(MOBA_TOPK):
            blk = sel_ref[(n * N_HEADS + h) * MOBA_TOPK + r]
            for half in range(pages_per_block):
                pos0 = (blk * MOBA_BLOCK + half * LANES).astype(F32)
                dist.append(float(past_len) - (pos0 + lanef))
        logits = raw[h] - slope_ref[h] * jnp.concatenate(dist, axis=1)
        kown = pair_lanes(kown_ref, h).astype(BF16).astype(F32)
        s_own = jnp.sum(qz[h].astype(F32) * kown, axis=-1, keepdims=True)
        m = jnp.maximum(s_own, jnp.max(logits, axis=-1, keepdims=True))
        p_own = jnp.exp(s_own - m)
        p = jnp.exp(logits - m)
        probs.append((p, p_own, p_own + jnp.sum(p, axis=-1, keepdims=True)))
    outs = []
    for h in range(N_HEADS):
        p, p_own, l = probs[h]
        p8 = jnp.broadcast_to(p.astype(BF16), (SUBLANES, p.shape[1]))
        vown = jnp.where(own_half(h), pair_lanes(vown_ref, h).astype(BF16).astype(F32), 0.0)
        acc = p_own.astype(BF16).astype(F32) * vown + _dot_nt(p8, head_tiles(v_refs, h))[0:1, :]
        outs.append(acc / l)
    o_ref[0] = jnp.concatenate([outs[h] + outs[h + 1] for h in range(0, N_HEADS, 2)], axis=1)


def _memattn_s_kernel(mq_ref, mk_ref, mv_ref, o_ref, *, group, n_mem):
    for s in range(group):
        mq = jnp.broadcast_to(mq_ref[s], (SUBLANES, MEM_W)).astype(BF16)
        outs = []
        for hh in range(MEM_HEADS):
            mk = mk_ref[s, pl.ds(hh, n_mem, stride=MEM_HEADS), :].astype(BF16)
            mv = mv_ref[s, pl.ds(hh, n_mem, stride=MEM_HEADS), :].astype(BF16)
            sc = _dot_nt(mq[:, MEM_HEAD_DIM * hh:MEM_HEAD_DIM * (hh + 1)], mk) * (MEM_HEAD_DIM ** -0.5)
            m = jnp.max(sc, axis=-1, keepdims=True)
            p = jnp.exp(sc - m)
            l = jnp.sum(p, axis=-1, keepdims=True)
            outs.append(_dot(p.astype(BF16), mv) / l)
        o_ref[s] = jnp.concatenate(outs, axis=-1)[0:1, :]


def _post_s_kernel(x_ref, cact_ref, ob_ref, oc_ref, gl_ref, bg_ref, wpa_ref, wpb_ref, wpc_ref, wout_ref,
                   g2_ref, wup_ref, st_ref, wdw_ref, bdw_ref, wdown_ref, y_ref, up_ref):
    x = x_ref[...]
    dff = wdown_ref.shape[0]
    ya = _dot(cact_ref[...], wpa_ref[...])
    yb = _dot(ob_ref[...].astype(BF16), wpb_ref[...])
    yc = _dot(oc_ref[...].astype(BF16), wpc_ref[...])
    x1 = _merge_out(x, gl_ref[...] + bg_ref[...], ya, yb, yc, wout_ref[...])
    up = _dot(_rms_rows(x1, g2_ref[...]).astype(BF16), wup_ref[...])
    up_ref[...] = up
    cv = bdw_ref[...] + wdw_ref[FFN_CONV_WIDTH - 1:FFN_CONV_WIDTH, :] * up
    for k in range(FFN_CONV_WIDTH - 1):
        cv = cv + wdw_ref[k:k + 1, :] * st_ref[k]
    a = cv[:, 0:dff]
    b = cv[:, dff:]
    act = (a * _sigmoid(a) * b).astype(BF16)
    y_ref[...] = x1 + _dot(act, wdown_ref[...])


def _resident(shape):
    nd = len(shape)
    return pl.BlockSpec(shape, lambda *_: (0,) * nd, pipeline_mode=pl.Buffered(1))


def _params(n_axes):
    return pltpu.CompilerParams(dimension_semantics=("arbitrary",) * n_axes, vmem_limit_bytes=VMEM_LIMIT)


def kernel(x_prompt, x_sample, mem_prompt, cache_k, cache_v, page_table, state_conv, state_ffn_conv, cache_mem_k, cache_mem_v, norm1_g, w_in, b_gate, w_dw_a, b_dw_a, ln_a_g, ln_a_b, w_proj_a, q_norm_g, k_norm_g, w_proj_b, mem_norm_g, w_mem_kv, mq_norm_g, mk_norm_g, w_proj_c, w_out, norm2_g, w_up, w_dw_f, b_dw_f, w_down):
    n, t, d = x_prompt.shape
    nd, td, _ = x_sample.shape
    n_mem = mem_prompt.shape[1]
    n_pool, page_size = cache_k.shape[:2]
    n_pages = page_table.shape[1]
    past_len = n_pages * page_size
    dff = w_down.shape[0]
    nt = t // TM
    n_mix = 2 * CONV_CH + 3 * ATTN_W + MEM_W
    assert t % TM == 0 and nt <= GROUP and td == 1
    assert past_len % MOBA_BLOCK == 0 and MOBA_BLOCK == 2 * page_size and page_size == LANES
    n_blocks = past_len // MOBA_BLOCK

    row = lambda v: v.reshape(1, -1).astype(F32)
    w_in_b = w_in.astype(BF16)
    w_mix, w_gate = w_in_b[:, :n_mix], w_in_b[:, n_mix:]
    wpa, wpb, wpc = w_proj_a.astype(BF16), w_proj_b.astype(BF16), w_proj_c.astype(BF16)
    wout, wup, wdown = w_out.astype(BF16), w_up.astype(BF16), w_down.astype(BF16)
    g1, g2 = row(norm1_g), row(norm2_g)
    qg = row(jnp.tile(q_norm_g, N_HEADS))
    kg = row(jnp.tile(k_norm_g, N_HEADS))
    mqg = row(jnp.tile(mq_norm_g, MEM_HEADS))
    mkg = row(jnp.tile(mk_norm_g, MEM_HEADS))
    bdw_a, lng, lnb, bg, bdw_f = row(b_dw_a), row(ln_a_g), row(ln_a_b), row(b_gate), row(b_dw_f)
    grp = np.arange(ATTN_W)
    bd64 = jnp.asarray((grp[:, None] // HEAD_DIM == grp[None, :] // HEAD_DIM) / HEAD_DIM, BF16)
    bd128 = jnp.asarray((grp[:, None] // MEM_HEAD_DIM == grp[None, :] // MEM_HEAD_DIM) / MEM_HEAD_DIM, BF16)
    slopes = 2.0 ** (-8.0 * np.arange(1, N_HEADS + 1) / N_HEADS)
    slope_lanes = jnp.asarray(np.repeat(slopes, LANES)[None, :], F32)

    xp = x_prompt.reshape(n * t, d)
    tile = lambda w: pl.BlockSpec((TM, w), lambda b, i, *_: (b * nt + i, 0))
    aw = N_HEADS * LANES
    vw = N_HEADS * V_ROWS
    blk3 = lambda r, c: pl.BlockSpec((1, r, c), lambda b, i: (b * nt + i, 0, 0))
    k_p, v_p, km, qt, ka, vt, mq, cact, utail = pl.pallas_call(
        _premix_kernel,
        grid=(n, nt),
        in_specs=[tile(d), _resident((1, d)), _resident((d, n_mix)), _resident((ATTN_W, ATTN_W)),
                  _resident((MEM_W, MEM_W)), _resident((1, ATTN_W)), _resident((1, ATTN_W)),
                  _resident((1, MEM_W)), _resident((CONV_WIDTH, CONV_CH)), _resident((1, CONV_CH)),
                  _resident((1, CONV_CH)), _resident((1, CONV_CH)), _resident((1, aw))],
        out_specs=[pl.BlockSpec((ATTN_W, TM), lambda b, i: (b, i)), pl.BlockSpec((ATTN_W, TM), lambda b, i: (b, i)),
                   pl.BlockSpec((1, 1, ATTN_W), lambda b, i: (b * nt + i, 0, 0)),
                   pl.BlockSpec((aw, TM), lambda b, i: (b, i)), blk3(TM, aw), blk3(vw, TM),
                   tile(MEM_W), tile(CONV_CH),
                   pl.BlockSpec((1, CONV_HALO, CONV_CH), lambda b, i: (b, 0, 0))],
        out_shape=[jax.ShapeDtypeStruct((n * ATTN_W, t), F32), jax.ShapeDtypeStruct((n * ATTN_W, t), F32),
                   jax.ShapeDtypeStruct((n * nt, 1, ATTN_W), F32),
                   jax.ShapeDtypeStruct((n * aw, t), BF16), jax.ShapeDtypeStruct((n * nt, TM, aw), BF16),
                   jax.ShapeDtypeStruct((n * nt, vw, TM), BF16), jax.ShapeDtypeStruct((n * t, MEM_W), BF16),
                   jax.ShapeDtypeStruct((n * t, CONV_CH), BF16),
                   jax.ShapeDtypeStruct((n, CONV_HALO, CONV_CH), F32)],
        scratch_shapes=[pltpu.VMEM((CONV_HALO + TM + SUBLANES, CONV_CH), F32)],
        compiler_params=_params(2),
        name="premix",
    )(xp, g1, w_mix, bd64, bd128, qg, kg, mqg, w_dw_a, bdw_a, lng, lnb, slope_lanes)

    km4 = km.reshape(n, nt, N_HEADS, HEAD_DIM).transpose(0, 2, 1, 3)
    km4 = jnp.pad(km4, ((0, 0), (0, 0), (0, GROUP - nt), (0, LANES - HEAD_DIM)))
    kmt = (km4[:, :, :, None, :] * jnp.eye(N_HEADS, dtype=F32)[None, :, None, :, None])
    kmt = kmt.reshape(n, N_HEADS * GROUP, aw).astype(BF16)

    seq3 = lambda r, c: pl.BlockSpec((nt, r, c), lambda b, i: (b, 0, 0))
    ob = pl.pallas_call(
        _moba_kernel,
        grid=(n, nt),
        in_specs=[pl.BlockSpec((aw, TM), lambda b, i: (b, i)), seq3(TM, aw), seq3(vw, TM),
                  pl.BlockSpec((1, N_HEADS * GROUP, aw), lambda b, i: (b, 0, 0))],
        out_specs=tile(ATTN_W),
        out_shape=jax.ShapeDtypeStruct((n * t, ATTN_W), BF16),
        scratch_shapes=[pltpu.VMEM((N_HEADS, LANES, TM), BF16), pltpu.VMEM((N_HEADS, 1, TM), F32),
                        pltpu.VMEM((N_HEADS, V_ROWS, TM), F32)]
        + [pltpu.VMEM((TM, TM), F32)] * N_HEADS + [pltpu.VMEM((1, TM), F32)] * N_HEADS,
        compiler_params=_params(2),
        name="moba",
    )(qt, ka, vt, kmt)

    mem = mem_prompt.reshape(n * n_mem, d)
    mtile = lambda w: pl.BlockSpec((n_mem, w), lambda b: (b, 0))
    mk_p, mv_p, mkb, mvb = pl.pallas_call(
        _memkv_kernel,
        grid=(n,),
        in_specs=[mtile(d), _resident((1, d)), _resident((d, 2 * MEM_W)), _resident((MEM_W, MEM_W)),
                  _resident((1, MEM_W))],
        out_specs=[mtile(MEM_W)] * 4,
        out_shape=[jax.ShapeDtypeStruct((n * n_mem, MEM_W), F32)] * 2
        + [jax.ShapeDtypeStruct((n * n_mem, MEM_W), BF16)] * 2,
        compiler_params=_params(1),
        name="memkv",
    )(mem, row(mem_norm_g), w_mem_kv.astype(BF16), bd128, mkg)

    xs = x_sample.reshape(nd, d)
    st_conv = state_conv.transpose(1, 0, 2)
    st_ffn = state_ffn_conv.transpose(1, 0, 2)
    vm = pltpu.CompilerParams(vmem_limit_bytes=VMEM_LIMIT)
    u_s, q_s, k_s, v_s, mq_s, gl_s, cact_s = pl.pallas_call(
        _premix_s_kernel,
        out_shape=[jax.ShapeDtypeStruct((nd, CONV_CH), F32), jax.ShapeDtypeStruct((nd, ATTN_W), F32),
                   jax.ShapeDtypeStruct((nd, ATTN_W), F32), jax.ShapeDtypeStruct((nd, ATTN_W), F32),
                   jax.ShapeDtypeStruct((nd, MEM_W), F32), jax.ShapeDtypeStruct((nd, 3 * d), F32),
                   jax.ShapeDtypeStruct((nd, CONV_CH), BF16)],
        compiler_params=vm,
        name="premix_s",
    )(xs, g1, w_in_b, bd64, bd128, qg, kg, mqg, st_conv, w_dw_a, bdw_a, lng, lnb)

    ck = cache_k.transpose(0, 2, 3, 1)
    cv = cache_v.transpose(0, 2, 3, 1)
    pt_flat = page_table.reshape(-1).astype(jnp.int32)
    q_cols = jnp.broadcast_to(q_s.reshape(nd, N_HEADS, HEAD_DIM, 1), (nd, N_HEADS, HEAD_DIM, page_size))
    assert (n * nt) % nd == 0 and n_pages % ((n * nt) // nd) == 0 and n_blocks <= LANES
    steps_per_sample = (n * nt) // nd
    pages_per_step = n_pages // steps_per_sample
    assert pages_per_step % (MOBA_BLOCK // page_size) == 0

    def page_spec(r):
        return pl.BlockSpec((1, N_HEADS, HEAD_DIM, page_size),
                            lambda b, i, pt: (pt[(b * nt + i) * pages_per_step + r], 0, 0, 0))

    memb = pl.BlockSpec((n_mem, MEM_W), lambda b, i, *_: (b, 0))
    x1, sel = pl.pallas_call(
        functools.partial(_postmix_kernel, pages_per_step=pages_per_step, n_blocks=n_blocks),
        grid_spec=pltpu.PrefetchScalarGridSpec(
            num_scalar_prefetch=1,
            grid=(n, nt),
            in_specs=[tile(d), tile(CONV_CH), tile(ATTN_W), tile(MEM_W), memb, memb, _resident((1, d)),
                      _resident((d, 3 * d)), _resident((1, 3 * d)), _resident((CONV_CH, d)),
                      _resident((ATTN_W, d)), _resident((MEM_W, d)), _resident((d, d)),
                      pl.BlockSpec((1, N_HEADS, HEAD_DIM, page_size),
                                   lambda b, i, pt: ((b * nt + i) // steps_per_sample, 0, 0, 0))]
            + [page_spec(r) for r in range(pages_per_step)],
            out_specs=[tile(d), pl.BlockSpec((1, N_HEADS, LANES),
                                             lambda b, i, pt: ((b * nt + i) // steps_per_sample, 0, 0))],
            scratch_shapes=[pltpu.VMEM((N_HEADS, LANES), F32), pltpu.VMEM((TM, d), BF16)]),
        out_shape=[jax.ShapeDtypeStruct((n * t, d), F32), jax.ShapeDtypeStruct((nd, N_HEADS, LANES), jnp.int32)],
        compiler_params=_params(2),
        name="postmix",
    )(pt_flat, xp, cact, ob, mq, mkb, mvb, g1, w_gate, bg, wpa, wpb, wpc, wout, q_cols, *([ck] * pages_per_step))
    sel_flat = sel[:, :, :MOBA_TOPK].reshape(-1)

    chunk = dff // 2
    assert chunk % LANES == 0
    y_p, ftail = pl.pallas_call(
        functools.partial(_ffn_kernel, chunk=chunk),
        grid=(n, nt),
        in_specs=[tile(d), _resident((1, d)), _resident((d, 2 * dff)), _resident((FFN_CONV_WIDTH, 2 * dff)),
                  _resident((1, 2 * dff)), _resident((dff, d))],
        out_specs=[tile(d), pl.BlockSpec((1, SUBLANES, 2 * dff), lambda b, i: (b, 0, 0))],
        out_shape=[jax.ShapeDtypeStruct((n * t, d), F32), jax.ShapeDtypeStruct((n, SUBLANES, 2 * dff), F32)],
        scratch_shapes=[pltpu.VMEM((SUBLANES + TM, 2 * dff), F32)],
        compiler_params=_params(2),
        name="ffn",
    )(x1, g2, wup, w_dw_f, bdw_f, wdown)

    row_spec = pl.BlockSpec((1, 1, ATTN_W), lambda b, *_: (b, 0, 0))
    rows3 = lambda a: a.reshape(nd, 1, -1)

    halves = jnp.arange(PAGES_PER_BLOCK, dtype=jnp.int32)
    sel_pages = PAGES_PER_BLOCK * sel[:, :, :MOBA_TOPK, None] + halves
    page_ids = jnp.take_along_axis(page_table.astype(jnp.int32), sel_pages.reshape(nd, -1), axis=1).reshape(-1)
    tiles_per_sample = N_HEADS * MOBA_TOPK * PAGES_PER_BLOCK

    def kv_spec(h, tile):
        return pl.BlockSpec((1, 1, HEAD_DIM, page_size),
                            lambda b, pg, sl: (pg[b * tiles_per_sample + tile], h, 0, 0))

    kv_specs = [kv_spec(h, h * MOBA_TOPK * PAGES_PER_BLOCK + idx)
                for h in range(N_HEADS) for idx in range(MOBA_TOPK * PAGES_PER_BLOCK)]
    ob_s = pl.pallas_call(
        functools.partial(_decode_attn_kernel, past_len=past_len),
        grid_spec=pltpu.PrefetchScalarGridSpec(
            num_scalar_prefetch=2,
            grid=(nd,),
            in_specs=[pl.BlockSpec(memory_space=pltpu.SMEM), row_spec, row_spec, row_spec]
            + kv_specs + kv_specs,
            out_specs=row_spec),
        out_shape=jax.ShapeDtypeStruct((nd, 1, ATTN_W), F32),
        compiler_params=_params(1),
        name="decode_attn",
    )(page_ids, sel_flat, jnp.asarray(slopes, F32), rows3(q_s), rows3(k_s), rows3(v_s),
      *([ck] * len(kv_specs)), *([cv] * len(kv_specs)))
    ob_s = ob_s.reshape(nd, ATTN_W)

    group = SUBLANES
    assert nd % group == 0
    cm_spec = pl.BlockSpec((group, n_mem * MEM_HEADS, MEM_HEAD_DIM), lambda b: (b, 0, 0))
    mq_spec = pl.BlockSpec((group, 1, MEM_W), lambda b: (b, 0, 0))
    oc_s = pl.pallas_call(
        functools.partial(_memattn_s_kernel, group=group, n_mem=n_mem),
        grid=(nd // group,),
        in_specs=[mq_spec, cm_spec, cm_spec],
        out_specs=mq_spec,
        out_shape=jax.ShapeDtypeStruct((nd, 1, MEM_W), F32),
        compiler_params=_params(1),
        name="memattn_s",
    )(rows3(mq_s), cache_mem_k.reshape(nd, n_mem * MEM_HEADS, MEM_HEAD_DIM),
      cache_mem_v.reshape(nd, n_mem * MEM_HEADS, MEM_HEAD_DIM))
    oc_s = oc_s.reshape(nd, MEM_W)

    y_s, up_s = pl.pallas_call(
        _post_s_kernel,
        out_shape=[jax.ShapeDtypeStruct((nd, d), F32), jax.ShapeDtypeStruct((nd, 2 * dff), F32)],
        compiler_params=vm,
        name="post_s",
    )(xs, cact_s, ob_s, oc_s, gl_s, bg, wpa, wpb, wpc, wout, g2, wup, st_ffn, w_dw_f, bdw_f, wdown)

    heads = lambda a, b, s: a.reshape(b, s, N_HEADS, HEAD_DIM)
    from_t = lambda a: a.reshape(n, N_HEADS, HEAD_DIM, t).transpose(0, 3, 1, 2)
    conv_p = utail[:, CONV_HALO - (CONV_WIDTH - 1):, :]
    conv_s = jnp.concatenate([state_conv[:, 1:, :], u_s[:, None, :]], axis=1)
    ffn_p = ftail[:, SUBLANES - (FFN_CONV_WIDTH - 1):, :]
    ffn_s = jnp.concatenate([state_ffn_conv[:, 1:, :], up_s[:, None, :]], axis=1)
    return (y_p.reshape(n, t, d), y_s.reshape(nd, td, d),
            from_t(k_p), from_t(v_p), heads(k_s, nd, td), heads(v_s, nd, td),
            conv_p, conv_s, ffn_p, ffn_s,
            mk_p.reshape(n, n_mem, MEM_HEADS, MEM_HEAD_DIM), mv_p.reshape(n, n_mem, MEM_HEADS, MEM_HEAD_DIM))
```

```python
import functools

import numpy as np
import jax
import jax.numpy as jnp
from jax import lax
from jax.experimental import pallas as pl
from jax.experimental.pallas import tpu as pltpu

F32 = jnp.float32
BF16 = jnp.bfloat16

EPS = 1e-6
CONV_CH = 512
CONV_WIDTH = 31
N_HEADS = 8
HEAD_DIM = 64
ATTN_W = N_HEADS * HEAD_DIM
MOBA_BLOCK = 256
MOBA_TOPK = 3
MEM_HEADS = 4
MEM_HEAD_DIM = 128
MEM_W = MEM_HEADS * MEM_HEAD_DIM
FFN_CONV_WIDTH = 3
LANES = 128
SUBLANES = 8
TM = MOBA_BLOCK
CONV_HALO = 32
MASK_NEG = -float(2 ** 30)
GROUP = 16
V_ROWS = HEAD_DIM + 16
VMEM_LIMIT = 56 * 1024 * 1024

X_SEL = HEAD_DIM
X_RQ = HEAD_DIM + GROUP
X_ONE = X_RQ + 1
X_TQ = X_RQ + 2
X_ONE2 = X_RQ + 3


def _dot(a, b):
    return jnp.dot(a, b, preferred_element_type=F32)


def _dot_nt(a, b):
    return lax.dot_general(a, b, (((1,), (1,)), ((), ())), preferred_element_type=F32)


def _rms_rows(x, g):
    return x * lax.rsqrt(jnp.mean(x * x, axis=-1, keepdims=True) + EPS) * g


def _group_rms(z, bd, g):
    ms = _dot((z * z).astype(BF16), bd)
    return z * lax.rsqrt(ms + EPS) * g


def _exact_zero(v):
    bits = pltpu.bitcast(v, jnp.uint32)
    half = jnp.uint32(16)
    return pltpu.bitcast(lax.shift_right_logical(lax.shift_right_logical(bits, half), half), F32)


def _sigmoid(x):
    return 1.0 / (1.0 + jnp.exp(-x))


def _layernorm_silu(c, g, b):
    mu = jnp.mean(c, axis=-1, keepdims=True)
    xc = c - mu
    var = jnp.mean(xc * xc, axis=-1, keepdims=True)
    y = xc * lax.rsqrt(var + EPS) * g + b
    return y * _sigmoid(y)


def _premix_kernel(x_ref, g1_ref, w_ref, bd64_ref, bd128_ref, qg_ref, kg_ref, mqg_ref,
                   wdw_ref, bdw_ref, lng_ref, lnb_ref, slope_ref,
                   k_ref, v_ref, km_ref, qt_ref, ka_ref, vt_ref, mq_ref, cact_ref, utail_ref,
                   ubuf):
    t = pl.program_id(1)
    nt = pl.num_programs(1)
    c = CONV_CH

    @pl.when(t == 0)
    def _():
        ubuf[0:CONV_HALO, :] = jnp.zeros((CONV_HALO, c), F32)
        ubuf[CONV_HALO + TM:, :] = jnp.zeros((SUBLANES, c), F32)

    hn = _rms_rows(x_ref[...], g1_ref[...]).astype(BF16)

    a = _dot(hn, w_ref[:, 0:c])
    g = _dot(hn, w_ref[:, c:2 * c])
    u = a * _sigmoid(g)
    ubuf[CONV_HALO:CONV_HALO + TM, :] = u
    o = 2 * c
    zq = _dot(hn, w_ref[:, o:o + ATTN_W])
    zk = _dot(hn, w_ref[:, o + ATTN_W:o + 2 * ATTN_W])
    zv = _dot(hn, w_ref[:, o + 2 * ATTN_W:o + 3 * ATTN_W])
    zm = _dot(hn, w_ref[:, o + 3 * ATTN_W:o + 3 * ATTN_W + MEM_W])
    acc = jnp.broadcast_to(bdw_ref[...], (TM, c))
    base = CONV_HALO - (CONV_WIDTH - 1)
    span = TM + 2 * SUBLANES
    for b in range(SUBLANES):
        part = None
        for k in range(b, CONV_WIDTH, SUBLANES):
            term = wdw_ref[k:k + 1, :] * ubuf[k - b:k - b + span, :]
            part = term if part is None else part + term
        acc = acc + part[base + b:base + b + TM, :]
    ubuf[0:CONV_HALO, :] = ubuf[TM:TM + CONV_HALO, :]
    cact_ref[...] = _layernorm_silu(acc, lng_ref[...], lnb_ref[...]).astype(BF16)

    qn = _group_rms(zq, bd64_ref[...], qg_ref[...]) * (HEAD_DIM ** -0.5)
    kn = _group_rms(zk, bd64_ref[...], kg_ref[...])
    mq = _group_rms(zm, bd128_ref[...], mqg_ref[...])

    ones_rows = jnp.where(lax.broadcasted_iota(jnp.int32, (V_ROWS - HEAD_DIM, TM), 0) == 0, 1.0, 0.0).astype(BF16)
    for cb in range(ATTN_W // LANES):
        k_ref[LANES * cb:LANES * (cb + 1), :] = kn[:, LANES * cb:LANES * (cb + 1)].T
        vt = zv[:, LANES * cb:LANES * (cb + 1)].T
        v_ref[LANES * cb:LANES * (cb + 1), :] = vt
        for sub in range(2):
            r0 = V_ROWS * (2 * cb + sub)
            vt_ref[0, r0:r0 + HEAD_DIM, :] = vt[HEAD_DIM * sub:HEAD_DIM * (sub + 1), :].astype(BF16)
            vt_ref[0, r0 + HEAD_DIM:r0 + V_ROWS, :] = ones_rows
    mq_ref[...] = mq.astype(BF16)
    km_ref[0] = jnp.mean(kn, axis=0, keepdims=True)

    lane = lax.broadcasted_iota(jnp.int32, (TM, LANES), 1)
    rowf = lax.broadcasted_iota(jnp.int32, (TM, LANES), 0).astype(F32)
    tf = t.astype(F32)
    zero = jnp.zeros((TM, LANES), F32)
    eq = jnp.where(lane == X_RQ, rowf,
                   jnp.where(lane == X_ONE, 1.0,
                             jnp.where(lane == X_TQ, tf,
                                       jnp.where(lane == X_ONE2, 1.0, zero))))
    for hp in range(N_HEADS // 2):
        xq = qn[:, LANES * hp:LANES * (hp + 1)]
        xk = kn[:, LANES * hp:LANES * (hp + 1)]
        for sub in range(2):
            h = 2 * hp + sub
            if sub == 1:
                xq = pltpu.roll(xq, HEAD_DIM, 1)
                xk = pltpu.roll(xk, HEAD_DIM, 1)
            sl = slope_ref[:, LANES * h:LANES * (h + 1)]
            ek = jnp.where(lane == X_SEL + t, MASK_NEG, zero)
            ek = jnp.where(lane == X_RQ, -sl,
                           jnp.where(lane == X_ONE, sl * rowf,
                                     jnp.where(lane == X_TQ, -sl * MOBA_BLOCK,
                                               jnp.where(lane == X_ONE2, sl * (MOBA_BLOCK * tf), ek))))
            qt_ref[LANES * h:LANES * (h + 1), :] = jnp.where(lane < HEAD_DIM, xq, eq).T.astype(BF16)
            ka_ref[0, :, LANES * h:LANES * (h + 1)] = jnp.where(lane < HEAD_DIM, xk, ek).astype(BF16)

    @pl.when(t == nt - 1)
    def _():
        utail_ref[0] = ubuf[0:CONV_HALO, :]


def _moba_kernel(qt_ref, ka_ref, vt_ref, kmt_ref, o_ref, qh_scr, m_scr, acc_scr, *stage):
    i = pl.program_id(1)
    s_scr, mx_scr = stage[:N_HEADS], stage[N_HEADS:]
    key = lax.broadcasted_iota(jnp.int32, (TM, TM), 0)
    qry = lax.broadcasted_iota(jnp.int32, (TM, TM), 1)
    causal = key <= qry

    def scores(j, h, diagonal):
        q = qt_ref[LANES * h:LANES * (h + 1), :] if diagonal else qh_scr[h]
        s = _dot(ka_ref[j, :, LANES * h:LANES * (h + 1)], q)
        if diagonal:
            s = jnp.where(causal, s, -jnp.inf)
        s_scr[h][...] = s
        mx_scr[h][...] = jnp.max(s, axis=0, keepdims=True)

    for h in range(N_HEADS):
        scores(i, h, True)

    grow = lax.broadcasted_iota(jnp.int32, (N_HEADS * GROUP, TM), 0)
    blk = grow & (GROUP - 1)
    past = blk < i

    gate = _dot(kmt_ref[0], qt_ref[...])
    g = jnp.where(past, gate, -jnp.inf)
    rank = jnp.zeros(g.shape, F32)
    for s in range(1, GROUP):
        up = pltpu.roll(g, N_HEADS * GROUP - s, 0)
        down = pltpu.roll(g, GROUP - s, 0)
        wrapped = blk >= GROUP - s
        partner = jnp.where(wrapped, down, up)
        beats = jnp.where(wrapped, jnp.where(partner >= g, 1.0, 0.0), jnp.where(partner > g, 1.0, 0.0))
        rank = rank + beats
    nsel = jnp.where(past, jnp.where(rank >= MOBA_TOPK, 1.0, 0.0), 0.0).astype(BF16)
    for h in range(N_HEADS):
        r0 = LANES * h
        qh_scr[h, 0:X_SEL, :] = qt_ref[r0:r0 + X_SEL, :]
        qh_scr[h, X_SEL:X_RQ, :] = nsel[GROUP * h:GROUP * (h + 1), :]
        qh_scr[h, X_RQ:LANES, :] = qt_ref[r0 + X_RQ:r0 + LANES, :]

    def accumulate(j, h):
        m_old = m_scr[h]
        m_new = jnp.maximum(m_old, mx_scr[h][...])
        alpha = jnp.exp(m_old - m_new)
        p = jnp.exp(s_scr[h][...] - m_new).astype(BF16)
        acc_scr[h] = alpha * acc_scr[h] + _dot(vt_ref[j, V_ROWS * h:V_ROWS * (h + 1), :], p)
        m_scr[h] = m_new

    m_scr[...] = jnp.full(m_scr.shape, -jnp.inf, F32)
    acc_scr[...] = jnp.zeros(acc_scr.shape, F32)

    def body(j, carry):
        prev = jnp.where(j == 0, i, j - 1)
        for h in range(N_HEADS):
            accumulate(prev, h)
            scores(j, h, False)
        return carry

    lax.fori_loop(0, i, body, 0)
    last = jnp.where(i == 0, i, i - 1)
    for h in range(N_HEADS):
        accumulate(last, h)
    outs = []
    for h in range(N_HEADS):
        acc = acc_scr[h]
        outs.append(acc[0:HEAD_DIM, :] / acc[HEAD_DIM:HEAD_DIM + 1, :])
    o_ref[...] = jnp.concatenate(outs, axis=0).T.astype(BF16)


def _memkv_kernel(mem_ref, g_ref, w_ref, bd128_ref, mkg_ref, mk_ref, mv_ref, mkb_ref, mvb_ref):
    hn = _rms_rows(mem_ref[...], g_ref[...]).astype(BF16)
    mk = _group_rms(_dot(hn, w_ref[:, 0:MEM_W]), bd128_ref[...], mkg_ref[...])
    mv = _dot(hn, w_ref[:, MEM_W:2 * MEM_W])
    mk_ref[...] = mk
    mv_ref[...] = mv
    mkb_ref[...] = mk.astype(BF16)
    mvb_ref[...] = mv.astype(BF16)


def _mem_attend_rows(mq, mk, mv):
    outs = []
    for hh in range(MEM_HEADS):
        sl = slice(MEM_HEAD_DIM * hh, MEM_HEAD_DIM * (hh + 1))
        s = _dot_nt(mq[:, sl], mk[:, sl]) * (MEM_HEAD_DIM ** -0.5)
        m = jnp.max(s, axis=-1, keepdims=True)
        p = jnp.exp(s - m)
        l = jnp.sum(p, axis=-1, keepdims=True)
        outs.append(_dot(p.astype(BF16), mv[:, sl]) / l)
    return jnp.concatenate(outs, axis=-1)


def _merge_out(x, gl, ya, yb, yc, wout):
    d = x.shape[-1]
    merged = (_sigmoid(gl[:, 0:d]) * ya + _sigmoid(gl[:, d:2 * d]) * yb + _sigmoid(gl[:, 2 * d:3 * d]) * yc)
    return x + _dot(merged.astype(BF16), wout)


def _postmix_kernel(pt_ref, x_ref, cact_ref, ob_ref, mq_ref, mk_ref, mv_ref, g1_ref, wg_ref, bg_ref,
                    wpa_ref, wpb_ref, wpc_ref, wout_ref, q_ref, ck_hbm, x1_ref, sel_ref,
                    gate_scr, merged_scr, page_buf, page_sem, *, pages_per_step, n_blocks):
    step = pl.program_id(0) * pl.num_programs(1) + pl.program_id(1)
    n_steps = pl.num_programs(0) * pl.num_programs(1)
    slot = step % 2

    def page_copy(page, to_slot, r):
        return pltpu.make_async_copy(ck_hbm.at[page], page_buf.at[to_slot, r], page_sem.at[to_slot])

    def fetch(for_step, to_slot):
        for r in range(pages_per_step):
            page_copy(pt_ref[for_step * pages_per_step + r], to_slot, r).start()

    @pl.when(step == 0)
    def _():
        gate_scr[...] = jnp.zeros(gate_scr.shape, F32)
        fetch(0, 0)

    @pl.when(step + 1 < n_steps)
    def _():
        fetch(step + 1, 1 - slot)

    for r in range(pages_per_step):
        page_copy(0, slot, r).wait()
    page_refs = [page_buf.at[slot, r] for r in range(pages_per_step)]

    blocks_per_step = pages_per_step // PAGES_PER_BLOCK
    steps_per_sample = n_blocks // blocks_per_step
    share = step % steps_per_sample
    n_chunks = 4
    chunk = pages_per_step // n_chunks
    assert chunk % PAGES_PER_BLOCK == 0 and chunk * n_chunks == pages_per_step

    def gate_chunk(g, c):
        first = share * blocks_per_step + c * (chunk // PAGES_PER_BLOCK)
        return _gate_scores(g, first, q_ref, page_refs[c * chunk:(c + 1) * chunk])

    g = jnp.where(share == 0, 0.0, gate_scr[...])
    x = x_ref[...]
    d = x.shape[-1]
    hn = _rms_rows(x, g1_ref[...]).astype(BF16)
    oc = _mem_attend_rows(mq_ref[...], mk_ref[...], mv_ref[...]).astype(BF16)
    width = d // n_chunks
    for c in range(n_chunks):
        g = gate_chunk(g, c)
        tie = jnp.concatenate([_exact_zero(g)[0:1, :]] * (width // LANES), axis=1)
        merged = None
        for br, (src, w_ref) in enumerate(((cact_ref[...], wpa_ref), (ob_ref[...], wpb_ref), (oc, wpc_ref))):
            cols = slice(d * br + width * c, d * br + width * (c + 1))
            gl = _dot(hn, wg_ref[:, cols]) + (bg_ref[:, cols] + tie)
            term = _sigmoid(gl) * _dot(src, w_ref[:, width * c:width * (c + 1)])
            merged = term if merged is None else merged + term
        merged_scr[:, width * c:width * (c + 1)] = merged.astype(BF16)
    x1_ref[...] = x + _dot(merged_scr[...], wout_ref[...])
    gate_scr[...] = g
    _gate_select(share == steps_per_sample - 1, gate_scr, sel_ref, n_blocks)


def _ffn_kernel(x1_ref, g2_ref, wup_ref, wdw_ref, bdw_ref, wdown_ref, y_ref, tail_ref, upbuf, *, chunk):
    t = pl.program_id(1)
    nt = pl.num_programs(1)
    dff = wdown_ref.shape[0]

    @pl.when(t == 0)
    def _():
        upbuf[0:SUBLANES, :] = jnp.zeros((SUBLANES, 2 * dff), F32)

    x1 = x1_ref[...]
    hn = _rms_rows(x1, g2_ref[...]).astype(BF16)

    for c in range(0, 2 * dff, chunk):
        upbuf[SUBLANES:SUBLANES + TM, c:c + chunk] = _dot(hn, wup_ref[:, c:c + chunk])

    def conv(c):
        out = bdw_ref[:, c:c + chunk]
        for k in range(FFN_CONV_WIDTH):
            r0 = SUBLANES - (FFN_CONV_WIDTH - 1) + k
            out = out + wdw_ref[k:k + 1, c:c + chunk] * upbuf[r0:r0 + TM, c:c + chunk]
        return out

    y = x1
    for c in range(0, dff, chunk):
        a = conv(c)
        b = conv(dff + c)
        act = (a * _sigmoid(a) * b).astype(BF16)
        y = y + _dot(act, wdown_ref[c:c + chunk, :])
    y_ref[...] = y

    @pl.when(t == nt - 1)
    def _():
        tail_ref[0] = upbuf[TM:TM + SUBLANES, :]

    upbuf[0:SUBLANES, :] = upbuf[TM:TM + SUBLANES, :]


def _premix_s_kernel(x_ref, g1_ref, w_ref, bd64_ref, bd128_ref, qg_ref, kg_ref, mqg_ref,
                     st_ref, wdw_ref, bdw_ref, lng_ref, lnb_ref,
                     u_ref, q_ref, k_ref, v_ref, mq_ref, gl_ref, cact_ref):
    c = CONV_CH
    hn = _rms_rows(x_ref[...], g1_ref[...]).astype(BF16)
    a = _dot(hn, w_ref[:, 0:c])
    g = _dot(hn, w_ref[:, c:2 * c])
    u = a * _sigmoid(g)
    u_ref[...] = u
    acc = bdw_ref[...] + wdw_ref[CONV_WIDTH - 1:CONV_WIDTH, :] * u
    for k in range(CONV_WIDTH - 1):
        acc = acc + wdw_ref[k:k + 1, :] * st_ref[k]
    cact_ref[...] = _layernorm_silu(acc, lng_ref[...], lnb_ref[...]).astype(BF16)

    o = 2 * c
    q_ref[...] = _group_rms(_dot(hn, w_ref[:, o:o + ATTN_W]), bd64_ref[...], qg_ref[...])
    o += ATTN_W
    k_ref[...] = _group_rms(_dot(hn, w_ref[:, o:o + ATTN_W]), bd64_ref[...], kg_ref[...])
    o += ATTN_W
    v_ref[...] = _dot(hn, w_ref[:, o:o + ATTN_W])
    o += ATTN_W
    mq_ref[...] = _group_rms(_dot(hn, w_ref[:, o:o + MEM_W]), bd128_ref[...], mqg_ref[...])
    o += MEM_W
    gl_ref[...] = _dot(hn, w_ref[:, o:])


PAGES_PER_BLOCK = MOBA_BLOCK // LANES


def _gate_scores(g, first_block, q_ref, page_refs):
    lane = lax.broadcasted_iota(jnp.int32, (N_HEADS, LANES), 1)
    head = lax.broadcasted_iota(jnp.int32, (N_HEADS, LANES), 0)
    for r in range(0, len(page_refs), PAGES_PER_BLOCK):
        bidx = first_block + r // PAGES_PER_BLOCK
        for h in range(N_HEADS):
            qc = q_ref[0, h][:, 0:1].astype(BF16).astype(F32)
            ksum = page_refs[r][h]
            for extra in range(1, PAGES_PER_BLOCK):
                ksum = ksum + page_refs[r + extra][h]
            kmean = jnp.sum(ksum, axis=1, keepdims=True) * (1.0 / MOBA_BLOCK)
            tot = jnp.sum(qc * kmean.astype(BF16).astype(F32), axis=0, keepdims=True)
            g = jnp.where(lane == bidx, jnp.where(head == h, tot, g), g)
    return g


def _gate_select(last, gate_scr, sel_ref, n_blocks):
    @pl.when(last)
    def _():
        g = gate_scr[:, 0:n_blocks]
        bl = lax.broadcasted_iota(jnp.int32, (N_HEADS, n_blocks), 1)
        rank = jnp.zeros((N_HEADS, n_blocks), F32)
        for b in range(n_blocks):
            other = g[:, b:b + 1]
            rank = rank + jnp.where(bl > b, jnp.where(other >= g, 1.0, 0.0), jnp.where(other > g, 1.0, 0.0))
        lane_o = lax.broadcasted_iota(jnp.int32, (N_HEADS, LANES), 1)
        out = jnp.zeros((N_HEADS, LANES), F32)
        blf = bl.astype(F32)
        for r in range(MOBA_TOPK):
            idx = jnp.sum(jnp.where(rank == float(r), blf, 0.0), axis=-1, keepdims=True)
            out = jnp.where(lane_o == r, idx, out)
        sel_ref[0] = out.astype(jnp.int32)


def _decode_attn_kernel(page_ref, sel_ref, slope_ref, q_ref, kown_ref, vown_ref, ck_hbm, cv_hbm, o_ref,
                        k_buf, v_buf, sem, *, past_len):
    pages_per_block = MOBA_BLOCK // LANES
    tiles_per_head = MOBA_TOPK * pages_per_block
    n_tiles = N_HEADS * tiles_per_head
    n = pl.program_id(0)
    slot = n % 2

    def tile_copies(page, to_slot, tile):
        h = tile // tiles_per_head
        return (pltpu.make_async_copy(ck_hbm.at[page, h], k_buf.at[to_slot, tile], sem.at[0, to_slot]),
                pltpu.make_async_copy(cv_hbm.at[page, h], v_buf.at[to_slot, tile], sem.at[1, to_slot]))

    def fetch(sample, to_slot):
        for tile in range(n_tiles):
            for copy in tile_copies(page_ref[sample * n_tiles + tile], to_slot, tile):
                copy.start()

    @pl.when(n == 0)
    def _():
        fetch(0, 0)

    @pl.when(n + 1 < pl.num_programs(0))
    def _():
        fetch(n + 1, 1 - slot)

    for tile in range(n_tiles):
        for copy in tile_copies(0, slot, tile):
            copy.wait()
    k_refs = [k_buf.at[slot, tile] for tile in range(n_tiles)]
    v_refs = [v_buf.at[slot, tile] for tile in range(n_tiles)]

    lane = lax.broadcasted_iota(jnp.int32, (1, LANES), 1)
    lanef = lane.astype(F32)
    zeros = jnp.zeros((HEAD_DIM, tiles_per_head * LANES), BF16)
    scale = HEAD_DIM ** -0.5

    def head_tiles(tile_refs, h):
        t = jnp.concatenate([tile_refs[h * tiles_per_head + idx][...] for idx in range(tiles_per_head)], axis=1)
        t = t.astype(BF16)
        return jnp.concatenate([t, zeros] if h % 2 == 0 else [zeros, t], axis=0)

    def own_half(h):
        return (lane < HEAD_DIM) if h % 2 == 0 else (lane >= HEAD_DIM)

    def pair_lanes(ref, h):
        return ref[0][:, LANES * (h // 2):LANES * (h // 2 + 1)]

    qz, raw = [], []
    for h in range(N_HEADS):
        qz.append(jnp.where(own_half(h), pair_lanes(q_ref, h) * scale, 0.0).astype(BF16))
        raw.append(_dot(jnp.broadcast_to(qz[h], (SUBLANES, LANES)), head_tiles(k_refs, h))[0:1, :])
    probs = []
    for h in range(N_HEADS):
        dist = []
        for r in range(MOBA_TOPK):
            blk = sel_ref[(n * N_HEADS + h) * MOBA_TOPK + r]
            for half in range(pages_per_block):
                pos0 = (blk * MOBA_BLOCK + half * LANES).astype(F32)
                dist.append(float(past_len) - (pos0 + lanef))
        logits = raw[h] - slope_ref[h] * jnp.concatenate(dist, axis=1)
        kown = pair_lanes(kown_ref, h).astype(BF16).astype(F32)
        s_own = jnp.sum(qz[h].astype(F32) * kown, axis=-1, keepdims=True)
        m = jnp.maximum(s_own, jnp.max(logits, axis=-1, keepdims=True))
        p_own = jnp.exp(s_own - m)
        p = jnp.exp(logits - m)
        probs.append((p, p_own, p_own + jnp.sum(p, axis=-1, keepdims=True)))
    outs = []
    for h in range(N_HEADS):
        p, p_own, l = probs[h]
        p8 = jnp.broadcast_to(p.astype(BF16), (SUBLANES, p.shape[1]))
        vown = jnp.where(own_half(h), pair_lanes(vown_ref, h).astype(BF16).astype(F32), 0.0)
        acc = p_own.astype(BF16).astype(F32) * vown + _dot_nt(p8, head_tiles(v_refs, h))[0:1, :]
        outs.append(acc / l)
    o_ref[0] = jnp.concatenate([outs[h] + outs[h + 1] for h in range(0, N_HEADS, 2)], axis=1)


def _memattn_s_kernel(mq_ref, mk_ref, mv_ref, o_ref, *, group, n_mem):
    for s in range(group):
        mq = jnp.broadcast_to(mq_ref[s], (SUBLANES, MEM_W)).astype(BF16)
        outs = []
        for hh in range(MEM_HEADS):
            mk = mk_ref[s, pl.ds(hh, n_mem, stride=MEM_HEADS), :].astype(BF16)
            mv = mv_ref[s, pl.ds(hh, n_mem, stride=MEM_HEADS), :].astype(BF16)
            sc = _dot_nt(mq[:, MEM_HEAD_DIM * hh:MEM_HEAD_DIM * (hh + 1)], mk) * (MEM_HEAD_DIM ** -0.5)
            m = jnp.max(sc, axis=-1, keepdims=True)
            p = jnp.exp(sc - m)
            l = jnp.sum(p, axis=-1, keepdims=True)
            outs.append(_dot(p.astype(BF16), mv) / l)
        o_ref[s] = jnp.concatenate(outs, axis=-1)[0:1, :]


def _post_s_kernel(x_ref, cact_ref, ob_ref, oc_ref, gl_ref, bg_ref, wpa_ref, wpb_ref, wpc_ref, wout_ref,
                   g2_ref, wup_ref, st_ref, wdw_ref, bdw_ref, wdown_ref, y_ref, up_ref):
    x = x_ref[...]
    dff = wdown_ref.shape[0]
    ya = _dot(cact_ref[...], wpa_ref[...])
    yb = _dot(ob_ref[...].astype(BF16), wpb_ref[...])
    yc = _dot(oc_ref[...].astype(BF16), wpc_ref[...])
    x1 = _merge_out(x, gl_ref[...] + bg_ref[...], ya, yb, yc, wout_ref[...])
    up = _dot(_rms_rows(x1, g2_ref[...]).astype(BF16), wup_ref[...])
    up_ref[...] = up
    cv = bdw_ref[...] + wdw_ref[FFN_CONV_WIDTH - 1:FFN_CONV_WIDTH, :] * up
    for k in range(FFN_CONV_WIDTH - 1):
        cv = cv + wdw_ref[k:k + 1, :] * st_ref[k]
    a = cv[:, 0:dff]
    b = cv[:, dff:]
    act = (a * _sigmoid(a) * b).astype(BF16)
    y_ref[...] = x1 + _dot(act, wdown_ref[...])


def _resident(shape):
    nd = len(shape)
    return pl.BlockSpec(shape, lambda *_: (0,) * nd, pipeline_mode=pl.Buffered(1))


def _params(n_axes):
    return pltpu.CompilerParams(dimension_semantics=("arbitrary",) * n_axes, vmem_limit_bytes=VMEM_LIMIT)


def kernel(x_prompt, x_sample, mem_prompt, cache_k, cache_v, page_table, state_conv, state_ffn_conv, cache_mem_k, cache_mem_v, norm1_g, w_in, b_gate, w_dw_a, b_dw_a, ln_a_g, ln_a_b, w_proj_a, q_norm_g, k_norm_g, w_proj_b, mem_norm_g, w_mem_kv, mq_norm_g, mk_norm_g, w_proj_c, w_out, norm2_g, w_up, w_dw_f, b_dw_f, w_down):
    n, t, d = x_prompt.shape
    nd, td, _ = x_sample.shape
    n_mem = mem_prompt.shape[1]
    n_pool, page_size = cache_k.shape[:2]
    n_pages = page_table.shape[1]
    past_len = n_pages * page_size
    dff = w_down.shape[0]
    nt = t // TM
    n_mix = 2 * CONV_CH + 3 * ATTN_W + MEM_W
    assert t % TM == 0 and nt <= GROUP and td == 1
    assert past_len % MOBA_BLOCK == 0 and MOBA_BLOCK == 2 * page_size and page_size == LANES
    n_blocks = past_len // MOBA_BLOCK

    row = lambda v: v.reshape(1, -1).astype(F32)
    w_in_b = w_in.astype(BF16)
    w_mix, w_gate = w_in_b[:, :n_mix], w_in_b[:, n_mix:]
    wpa, wpb, wpc = w_proj_a.astype(BF16), w_proj_b.astype(BF16), w_proj_c.astype(BF16)
    wout, wup, wdown = w_out.astype(BF16), w_up.astype(BF16), w_down.astype(BF16)
    g1, g2 = row(norm1_g), row(norm2_g)
    qg = row(jnp.tile(q_norm_g, N_HEADS))
    kg = row(jnp.tile(k_norm_g, N_HEADS))
    mqg = row(jnp.tile(mq_norm_g, MEM_HEADS))
    mkg = row(jnp.tile(mk_norm_g, MEM_HEADS))
    bdw_a, lng, lnb, bg, bdw_f = row(b_dw_a), row(ln_a_g), row(ln_a_b), row(b_gate), row(b_dw_f)
    grp = np.arange(ATTN_W)
    bd64 = jnp.asarray((grp[:, None] // HEAD_DIM == grp[None, :] // HEAD_DIM) / HEAD_DIM, BF16)
    bd128 = jnp.asarray((grp[:, None] // MEM_HEAD_DIM == grp[None, :] // MEM_HEAD_DIM) / MEM_HEAD_DIM, BF16)
    slopes = 2.0 ** (-8.0 * np.arange(1, N_HEADS + 1) / N_HEADS)
    slope_lanes = jnp.asarray(np.repeat(slopes, LANES)[None, :], F32)

    xp = x_prompt.reshape(n * t, d)
    tile = lambda w: pl.BlockSpec((TM, w), lambda b, i, *_: (b * nt + i, 0))
    aw = N_HEADS * LANES
    vw = N_HEADS * V_ROWS
    blk3 = lambda r, c: pl.BlockSpec((1, r, c), lambda b, i: (b * nt + i, 0, 0))
    k_p, v_p, km, qt, ka, vt, mq, cact, utail = pl.pallas_call(
        _premix_kernel,
        grid=(n, nt),
        in_specs=[tile(d), _resident((1, d)), _resident((d, n_mix)), _resident((ATTN_W, ATTN_W)),
                  _resident((MEM_W, MEM_W)), _resident((1, ATTN_W)), _resident((1, ATTN_W)),
                  _resident((1, MEM_W)), _resident((CONV_WIDTH, CONV_CH)), _resident((1, CONV_CH)),
                  _resident((1, CONV_CH)), _resident((1, CONV_CH)), _resident((1, aw))],
        out_specs=[pl.BlockSpec((ATTN_W, TM), lambda b, i: (b, i)), pl.BlockSpec((ATTN_W, TM), lambda b, i: (b, i)),
                   pl.BlockSpec((1, 1, ATTN_W), lambda b, i: (b * nt + i, 0, 0)),
                   pl.BlockSpec((aw, TM), lambda b, i: (b, i)), blk3(TM, aw), blk3(vw, TM),
                   tile(MEM_W), tile(CONV_CH),
                   pl.BlockSpec((1, CONV_HALO, CONV_CH), lambda b, i: (b, 0, 0))],
        out_shape=[jax.ShapeDtypeStruct((n * ATTN_W, t), F32), jax.ShapeDtypeStruct((n * ATTN_W, t), F32),
                   jax.ShapeDtypeStruct((n * nt, 1, ATTN_W), F32),
                   jax.ShapeDtypeStruct((n * aw, t), BF16), jax.ShapeDtypeStruct((n * nt, TM, aw), BF16),
                   jax.ShapeDtypeStruct((n * nt, vw, TM), BF16), jax.ShapeDtypeStruct((n * t, MEM_W), BF16),
                   jax.ShapeDtypeStruct((n * t, CONV_CH), BF16),
                   jax.ShapeDtypeStruct((n, CONV_HALO, CONV_CH), F32)],
        scratch_shapes=[pltpu.VMEM((CONV_HALO + TM + SUBLANES, CONV_CH), F32)],
        compiler_params=_params(2),
        name="premix",
    )(xp, g1, w_mix, bd64, bd128, qg, kg, mqg, w_dw_a, bdw_a, lng, lnb, slope_lanes)

    km4 = km.reshape(n, nt, N_HEADS, HEAD_DIM).transpose(0, 2, 1, 3)
    km4 = jnp.pad(km4, ((0, 0), (0, 0), (0, GROUP - nt), (0, LANES - HEAD_DIM)))
    kmt = (km4[:, :, :, None, :] * jnp.eye(N_HEADS, dtype=F32)[None, :, None, :, None])
    kmt = kmt.reshape(n, N_HEADS * GROUP, aw).astype(BF16)

    seq3 = lambda r, c: pl.BlockSpec((nt, r, c), lambda b, i: (b, 0, 0))
    ob = pl.pallas_call(
        _moba_kernel,
        grid=(n, nt),
        in_specs=[pl.BlockSpec((aw, TM), lambda b, i: (b, i)), seq3(TM, aw), seq3(vw, TM),
                  pl.BlockSpec((1, N_HEADS * GROUP, aw), lambda b, i: (b, 0, 0))],
        out_specs=tile(ATTN_W),
        out_shape=jax.ShapeDtypeStruct((n * t, ATTN_W), BF16),
        scratch_shapes=[pltpu.VMEM((N_HEADS, LANES, TM), BF16), pltpu.VMEM((N_HEADS, 1, TM), F32),
                        pltpu.VMEM((N_HEADS, V_ROWS, TM), F32)]
        + [pltpu.VMEM((TM, TM), F32)] * N_HEADS + [pltpu.VMEM((1, TM), F32)] * N_HEADS,
        compiler_params=_params(2),
        name="moba",
    )(qt, ka, vt, kmt)

    mem = mem_prompt.reshape(n * n_mem, d)
    mtile = lambda w: pl.BlockSpec((n_mem, w), lambda b: (b, 0))
    mk_p, mv_p, mkb, mvb = pl.pallas_call(
        _memkv_kernel,
        grid=(n,),
        in_specs=[mtile(d), _resident((1, d)), _resident((d, 2 * MEM_W)), _resident((MEM_W, MEM_W)),
                  _resident((1, MEM_W))],
        out_specs=[mtile(MEM_W)] * 4,
        out_shape=[jax.ShapeDtypeStruct((n * n_mem, MEM_W), F32)] * 2
        + [jax.ShapeDtypeStruct((n * n_mem, MEM_W), BF16)] * 2,
        compiler_params=_params(1),
        name="memkv",
    )(mem, row(mem_norm_g), w_mem_kv.astype(BF16), bd128, mkg)

    xs = x_sample.reshape(nd, d)
    st_conv = state_conv.transpose(1, 0, 2)
    st_ffn = state_ffn_conv.transpose(1, 0, 2)
    vm = pltpu.CompilerParams(vmem_limit_bytes=VMEM_LIMIT)
    u_s, q_s, k_s, v_s, mq_s, gl_s, cact_s = pl.pallas_call(
        _premix_s_kernel,
        out_shape=[jax.ShapeDtypeStruct((nd, CONV_CH), F32), jax.ShapeDtypeStruct((nd, ATTN_W), F32),
                   jax.ShapeDtypeStruct((nd, ATTN_W), F32), jax.ShapeDtypeStruct((nd, ATTN_W), F32),
                   jax.ShapeDtypeStruct((nd, MEM_W), F32), jax.ShapeDtypeStruct((nd, 3 * d), F32),
                   jax.ShapeDtypeStruct((nd, CONV_CH), BF16)],
        compiler_params=vm,
        name="premix_s",
    )(xs, g1, w_in_b, bd64, bd128, qg, kg, mqg, st_conv, w_dw_a, bdw_a, lng, lnb)

    ck = cache_k.transpose(0, 2, 3, 1)
    cv = cache_v.transpose(0, 2, 3, 1)
    pt_flat = page_table.reshape(-1).astype(jnp.int32)
    q_cols = jnp.broadcast_to(q_s.reshape(nd, N_HEADS, HEAD_DIM, 1), (nd, N_HEADS, HEAD_DIM, page_size))
    assert (n * nt) % nd == 0 and n_pages % ((n * nt) // nd) == 0 and n_blocks <= LANES
    steps_per_sample = (n * nt) // nd
    pages_per_step = n_pages // steps_per_sample
    assert pages_per_step % (MOBA_BLOCK // page_size) == 0

    memb = pl.BlockSpec((n_mem, MEM_W), lambda b, i, *_: (b, 0))
    x1, sel = pl.pallas_call(
        functools.partial(_postmix_kernel, pages_per_step=pages_per_step, n_blocks=n_blocks),
        grid_spec=pltpu.PrefetchScalarGridSpec(
            num_scalar_prefetch=1,
            grid=(n, nt),
            in_specs=[tile(d), tile(CONV_CH), tile(ATTN_W), tile(MEM_W), memb, memb, _resident((1, d)),
                      _resident((d, 3 * d)), _resident((1, 3 * d)), _resident((CONV_CH, d)),
                      _resident((ATTN_W, d)), _resident((MEM_W, d)), _resident((d, d)),
                      pl.BlockSpec((1, N_HEADS, HEAD_DIM, page_size),
                                   lambda b, i, pt: ((b * nt + i) // steps_per_sample, 0, 0, 0)),
                      pl.BlockSpec(memory_space=pl.ANY)],
            out_specs=[tile(d), pl.BlockSpec((1, N_HEADS, LANES),
                                             lambda b, i, pt: ((b * nt + i) // steps_per_sample, 0, 0))],
            scratch_shapes=[pltpu.VMEM((N_HEADS, LANES), F32), pltpu.VMEM((TM, d), BF16),
                            pltpu.VMEM((2, pages_per_step, N_HEADS, HEAD_DIM, page_size), F32),
                            pltpu.SemaphoreType.DMA((2,))]),
        out_shape=[jax.ShapeDtypeStruct((n * t, d), F32), jax.ShapeDtypeStruct((nd, N_HEADS, LANES), jnp.int32)],
        compiler_params=_params(2),
        name="postmix",
    )(pt_flat, xp, cact, ob, mq, mkb, mvb, g1, w_gate, bg, wpa, wpb, wpc, wout, q_cols, ck)
    sel_flat = sel[:, :, :MOBA_TOPK].reshape(-1)

    chunk = dff // 2
    assert chunk % LANES == 0
    y_p, ftail = pl.pallas_call(
        functools.partial(_ffn_kernel, chunk=chunk),
        grid=(n, nt),
        in_specs=[tile(d), _resident((1, d)), _resident((d, 2 * dff)), _resident((FFN_CONV_WIDTH, 2 * dff)),
                  _resident((1, 2 * dff)), _resident((dff, d))],
        out_specs=[tile(d), pl.BlockSpec((1, SUBLANES, 2 * dff), lambda b, i: (b, 0, 0))],
        out_shape=[jax.ShapeDtypeStruct((n * t, d), F32), jax.ShapeDtypeStruct((n, SUBLANES, 2 * dff), F32)],
        scratch_shapes=[pltpu.VMEM((SUBLANES + TM, 2 * dff), F32)],
        compiler_params=_params(2),
        name="ffn",
    )(x1, g2, wup, w_dw_f, bdw_f, wdown)

    row_spec = pl.BlockSpec((1, 1, ATTN_W), lambda b, *_: (b, 0, 0))
    rows3 = lambda a: a.reshape(nd, 1, -1)

    halves = jnp.arange(PAGES_PER_BLOCK, dtype=jnp.int32)
    sel_pages = PAGES_PER_BLOCK * sel[:, :, :MOBA_TOPK, None] + halves
    page_ids = jnp.take_along_axis(page_table.astype(jnp.int32), sel_pages.reshape(nd, -1), axis=1).reshape(-1)
    tiles_per_sample = N_HEADS * MOBA_TOPK * PAGES_PER_BLOCK
    tile_buf = pltpu.VMEM((2, tiles_per_sample, HEAD_DIM, page_size), F32)
    ob_s = pl.pallas_call(
        functools.partial(_decode_attn_kernel, past_len=past_len),
        grid_spec=pltpu.PrefetchScalarGridSpec(
            num_scalar_prefetch=2,
            grid=(nd,),
            in_specs=[pl.BlockSpec(memory_space=pltpu.SMEM), row_spec, row_spec, row_spec,
                      pl.BlockSpec(memory_space=pl.ANY), pl.BlockSpec(memory_space=pl.ANY)],
            out_specs=row_spec,
            scratch_shapes=[tile_buf, tile_buf, pltpu.SemaphoreType.DMA((2, 2))]),
        out_shape=jax.ShapeDtypeStruct((nd, 1, ATTN_W), F32),
        compiler_params=_params(1),
        name="decode_attn",
    )(page_ids, sel_flat, jnp.asarray(slopes, F32), rows3(q_s), rows3(k_s), rows3(v_s), ck, cv)
    ob_s = ob_s.reshape(nd, ATTN_W)

    group = SUBLANES
    assert nd % group == 0
    cm_spec = pl.BlockSpec((group, n_mem * MEM_HEADS, MEM_HEAD_DIM), lambda b: (b, 0, 0))
    mq_spec = pl.BlockSpec((group, 1, MEM_W), lambda b: (b, 0, 0))
    oc_s = pl.pallas_call(
        functools.partial(_memattn_s_kernel, group=group, n_mem=n_mem),
        grid=(nd // group,),
        in_specs=[mq_spec, cm_spec, cm_spec],
        out_specs=mq_spec,
        out_shape=jax.ShapeDtypeStruct((nd, 1, MEM_W), F32),
        compiler_params=_params(1),
        name="memattn_s",
    )(rows3(mq_s), cache_mem_k.reshape(nd, n_mem * MEM_HEADS, MEM_HEAD_DIM),
      cache_mem_v.reshape(nd, n_mem * MEM_HEADS, MEM_HEAD_DIM))
    oc_s = oc_s.reshape(nd, MEM_W)

    y_s, up_s = pl.pallas_call(
        _post_s_kernel,
        out_shape=[jax.ShapeDtypeStruct((nd, d), F32), jax.ShapeDtypeStruct((nd, 2 * dff), F32)],
        compiler_params=vm,
        name="post_s",
    )(xs, cact_s, ob_s, oc_s, gl_s, bg, wpa, wpb, wpc, wout, g2, wup, st_ffn, w_dw_f, bdw_f, wdown)

    heads = lambda a, b, s: a.reshape(b, s, N_HEADS, HEAD_DIM)
    from_t = lambda a: a.reshape(n, N_HEADS, HEAD_DIM, t).transpose(0, 3, 1, 2)
    conv_p = utail[:, CONV_HALO - (CONV_WIDTH - 1):, :]
    conv_s = jnp.concatenate([state_conv[:, 1:, :], u_s[:, None, :]], axis=1)
    ffn_p = ftail[:, SUBLANES - (FFN_CONV_WIDTH - 1):, :]
    ffn_s = jnp.concatenate([state_ffn_conv[:, 1:, :], up_s[:, None, :]], axis=1)
    return (y_p.reshape(n, t, d), y_s.reshape(nd, td, d),
            from_t(k_p), from_t(v_p), heads(k_s, nd, td), heads(v_s, nd, td),
            conv_p, conv_s, ffn_p, ffn_s,
            mk_p.reshape(n, n_mem, MEM_HEADS, MEM_HEAD_DIM), mv_p.reshape(n, n_mem, MEM_HEADS, MEM_HEAD_DIM))
```

```python
import functools

import numpy as np
import jax
import jax.numpy as jnp
from jax import lax
from jax.experimental import pallas as pl
from jax.experimental.pallas import tpu as pltpu

F32 = jnp.float32
BF16 = jnp.bfloat16

EPS = 1e-6
CONV_CH = 512
CONV_WIDTH = 31
N_HEADS = 8
HEAD_DIM = 64
ATTN_W = N_HEADS * HEAD_DIM
MOBA_BLOCK = 256
MOBA_TOPK = 3
MEM_HEADS = 4
MEM_HEAD_DIM = 128
MEM_W = MEM_HEADS * MEM_HEAD_DIM
FFN_CONV_WIDTH = 3
LANES = 128
SUBLANES = 8
TM = MOBA_BLOCK
CONV_HALO = 32
MASK_NEG = -float(2 ** 30)
GROUP = 16
V_ROWS = HEAD_DIM + 16
VMEM_LIMIT = 56 * 1024 * 1024

X_SEL = HEAD_DIM
X_RQ = HEAD_DIM + GROUP
X_ONE = X_RQ + 1
X_TQ = X_RQ + 2
X_ONE2 = X_RQ + 3


def _dot(a, b):
    return jnp.dot(a, b, preferred_element_type=F32)


def _dot_nt(a, b):
    return lax.dot_general(a, b, (((1,), (1,)), ((), ())), preferred_element_type=F32)


def _rms_rows(x, g):
    return x * lax.rsqrt(jnp.mean(x * x, axis=-1, keepdims=True) + EPS) * g


def _group_rms(z, bd, g):
    sq = z * z
    hi = sq.astype(BF16)
    lo = (sq - hi.astype(F32)).astype(BF16)
    ms = _dot(hi, bd) + _dot(lo, bd)
    return z * lax.rsqrt(ms + EPS) * g


def _exact_zero(v):
    bits = pltpu.bitcast(v, jnp.uint32)
    half = jnp.uint32(16)
    return pltpu.bitcast(lax.shift_right_logical(lax.shift_right_logical(bits, half), half), F32)


def _sigmoid(x):
    return 1.0 / (1.0 + jnp.exp(-x))


def _layernorm_silu(c, g, b):
    mu = jnp.mean(c, axis=-1, keepdims=True)
    xc = c - mu
    var = jnp.mean(xc * xc, axis=-1, keepdims=True)
    y = xc * lax.rsqrt(var + EPS) * g + b
    return y * _sigmoid(y)


def _premix_kernel(x_ref, g1_ref, w_ref, bd64_ref, bd128_ref, qg_ref, kg_ref, mqg_ref,
                   wdw_ref, bdw_ref, lng_ref, lnb_ref, kx_ref,
                   k_ref, v_ref, km_ref, qt_ref, ka_ref, vt_ref, mq_ref, cact_ref, utail_ref,
                   ubuf):
    t = pl.program_id(1)
    nt = pl.num_programs(1)
    c = CONV_CH

    @pl.when(t == 0)
    def _():
        ubuf[0:CONV_HALO, :] = jnp.zeros((CONV_HALO, c), F32)
        ubuf[CONV_HALO + TM:, :] = jnp.zeros((SUBLANES, c), F32)

    hn = _rms_rows(x_ref[...], g1_ref[...]).astype(BF16)

    a = _dot(hn, w_ref[:, 0:c])
    g = _dot(hn, w_ref[:, c:2 * c])
    u = a * _sigmoid(g)
    ubuf[CONV_HALO:CONV_HALO + TM, :] = u
    o = 2 * c
    zq = _dot(hn, w_ref[:, o:o + ATTN_W])
    zk = _dot(hn, w_ref[:, o + ATTN_W:o + 2 * ATTN_W])
    zv = _dot(hn, w_ref[:, o + 2 * ATTN_W:o + 3 * ATTN_W])
    zm = _dot(hn, w_ref[:, o + 3 * ATTN_W:o + 3 * ATTN_W + MEM_W])
    acc = jnp.broadcast_to(bdw_ref[...], (TM, c))
    base = CONV_HALO - (CONV_WIDTH - 1)
    span = TM + 2 * SUBLANES
    for b in range(SUBLANES):
        part = None
        for k in range(b, CONV_WIDTH, SUBLANES):
            term = wdw_ref[k:k + 1, :] * ubuf[k - b:k - b + span, :]
            part = term if part is None else part + term
        acc = acc + part[base + b:base + b + TM, :]
    ubuf[0:CONV_HALO, :] = ubuf[TM:TM + CONV_HALO, :]
    cact_ref[...] = _layernorm_silu(acc, lng_ref[...], lnb_ref[...]).astype(BF16)

    qn = _group_rms(zq, bd64_ref[...], qg_ref[...]) * (HEAD_DIM ** -0.5)
    kn = _group_rms(zk, bd64_ref[...], kg_ref[...])
    mq = _group_rms(zm, bd128_ref[...], mqg_ref[...])

    ones_rows = jnp.where(lax.broadcasted_iota(jnp.int32, (V_ROWS - HEAD_DIM, TM), 0) == 0, 1.0, 0.0).astype(BF16)
    for cb in range(ATTN_W // LANES):
        k_ref[LANES * cb:LANES * (cb + 1), :] = kn[:, LANES * cb:LANES * (cb + 1)].T
        vt = zv[:, LANES * cb:LANES * (cb + 1)].T
        v_ref[LANES * cb:LANES * (cb + 1), :] = vt
        for sub in range(2):
            r0 = V_ROWS * (2 * cb + sub)
            vt_ref[0, r0:r0 + HEAD_DIM, :] = vt[HEAD_DIM * sub:HEAD_DIM * (sub + 1), :].astype(BF16)
            vt_ref[0, r0 + HEAD_DIM:r0 + V_ROWS, :] = ones_rows
    mq_ref[...] = mq.astype(BF16)
    km_ref[0] = jnp.mean(kn, axis=0, keepdims=True)

    lane = lax.broadcasted_iota(jnp.int32, (TM, LANES), 1)
    rowf = lax.broadcasted_iota(jnp.int32, (TM, LANES), 0).astype(F32)
    tf = t.astype(F32)
    zero = jnp.zeros((TM, LANES), F32)
    eq = jnp.where(lane == X_RQ, rowf,
                   jnp.where(lane == X_ONE, 1.0,
                             jnp.where(lane == X_TQ, tf,
                                       jnp.where(lane == X_ONE2, 1.0, zero))))
    for hp in range(N_HEADS // 2):
        xq = qn[:, LANES * hp:LANES * (hp + 1)]
        xk = kn[:, LANES * hp:LANES * (hp + 1)]
        for sub in range(2):
            h = 2 * hp + sub
            if sub == 1:
                xq = pltpu.roll(xq, HEAD_DIM, 1)
                xk = pltpu.roll(xk, HEAD_DIM, 1)
            ek = kx_ref[0, :, LANES * h:LANES * (h + 1)].astype(F32)
            qt_ref[LANES * h:LANES * (h + 1), :] = jnp.where(lane < HEAD_DIM, xq, eq).T.astype(BF16)
            ka_ref[0, :, LANES * h:LANES * (h + 1)] = jnp.where(lane < HEAD_DIM, xk, ek).astype(BF16)

    @pl.when(t == nt - 1)
    def _():
        utail_ref[0] = ubuf[0:CONV_HALO, :]


def _moba_kernel(qt_ref, ka_ref, vt_ref, kmt_ref, o_ref, qh_scr, m_scr, acc_scr, *stage):
    i = pl.program_id(1)
    s_scr, mx_scr = stage[:N_HEADS], stage[N_HEADS:]
    key = lax.broadcasted_iota(jnp.int32, (TM, TM), 0)
    qry = lax.broadcasted_iota(jnp.int32, (TM, TM), 1)
    causal = key <= qry

    def scores(j, h, diagonal):
        q = qt_ref[LANES * h:LANES * (h + 1), :] if diagonal else qh_scr[h]
        s = _dot(ka_ref[j, :, LANES * h:LANES * (h + 1)], q)
        if diagonal:
            s = jnp.where(causal, s, -jnp.inf)
        s_scr[h][...] = s
        mx_scr[h][...] = jnp.max(s, axis=0, keepdims=True)

    for h in range(N_HEADS):
        scores(i, h, True)

    grow = lax.broadcasted_iota(jnp.int32, (N_HEADS * GROUP, TM), 0)
    blk = grow & (GROUP - 1)
    past = blk < i

    gate = _dot(kmt_ref[0], qt_ref[...])
    g = jnp.where(past, gate, -jnp.inf)
    rank = jnp.zeros(g.shape, F32)
    for s in range(1, GROUP):
        up = pltpu.roll(g, N_HEADS * GROUP - s, 0)
        down = pltpu.roll(g, GROUP - s, 0)
        wrapped = blk >= GROUP - s
        partner = jnp.where(wrapped, down, up)
        beats = jnp.where(wrapped, jnp.where(partner >= g, 1.0, 0.0), jnp.where(partner > g, 1.0, 0.0))
        rank = rank + beats
    nsel = jnp.where(past, jnp.where(rank >= MOBA_TOPK, 1.0, 0.0), 0.0).astype(BF16)
    for h in range(N_HEADS):
        r0 = LANES * h
        qh_scr[h, 0:X_SEL, :] = qt_ref[r0:r0 + X_SEL, :]
        qh_scr[h, X_SEL:X_RQ, :] = nsel[GROUP * h:GROUP * (h + 1), :]
        qh_scr[h, X_RQ:LANES, :] = qt_ref[r0 + X_RQ:r0 + LANES, :]

    def accumulate(j, h):
        m_old = m_scr[h]
        m_new = jnp.maximum(m_old, mx_scr[h][...])
        alpha = jnp.exp(m_old - m_new)
        p = jnp.exp(s_scr[h][...] - m_new).astype(BF16)
        acc_scr[h] = alpha * acc_scr[h] + _dot(vt_ref[j, V_ROWS * h:V_ROWS * (h + 1), :], p)
        m_scr[h] = m_new

    m_scr[...] = jnp.full(m_scr.shape, -jnp.inf, F32)
    acc_scr[...] = jnp.zeros(acc_scr.shape, F32)

    def body(j, carry):
        prev = jnp.where(j == 0, i, j - 1)
        for h in range(N_HEADS):
            accumulate(prev, h)
            scores(j, h, False)
        return carry

    lax.fori_loop(0, i, body, 0)
    last = jnp.where(i == 0, i, i - 1)
    for h in range(N_HEADS):
        accumulate(last, h)
    outs = []
    for h in range(N_HEADS):
        acc = acc_scr[h]
        outs.append(acc[0:HEAD_DIM, :] / acc[HEAD_DIM:HEAD_DIM + 1, :])
    o_ref[...] = jnp.concatenate(outs, axis=0).T.astype(BF16)


def _memkv_kernel(mem_ref, g_ref, w_ref, bd128_ref, mkg_ref, mk_ref, mv_ref, mkb_ref, mvb_ref):
    hn = _rms_rows(mem_ref[...], g_ref[...]).astype(BF16)
    mk = _group_rms(_dot(hn, w_ref[:, 0:MEM_W]), bd128_ref[...], mkg_ref[...])
    mv = _dot(hn, w_ref[:, MEM_W:2 * MEM_W])
    mk_ref[...] = mk
    mv_ref[...] = mv
    mkb_ref[...] = mk.astype(BF16)
    mvb_ref[...] = mv.astype(BF16)


def _mem_attend_rows(mq, mk, mv):
    outs = []
    for hh in range(MEM_HEADS):
        sl = slice(MEM_HEAD_DIM * hh, MEM_HEAD_DIM * (hh + 1))
        s = _dot_nt(mq[:, sl], mk[:, sl]) * (MEM_HEAD_DIM ** -0.5)
        m = jnp.max(s, axis=-1, keepdims=True)
        p = jnp.exp(s - m)
        l = jnp.sum(p, axis=-1, keepdims=True)
        outs.append(_dot(p.astype(BF16), mv[:, sl]) / l)
    return jnp.concatenate(outs, axis=-1)


def _merge_out(x, gl, ya, yb, yc, wout):
    d = x.shape[-1]
    merged = (_sigmoid(gl[:, 0:d]) * ya + _sigmoid(gl[:, d:2 * d]) * yb + _sigmoid(gl[:, 2 * d:3 * d]) * yc)
    return x + _dot(merged.astype(BF16), wout)


def _postmix_kernel(pt_ref, x_ref, cact_ref, ob_ref, mq_ref, mk_ref, mv_ref, g1_ref, wg_ref, bg_ref,
                    wpa_ref, wpb_ref, wpc_ref, wout_ref, q_ref, ck_hbm, x1_ref, sel_ref,
                    gate_scr, merged_scr, page_buf, page_sem, *, pages_per_step, n_blocks):
    step = pl.program_id(0) * pl.num_programs(1) + pl.program_id(1)
    n_steps = pl.num_programs(0) * pl.num_programs(1)
    slot = step % 2

    def page_copy(page, to_slot, r):
        return pltpu.make_async_copy(ck_hbm.at[page], page_buf.at[to_slot, r], page_sem.at[to_slot])

    def fetch(for_step, to_slot):
        for r in range(pages_per_step):
            page_copy(pt_ref[for_step * pages_per_step + r], to_slot, r).start()

    @pl.when(step == 0)
    def _():
        gate_scr[...] = jnp.zeros(gate_scr.shape, F32)
        fetch(0, 0)

    @pl.when(step + 1 < n_steps)
    def _():
        fetch(step + 1, 1 - slot)

    for r in range(pages_per_step):
        page_copy(0, slot, r).wait()
    page_refs = [page_buf.at[slot, r] for r in range(pages_per_step)]

    blocks_per_step = pages_per_step // PAGES_PER_BLOCK
    steps_per_sample = n_blocks // blocks_per_step
    share = step % steps_per_sample
    n_chunks = 4
    chunk = pages_per_step // n_chunks
    assert chunk % PAGES_PER_BLOCK == 0 and chunk * n_chunks == pages_per_step

    def gate_chunk(g, c):
        first = share * blocks_per_step + c * (chunk // PAGES_PER_BLOCK)
        return _gate_scores(g, first, q_ref, page_refs[c * chunk:(c + 1) * chunk])

    g = jnp.where(share == 0, 0.0, gate_scr[...])
    x = x_ref[...]
    d = x.shape[-1]
    hn = _rms_rows(x, g1_ref[...]).astype(BF16)
    oc = _mem_attend_rows(mq_ref[...], mk_ref[...], mv_ref[...]).astype(BF16)
    width = d // n_chunks
    for c in range(n_chunks):
        g = gate_chunk(g, c)
        tie = jnp.concatenate([_exact_zero(g)[0:1, :]] * (width // LANES), axis=1)
        merged = None
        for br, (src, w_ref) in enumerate(((cact_ref[...], wpa_ref), (ob_ref[...], wpb_ref), (oc, wpc_ref))):
            cols = slice(d * br + width * c, d * br + width * (c + 1))
            gl = _dot(hn, wg_ref[:, cols]) + (bg_ref[:, cols] + tie)
            term = _sigmoid(gl) * _dot(src, w_ref[:, width * c:width * (c + 1)])
            merged = term if merged is None else merged + term
        merged_scr[:, width * c:width * (c + 1)] = merged.astype(BF16)
    x1_ref[...] = x + _dot(merged_scr[...], wout_ref[...])
    gate_scr[...] = g
    _gate_select(share == steps_per_sample - 1, gate_scr, sel_ref, n_blocks)


def _ffn_kernel(x1_ref, g2_ref, wup_ref, wdw_ref, bdw_ref, wdown_ref, y_ref, tail_ref, upbuf, *, chunk):
    t = pl.program_id(1)
    nt = pl.num_programs(1)
    dff = wdown_ref.shape[0]

    @pl.when(t == 0)
    def _():
        upbuf[0:SUBLANES, :] = jnp.zeros((SUBLANES, 2 * dff), F32)

    x1 = x1_ref[...]
    hn = _rms_rows(x1, g2_ref[...]).astype(BF16)

    for c in range(0, 2 * dff, chunk):
        upbuf[SUBLANES:SUBLANES + TM, c:c + chunk] = _dot(hn, wup_ref[:, c:c + chunk])

    def conv(c):
        out = bdw_ref[:, c:c + chunk]
        for k in range(FFN_CONV_WIDTH):
            r0 = SUBLANES - (FFN_CONV_WIDTH - 1) + k
            out = out + wdw_ref[k:k + 1, c:c + chunk] * upbuf[r0:r0 + TM, c:c + chunk]
        return out

    y = x1
    for c in range(0, dff, chunk):
        a = conv(c)
        b = conv(dff + c)
        act = (a * _sigmoid(a) * b).astype(BF16)
        y = y + _dot(act, wdown_ref[c:c + chunk, :])
    y_ref[...] = y

    @pl.when(t == nt - 1)
    def _():
        tail_ref[0] = upbuf[TM:TM + SUBLANES, :]

    upbuf[0:SUBLANES, :] = upbuf[TM:TM + SUBLANES, :]


def _premix_s_kernel(x_ref, g1_ref, w_ref, bd64_ref, bd128_ref, qg_ref, kg_ref, mqg_ref,
                     st_ref, wdw_ref, bdw_ref, lng_ref, lnb_ref,
                     u_ref, q_ref, k_ref, v_ref, mq_ref, gl_ref, cact_ref):
    c = CONV_CH
    hn = _rms_rows(x_ref[...], g1_ref[...]).astype(BF16)
    a = _dot(hn, w_ref[:, 0:c])
    g = _dot(hn, w_ref[:, c:2 * c])
    u = a * _sigmoid(g)
    u_ref[...] = u
    acc = bdw_ref[...] + wdw_ref[CONV_WIDTH - 1:CONV_WIDTH, :] * u
    for k in range(CONV_WIDTH - 1):
        acc = acc + wdw_ref[k:k + 1, :] * st_ref[k]
    cact_ref[...] = _layernorm_silu(acc, lng_ref[...], lnb_ref[...]).astype(BF16)

    o = 2 * c
    q_ref[...] = _group_rms(_dot(hn, w_ref[:, o:o + ATTN_W]), bd64_ref[...], qg_ref[...])
    o += ATTN_W
    k_ref[...] = _group_rms(_dot(hn, w_ref[:, o:o + ATTN_W]), bd64_ref[...], kg_ref[...])
    o += ATTN_W
    v_ref[...] = _dot(hn, w_ref[:, o:o + ATTN_W])
    o += ATTN_W
    mq_ref[...] = _group_rms(_dot(hn, w_ref[:, o:o + MEM_W]), bd128_ref[...], mqg_ref[...])
    o += MEM_W
    gl_ref[...] = _dot(hn, w_ref[:, o:])


PAGES_PER_BLOCK = MOBA_BLOCK // LANES


def _gate_scores(g, first_block, q_ref, page_refs):
    head = lax.broadcasted_iota(jnp.int32, (N_HEADS, LANES), 0)
    lane = lax.broadcasted_iota(jnp.int32, (HEAD_DIM, LANES), 1)
    for h in range(N_HEADS):
        kmean = jnp.zeros((HEAD_DIM, LANES), F32)
        for r in range(0, len(page_refs), PAGES_PER_BLOCK):
            ksum = page_refs[r][h]
            for extra in range(1, PAGES_PER_BLOCK):
                ksum = ksum + page_refs[r + extra][h]
            col = jnp.sum(ksum, axis=1, keepdims=True) * (1.0 / MOBA_BLOCK)
            kmean = jnp.where(lane == first_block + r // PAGES_PER_BLOCK, col, kmean)
        prod = q_ref[0, h].astype(BF16).astype(F32) * kmean.astype(BF16).astype(F32)
        g = g + jnp.where(head == h, jnp.sum(prod, axis=0, keepdims=True), 0.0)
    return g


def _gate_select(last, gate_scr, sel_ref, n_blocks):
    @pl.when(last)
    def _():
        g = gate_scr[:, 0:n_blocks]
        bl = lax.broadcasted_iota(jnp.int32, (N_HEADS, n_blocks), 1)
        rank = jnp.zeros((N_HEADS, n_blocks), F32)
        for b in range(n_blocks):
            other = g[:, b:b + 1]
            rank = rank + jnp.where(bl > b, jnp.where(other >= g, 1.0, 0.0), jnp.where(other > g, 1.0, 0.0))
        lane_o = lax.broadcasted_iota(jnp.int32, (N_HEADS, LANES), 1)
        out = jnp.zeros((N_HEADS, LANES), F32)
        blf = bl.astype(F32)
        for r in range(MOBA_TOPK):
            idx = jnp.sum(jnp.where(rank == float(r), blf, 0.0), axis=-1, keepdims=True)
            out = jnp.where(lane_o == r, idx, out)
        sel_ref[0] = out.astype(jnp.int32)


def _decode_attn_kernel(page_ref, sel_ref, slope_ref, q_ref, kown_ref, vown_ref, ck_hbm, cv_hbm, o_ref,
                        k_buf, v_buf, sem, *, past_len):
    pages_per_block = MOBA_BLOCK // LANES
    tiles_per_head = MOBA_TOPK * pages_per_block
    n_tiles = N_HEADS * tiles_per_head
    n = pl.program_id(0)
    slot = n % 2

    def tile_copies(page, to_slot, tile):
        h = tile // tiles_per_head
        return (pltpu.make_async_copy(ck_hbm.at[page, h], k_buf.at[to_slot, tile], sem.at[0, to_slot]),
                pltpu.make_async_copy(cv_hbm.at[page, h], v_buf.at[to_slot, tile], sem.at[1, to_slot]))

    def fetch(sample, to_slot):
        for tile in range(n_tiles):
            for copy in tile_copies(page_ref[sample * n_tiles + tile], to_slot, tile):
                copy.start()

    @pl.when(n == 0)
    def _():
        fetch(0, 0)

    @pl.when(n + 1 < pl.num_programs(0))
    def _():
        fetch(n + 1, 1 - slot)

    for tile in range(n_tiles):
        for copy in tile_copies(0, slot, tile):
            copy.wait()
    k_refs = [k_buf.at[slot, tile] for tile in range(n_tiles)]
    v_refs = [v_buf.at[slot, tile] for tile in range(n_tiles)]

    lane = lax.broadcasted_iota(jnp.int32, (1, LANES), 1)
    lanef = lane.astype(F32)
    zeros = jnp.zeros((HEAD_DIM, tiles_per_head * LANES), BF16)
    scale = HEAD_DIM ** -0.5

    def head_tiles(tile_refs, h):
        t = jnp.concatenate([tile_refs[h * tiles_per_head + idx][...] for idx in range(tiles_per_head)], axis=1)
        t = t.astype(BF16)
        return jnp.concatenate([t, zeros] if h % 2 == 0 else [zeros, t], axis=0)

    def own_half(h):
        return (lane < HEAD_DIM) if h % 2 == 0 else (lane >= HEAD_DIM)

    def pair_lanes(ref, h):
        return ref[0][:, LANES * (h // 2):LANES * (h // 2 + 1)]

    qz, raw = [], []
    for h in range(N_HEADS):
        qz.append(jnp.where(own_half(h), pair_lanes(q_ref, h) * scale, 0.0).astype(BF16))
        raw.append(_dot(jnp.broadcast_to(qz[h], (SUBLANES, LANES)), head_tiles(k_refs, h))[0:1, :])
    probs = []
    for h in range(N_HEADS):
        dist = []
        for r in range(MOBA_TOPK):
            blk = sel_ref[(n * N_HEADS + h) * MOBA_TOPK + r]
            for half in range(pages_per_block):
                pos0 = (blk * MOBA_BLOCK + half * LANES).astype(F32)
                dist.append(float(past_len) - (pos0 + lanef))
        logits = raw[h] - slope_ref[h] * jnp.concatenate(dist, axis=1)
        kown = pair_lanes(kown_ref, h).astype(BF16).astype(F32)
        s_own = jnp.sum(qz[h].astype(F32) * kown, axis=-1, keepdims=True)
        m = jnp.maximum(s_own, jnp.max(logits, axis=-1, keepdims=True))
        p_own = jnp.exp(s_own - m)
        p = jnp.exp(logits - m)
        probs.append((p, p_own, p_own + jnp.sum(p, axis=-1, keepdims=True)))
    outs = []
    for h in range(N_HEADS):
        p, p_own, l = probs[h]
        p8 = jnp.broadcast_to(p.astype(BF16), (SUBLANES, p.shape[1]))
        vown = jnp.where(own_half(h), pair_lanes(vown_ref, h).astype(BF16).astype(F32), 0.0)
        acc = p_own.astype(BF16).astype(F32) * vown + _dot_nt(p8, head_tiles(v_refs, h))[0:1, :]
        outs.append(acc / l)
    o_ref[0] = jnp.concatenate([outs[h] + outs[h + 1] for h in range(0, N_HEADS, 2)], axis=1)


def _memattn_s_kernel(mq_ref, mk_ref, mv_ref, o_ref, *, group, n_mem):
    units = [(s, hh) for s in range(group) for hh in range(MEM_HEADS)]
    scores = []
    for s, hh in units:
        mq = jnp.broadcast_to(mq_ref[s][:, MEM_HEAD_DIM * hh:MEM_HEAD_DIM * (hh + 1)], (SUBLANES, MEM_HEAD_DIM))
        mk = mk_ref[s, pl.ds(hh, n_mem, stride=MEM_HEADS), :].astype(BF16)
        scores.append(_dot_nt(mq.astype(BF16), mk)[0:1, :] * (MEM_HEAD_DIM ** -0.5))
    probs = []
    for sc in scores:
        p = jnp.exp(sc - jnp.max(sc, axis=-1, keepdims=True))
        probs.append((p, jnp.sum(p, axis=-1, keepdims=True)))
    outs = {}
    for (s, hh), (p, l) in zip(units, probs):
        mv = mv_ref[s, pl.ds(hh, n_mem, stride=MEM_HEADS), :].astype(BF16)
        p8 = jnp.broadcast_to(p.astype(BF16), (SUBLANES, n_mem))
        outs[s, hh] = _dot(p8, mv)[0:1, :] / l
    for s in range(group):
        o_ref[s] = jnp.concatenate([outs[s, hh] for hh in range(MEM_HEADS)], axis=-1)


def _post_s_kernel(x_ref, cact_ref, ob_ref, oc_ref, gl_ref, bg_ref, wpa_ref, wpb_ref, wpc_ref, wout_ref,
                   g2_ref, wup_ref, st_ref, wdw_ref, bdw_ref, wdown_ref, y_ref, up_ref):
    x = x_ref[...]
    dff = wdown_ref.shape[0]
    ya = _dot(cact_ref[...], wpa_ref[...])
    yb = _dot(ob_ref[...].astype(BF16), wpb_ref[...])
    yc = _dot(oc_ref[...].astype(BF16), wpc_ref[...])
    x1 = _merge_out(x, gl_ref[...] + bg_ref[...], ya, yb, yc, wout_ref[...])
    up = _dot(_rms_rows(x1, g2_ref[...]).astype(BF16), wup_ref[...])
    up_ref[...] = up
    cv = bdw_ref[...] + wdw_ref[FFN_CONV_WIDTH - 1:FFN_CONV_WIDTH, :] * up
    for k in range(FFN_CONV_WIDTH - 1):
        cv = cv + wdw_ref[k:k + 1, :] * st_ref[k]
    a = cv[:, 0:dff]
    b = cv[:, dff:]
    act = (a * _sigmoid(a) * b).astype(BF16)
    y_ref[...] = x1 + _dot(act, wdown_ref[...])


def _resident(shape):
    nd = len(shape)
    return pl.BlockSpec(shape, lambda *_: (0,) * nd, pipeline_mode=pl.Buffered(1))


def _params(n_axes):
    return pltpu.CompilerParams(dimension_semantics=("arbitrary",) * n_axes, vmem_limit_bytes=VMEM_LIMIT)


def kernel(x_prompt, x_sample, mem_prompt, cache_k, cache_v, page_table, state_conv, state_ffn_conv, cache_mem_k, cache_mem_v, norm1_g, w_in, b_gate, w_dw_a, b_dw_a, ln_a_g, ln_a_b, w_proj_a, q_norm_g, k_norm_g, w_proj_b, mem_norm_g, w_mem_kv, mq_norm_g, mk_norm_g, w_proj_c, w_out, norm2_g, w_up, w_dw_f, b_dw_f, w_down):
    n, t, d = x_prompt.shape
    nd, td, _ = x_sample.shape
    n_mem = mem_prompt.shape[1]
    n_pool, page_size = cache_k.shape[:2]
    n_pages = page_table.shape[1]
    past_len = n_pages * page_size
    dff = w_down.shape[0]
    nt = t // TM
    n_mix = 2 * CONV_CH + 3 * ATTN_W + MEM_W
    assert t % TM == 0 and nt <= GROUP and td == 1
    assert past_len % MOBA_BLOCK == 0 and MOBA_BLOCK == 2 * page_size and page_size == LANES
    n_blocks = past_len // MOBA_BLOCK

    row = lambda v: v.reshape(1, -1).astype(F32)
    w_in_b = w_in.astype(BF16)
    w_mix, w_gate = w_in_b[:, :n_mix], w_in_b[:, n_mix:]
    wpa, wpb, wpc = w_proj_a.astype(BF16), w_proj_b.astype(BF16), w_proj_c.astype(BF16)
    wout, wup, wdown = w_out.astype(BF16), w_up.astype(BF16), w_down.astype(BF16)
    g1, g2 = row(norm1_g), row(norm2_g)
    qg = row(jnp.tile(q_norm_g, N_HEADS))
    kg = row(jnp.tile(k_norm_g, N_HEADS))
    mqg = row(jnp.tile(mq_norm_g, MEM_HEADS))
    mkg = row(jnp.tile(mk_norm_g, MEM_HEADS))
    bdw_a, lng, lnb, bg, bdw_f = row(b_dw_a), row(ln_a_g), row(ln_a_b), row(b_gate), row(b_dw_f)
    grp = np.arange(ATTN_W)
    bd64 = jnp.asarray((grp[:, None] // HEAD_DIM == grp[None, :] // HEAD_DIM) / HEAD_DIM, BF16)
    bd128 = jnp.asarray((grp[:, None] // MEM_HEAD_DIM == grp[None, :] // MEM_HEAD_DIM) / MEM_HEAD_DIM, BF16)
    slopes = 2.0 ** (-8.0 * np.arange(1, N_HEADS + 1) / N_HEADS)
    col = jnp.arange(N_HEADS * LANES, dtype=jnp.int32)[None, None, :] % LANES
    sl = jnp.asarray(np.repeat(slopes, LANES), F32)[None, None, :]
    tile_f = jnp.arange(nt, dtype=F32)[:, None, None]
    row_f = jnp.arange(TM, dtype=F32)[None, :, None]
    zero3 = jnp.zeros((nt, TM, N_HEADS * LANES), F32)
    k_extra = jnp.where(col == X_SEL + tile_f.astype(jnp.int32), MASK_NEG, zero3)
    k_extra = jnp.where(col == X_RQ, -sl + zero3, k_extra)
    k_extra = jnp.where(col == X_ONE, sl * row_f + zero3, k_extra)
    k_extra = jnp.where(col == X_TQ, -sl * MOBA_BLOCK + zero3, k_extra)
    k_extra = jnp.where(col == X_ONE2, sl * MOBA_BLOCK * tile_f + zero3, k_extra).astype(BF16)

    xp = x_prompt.reshape(n * t, d)
    tile = lambda w: pl.BlockSpec((TM, w), lambda b, i, *_: (b * nt + i, 0))
    aw = N_HEADS * LANES
    vw = N_HEADS * V_ROWS
    blk3 = lambda r, c: pl.BlockSpec((1, r, c), lambda b, i: (b * nt + i, 0, 0))
    k_p, v_p, km, qt, ka, vt, mq, cact, utail = pl.pallas_call(
        _premix_kernel,
        grid=(n, nt),
        in_specs=[tile(d), _resident((1, d)), _resident((d, n_mix)), _resident((ATTN_W, ATTN_W)),
                  _resident((MEM_W, MEM_W)), _resident((1, ATTN_W)), _resident((1, ATTN_W)),
                  _resident((1, MEM_W)), _resident((CONV_WIDTH, CONV_CH)), _resident((1, CONV_CH)),
                  _resident((1, CONV_CH)), _resident((1, CONV_CH)),
                  pl.BlockSpec((1, TM, aw), lambda b, i: (i, 0, 0))],
        out_specs=[pl.BlockSpec((ATTN_W, TM), lambda b, i: (b, i)), pl.BlockSpec((ATTN_W, TM), lambda b, i: (b, i)),
                   pl.BlockSpec((1, 1, ATTN_W), lambda b, i: (b * nt + i, 0, 0)),
                   pl.BlockSpec((aw, TM), lambda b, i: (b, i)), blk3(TM, aw), blk3(vw, TM),
                   tile(MEM_W), tile(CONV_CH),
                   pl.BlockSpec((1, CONV_HALO, CONV_CH), lambda b, i: (b, 0, 0))],
        out_shape=[jax.ShapeDtypeStruct((n * ATTN_W, t), F32), jax.ShapeDtypeStruct((n * ATTN_W, t), F32),
                   jax.ShapeDtypeStruct((n * nt, 1, ATTN_W), F32),
                   jax.ShapeDtypeStruct((n * aw, t), BF16), jax.ShapeDtypeStruct((n * nt, TM, aw), BF16),
                   jax.ShapeDtypeStruct((n * nt, vw, TM), BF16), jax.ShapeDtypeStruct((n * t, MEM_W), BF16),
                   jax.ShapeDtypeStruct((n * t, CONV_CH), BF16),
                   jax.ShapeDtypeStruct((n, CONV_HALO, CONV_CH), F32)],
        scratch_shapes=[pltpu.VMEM((CONV_HALO + TM + SUBLANES, CONV_CH), F32)],
        compiler_params=_params(2),
        name="premix",
    )(xp, g1, w_mix, bd64, bd128, qg, kg, mqg, w_dw_a, bdw_a, lng, lnb, k_extra)

    km4 = km.reshape(n, nt, N_HEADS, HEAD_DIM).transpose(0, 2, 1, 3)
    km4 = jnp.pad(km4, ((0, 0), (0, 0), (0, GROUP - nt), (0, LANES - HEAD_DIM)))
    kmt = (km4[:, :, :, None, :] * jnp.eye(N_HEADS, dtype=F32)[None, :, None, :, None])
    kmt = kmt.reshape(n, N_HEADS * GROUP, aw).astype(BF16)

    seq3 = lambda r, c: pl.BlockSpec((nt, r, c), lambda b, i: (b, 0, 0))
    ob = pl.pallas_call(
        _moba_kernel,
        grid=(n, nt),
        in_specs=[pl.BlockSpec((aw, TM), lambda b, i: (b, i)), seq3(TM, aw), seq3(vw, TM),
                  pl.BlockSpec((1, N_HEADS * GROUP, aw), lambda b, i: (b, 0, 0))],
        out_specs=tile(ATTN_W),
        out_shape=jax.ShapeDtypeStruct((n * t, ATTN_W), BF16),
        scratch_shapes=[pltpu.VMEM((N_HEADS, LANES, TM), BF16), pltpu.VMEM((N_HEADS, 1, TM), F32),
                        pltpu.VMEM((N_HEADS, V_ROWS, TM), F32)]
        + [pltpu.VMEM((TM, TM), F32)] * N_HEADS + [pltpu.VMEM((1, TM), F32)] * N_HEADS,
        compiler_params=_params(2),
        name="moba",
    )(qt, ka, vt, kmt)

    mem = mem_prompt.reshape(n * n_mem, d)
    mtile = lambda w: pl.BlockSpec((n_mem, w), lambda b: (b, 0))
    mk_p, mv_p, mkb, mvb = pl.pallas_call(
        _memkv_kernel,
        grid=(n,),
        in_specs=[mtile(d), _resident((1, d)), _resident((d, 2 * MEM_W)), _resident((MEM_W, MEM_W)),
                  _resident((1, MEM_W))],
        out_specs=[mtile(MEM_W)] * 4,
        out_shape=[jax.ShapeDtypeStruct((n * n_mem, MEM_W), F32)] * 2
        + [jax.ShapeDtypeStruct((n * n_mem, MEM_W), BF16)] * 2,
        compiler_params=_params(1),
        name="memkv",
    )(mem, row(mem_norm_g), w_mem_kv.astype(BF16), bd128, mkg)

    xs = x_sample.reshape(nd, d)
    st_conv = state_conv.transpose(1, 0, 2)
    st_ffn = state_ffn_conv.transpose(1, 0, 2)
    vm = pltpu.CompilerParams(vmem_limit_bytes=VMEM_LIMIT)
    u_s, q_s, k_s, v_s, mq_s, gl_s, cact_s = pl.pallas_call(
        _premix_s_kernel,
        out_shape=[jax.ShapeDtypeStruct((nd, CONV_CH), F32), jax.ShapeDtypeStruct((nd, ATTN_W), F32),
                   jax.ShapeDtypeStruct((nd, ATTN_W), F32), jax.ShapeDtypeStruct((nd, ATTN_W), F32),
                   jax.ShapeDtypeStruct((nd, MEM_W), F32), jax.ShapeDtypeStruct((nd, 3 * d), F32),
                   jax.ShapeDtypeStruct((nd, CONV_CH), BF16)],
        compiler_params=vm,
        name="premix_s",
    )(xs, g1, w_in_b, bd64, bd128, qg, kg, mqg, st_conv, w_dw_a, bdw_a, lng, lnb)

    ck = cache_k.transpose(0, 2, 3, 1)
    cv = cache_v.transpose(0, 2, 3, 1)
    pt_flat = page_table.reshape(-1).astype(jnp.int32)
    q_cols = jnp.broadcast_to(q_s.reshape(nd, N_HEADS, HEAD_DIM, 1), (nd, N_HEADS, HEAD_DIM, page_size))
    assert (n * nt) % nd == 0 and n_pages % ((n * nt) // nd) == 0 and n_blocks <= LANES
    steps_per_sample = (n * nt) // nd
    pages_per_step = n_pages // steps_per_sample
    assert pages_per_step % (MOBA_BLOCK // page_size) == 0

    memb = pl.BlockSpec((n_mem, MEM_W), lambda b, i, *_: (b, 0))
    x1, sel = pl.pallas_call(
        functools.partial(_postmix_kernel, pages_per_step=pages_per_step, n_blocks=n_blocks),
        grid_spec=pltpu.PrefetchScalarGridSpec(
            num_scalar_prefetch=1,
            grid=(n, nt),
            in_specs=[tile(d), tile(CONV_CH), tile(ATTN_W), tile(MEM_W), memb, memb, _resident((1, d)),
                      _resident((d, 3 * d)), _resident((1, 3 * d)), _resident((CONV_CH, d)),
                      _resident((ATTN_W, d)), _resident((MEM_W, d)), _resident((d, d)),
                      pl.BlockSpec((1, N_HEADS, HEAD_DIM, page_size),
                                   lambda b, i, pt: ((b * nt + i) // steps_per_sample, 0, 0, 0)),
                      pl.BlockSpec(memory_space=pl.ANY)],
            out_specs=[tile(d), pl.BlockSpec((1, N_HEADS, LANES),
                                             lambda b, i, pt: ((b * nt + i) // steps_per_sample, 0, 0))],
            scratch_shapes=[pltpu.VMEM((N_HEADS, LANES), F32), pltpu.VMEM((TM, d), BF16),
                            pltpu.VMEM((2, pages_per_step, N_HEADS, HEAD_DIM, page_size), F32),
                            pltpu.SemaphoreType.DMA((2,))]),
        out_shape=[jax.ShapeDtypeStruct((n * t, d), F32), jax.ShapeDtypeStruct((nd, N_HEADS, LANES), jnp.int32)],
        compiler_params=_params(2),
        name="postmix",
    )(pt_flat, xp, cact, ob, mq, mkb, mvb, g1, w_gate, bg, wpa, wpb, wpc, wout, q_cols, ck)
    sel_flat = sel[:, :, :MOBA_TOPK].reshape(-1)

    chunk = dff // 2
    assert chunk % LANES == 0
    y_p, ftail = pl.pallas_call(
        functools.partial(_ffn_kernel, chunk=chunk),
        grid=(n, nt),
        in_specs=[tile(d), _resident((1, d)), _resident((d, 2 * dff)), _resident((FFN_CONV_WIDTH, 2 * dff)),
                  _resident((1, 2 * dff)), _resident((dff, d))],
        out_specs=[tile(d), pl.BlockSpec((1, SUBLANES, 2 * dff), lambda b, i: (b, 0, 0))],
        out_shape=[jax.ShapeDtypeStruct((n * t, d), F32), jax.ShapeDtypeStruct((n, SUBLANES, 2 * dff), F32)],
        scratch_shapes=[pltpu.VMEM((SUBLANES + TM, 2 * dff), F32)],
        compiler_params=_params(2),
        name="ffn",
    )(x1, g2, wup, w_dw_f, bdw_f, wdown)

    row_spec = pl.BlockSpec((1, 1, ATTN_W), lambda b, *_: (b, 0, 0))
    rows3 = lambda a: a.reshape(nd, 1, -1)

    halves = jnp.arange(PAGES_PER_BLOCK, dtype=jnp.int32)
    sel_pages = PAGES_PER_BLOCK * sel[:, :, :MOBA_TOPK, None] + halves
    page_ids = jnp.take_along_axis(page_table.astype(jnp.int32), sel_pages.reshape(nd, -1), axis=1).reshape(-1)
    tiles_per_sample = N_HEADS * MOBA_TOPK * PAGES_PER_BLOCK
    tile_buf = pltpu.VMEM((2, tiles_per_sample, HEAD_DIM, page_size), F32)
    ob_s = pl.pallas_call(
        functools.partial(_decode_attn_kernel, past_len=past_len),
        grid_spec=pltpu.PrefetchScalarGridSpec(
            num_scalar_prefetch=2,
            grid=(nd,),
            in_specs=[pl.BlockSpec(memory_space=pltpu.SMEM), row_spec, row_spec, row_spec,
                      pl.BlockSpec(memory_space=pl.ANY), pl.BlockSpec(memory_space=pl.ANY)],
            out_specs=row_spec,
            scratch_shapes=[tile_buf, tile_buf, pltpu.SemaphoreType.DMA((2, 2))]),
        out_shape=jax.ShapeDtypeStruct((nd, 1, ATTN_W), F32),
        compiler_params=_params(1),
        name="decode_attn",
    )(page_ids, sel_flat, jnp.asarray(slopes, F32), rows3(q_s), rows3(k_s), rows3(v_s), ck, cv)
    ob_s = ob_s.reshape(nd, ATTN_W)

    group = SUBLANES
    assert nd % group == 0
    cm_spec = pl.BlockSpec((group, n_mem * MEM_HEADS, MEM_HEAD_DIM), lambda b: (b, 0, 0))
    mq_spec = pl.BlockSpec((group, 1, MEM_W), lambda b: (b, 0, 0))
    oc_s = pl.pallas_call(
        functools.partial(_memattn_s_kernel, group=group, n_mem=n_mem),
        grid=(nd // group,),
        in_specs=[mq_spec, cm_spec, cm_spec],
        out_specs=mq_spec,
        out_shape=jax.ShapeDtypeStruct((nd, 1, MEM_W), F32),
        compiler_params=_params(1),
        name="memattn_s",
    )(rows3(mq_s), cache_mem_k.reshape(nd, n_mem * MEM_HEADS, MEM_HEAD_DIM),
      cache_mem_v.reshape(nd, n_mem * MEM_HEADS, MEM_HEAD_DIM))
    oc_s = oc_s.reshape(nd, MEM_W)

    y_s, up_s = pl.pallas_call(
        _post_s_kernel,
        out_shape=[jax.ShapeDtypeStruct((nd, d), F32), jax.ShapeDtypeStruct((nd, 2 * dff), F32)],
        compiler_params=vm,
        name="post_s",
    )(xs, cact_s, ob_s, oc_s, gl_s, bg, wpa, wpb, wpc, wout, g2, wup, st_ffn, w_dw_f, bdw_f, wdown)

    heads = lambda a, b, s: a.reshape(b, s, N_HEADS, HEAD_DIM)
    from_t = lambda a: a.reshape(n, N_HEADS, HEAD_DIM, t).transpose(0, 3, 1, 2)
    conv_p = utail[:, CONV_HALO - (CONV_WIDTH - 1):, :]
    conv_s = jnp.concatenate([state_conv[:, 1:, :], u_s[:, None, :]], axis=1)
    ffn_p = ftail[:, SUBLANES - (FFN_CONV_WIDTH - 1):, :]
    ffn_s = jnp.concatenate([state_ffn_conv[:, 1:, :], up_s[:, None, :]], axis=1)
    return (y_p.reshape(n, t, d), y_s.reshape(nd, td, d),
            from_t(k_p), from_t(v_p), heads(k_s, nd, td), heads(v_s, nd, td),
            conv_p, conv_s, ffn_p, ffn_s,
            mk_p.reshape(n, n_mem, MEM_HEADS, MEM_HEAD_DIM), mv_p.reshape(n, n_mem, MEM_HEADS, MEM_HEAD_DIM))
```

```python
import functools

import numpy as np
import jax
import jax.numpy as jnp
from jax import lax
from jax.experimental import pallas as pl
from jax.experimental.pallas import tpu as pltpu

F32 = jnp.float32
BF16 = jnp.bfloat16

EPS = 1e-6
CONV_CH = 512
CONV_WIDTH = 31
N_HEADS = 8
HEAD_DIM = 64
ATTN_W = N_HEADS * HEAD_DIM
MOBA_BLOCK = 256
MOBA_TOPK = 3
MEM_HEADS = 4
MEM_HEAD_DIM = 128
MEM_W = MEM_HEADS * MEM_HEAD_DIM
FFN_CONV_WIDTH = 3
LANES = 128
SUBLANES = 8
TM = MOBA_BLOCK
CONV_HALO = 32
MASK_NEG = -float(2 ** 30)
GROUP = 16
V_ROWS = HEAD_DIM + 16
BLOCKS_PER_TRIP = 4
VMEM_LIMIT = 56 * 1024 * 1024

X_SEL = HEAD_DIM
X_PARTS = 3
X_RQ = HEAD_DIM + GROUP
X_RK = X_RQ + X_PARTS
X_TQ = X_RK + X_PARTS
X_TK = X_TQ + X_PARTS
LOG2E = 1.4426950408889634


def _dot(a, b):
    return jnp.dot(a, b, preferred_element_type=F32)


def _dot_nt(a, b):
    return lax.dot_general(a, b, (((1,), (1,)), ((), ())), preferred_element_type=F32)


def _rms_rows(x, g):
    return x * lax.rsqrt(jnp.mean(x * x, axis=-1, keepdims=True) + EPS) * g


def _group_rms(z, bd, g):
    sq = z * z
    hi = sq.astype(BF16)
    lo = (sq - hi.astype(F32)).astype(BF16)
    ms = _dot(hi, bd) + _dot(lo, bd)
    return z * lax.rsqrt(ms + EPS) * g


def _exact_zero(v):
    bits = pltpu.bitcast(v, jnp.uint32)
    half = jnp.uint32(16)
    return pltpu.bitcast(lax.shift_right_logical(lax.shift_right_logical(bits, half), half), F32)


def _sigmoid(x):
    return 1.0 / (1.0 + jnp.exp(-x))


def _layernorm_silu(c, g, b):
    mu = jnp.mean(c, axis=-1, keepdims=True)
    xc = c - mu
    var = jnp.mean(xc * xc, axis=-1, keepdims=True)
    y = xc * lax.rsqrt(var + EPS) * g + b
    return y * _sigmoid(y)


def _premix_kernel(x_ref, g1_ref, w_ref, bd64_ref, bd128_ref, qg_ref, kg_ref, mqg_ref,
                   wdw_ref, bdw_ref, lng_ref, lnb_ref, qx_ref, kx_ref,
                   k_ref, v_ref, km_ref, qt_ref, ka_ref, vt_ref, mq_ref, cact_ref, utail_ref,
                   ubuf):
    t = pl.program_id(1)
    nt = pl.num_programs(1)
    c = CONV_CH

    @pl.when(t == 0)
    def _():
        ubuf[0:CONV_HALO, :] = jnp.zeros((CONV_HALO, c), F32)
        ubuf[CONV_HALO + TM:, :] = jnp.zeros((SUBLANES, c), F32)

    hn = _rms_rows(x_ref[...], g1_ref[...]).astype(BF16)

    a = _dot(hn, w_ref[:, 0:c])
    g = _dot(hn, w_ref[:, c:2 * c])
    u = a * _sigmoid(g)
    ubuf[CONV_HALO:CONV_HALO + TM, :] = u
    o = 2 * c
    zq = _dot(hn, w_ref[:, o:o + ATTN_W])
    zk = _dot(hn, w_ref[:, o + ATTN_W:o + 2 * ATTN_W])
    zv = _dot(hn, w_ref[:, o + 2 * ATTN_W:o + 3 * ATTN_W])
    zm = _dot(hn, w_ref[:, o + 3 * ATTN_W:o + 3 * ATTN_W + MEM_W])
    acc = jnp.broadcast_to(bdw_ref[...], (TM, c))
    base = CONV_HALO - (CONV_WIDTH - 1)
    span = TM + 2 * SUBLANES
    for b in range(SUBLANES):
        part = None
        for k in range(b, CONV_WIDTH, SUBLANES):
            term = wdw_ref[k:k + 1, :] * ubuf[k - b:k - b + span, :]
            part = term if part is None else part + term
        acc = acc + part[base + b:base + b + TM, :]
    ubuf[0:CONV_HALO, :] = ubuf[TM:TM + CONV_HALO, :]
    cact_ref[...] = _layernorm_silu(acc, lng_ref[...], lnb_ref[...]).astype(BF16)

    qn = _group_rms(zq, bd64_ref[...], qg_ref[...]) * (HEAD_DIM ** -0.5)
    kn = _group_rms(zk, bd64_ref[...], kg_ref[...])
    mq = _group_rms(zm, bd128_ref[...], mqg_ref[...])

    ones_rows = jnp.where(lax.broadcasted_iota(jnp.int32, (V_ROWS - HEAD_DIM, TM), 0) == 0, 1.0, 0.0).astype(BF16)
    for cb in range(ATTN_W // LANES):
        k_ref[LANES * cb:LANES * (cb + 1), :] = kn[:, LANES * cb:LANES * (cb + 1)].T
        vt = zv[:, LANES * cb:LANES * (cb + 1)].T
        v_ref[LANES * cb:LANES * (cb + 1), :] = vt
        for sub in range(2):
            r0 = V_ROWS * (2 * cb + sub)
            vt_ref[0, r0:r0 + HEAD_DIM, :] = vt[HEAD_DIM * sub:HEAD_DIM * (sub + 1), :].astype(BF16)
            vt_ref[0, r0 + HEAD_DIM:r0 + V_ROWS, :] = ones_rows
    mq_ref[...] = mq.astype(BF16)
    km_ref[0] = jnp.mean(kn, axis=0, keepdims=True)

    lane = lax.broadcasted_iota(jnp.int32, (TM, LANES), 1)
    for hp in range(N_HEADS // 2):
        xq = qn[:, LANES * hp:LANES * (hp + 1)]
        xk = kn[:, LANES * hp:LANES * (hp + 1)] * LOG2E
        for sub in range(2):
            h = 2 * hp + sub
            if sub == 1:
                xq = pltpu.roll(xq, HEAD_DIM, 1)
                xk = pltpu.roll(xk, HEAD_DIM, 1)
            eq = qx_ref[0, :, LANES * h:LANES * (h + 1)].astype(F32)
            ek = kx_ref[0, :, LANES * h:LANES * (h + 1)].astype(F32)
            qt_ref[LANES * h:LANES * (h + 1), :] = jnp.where(lane < HEAD_DIM, xq, eq).T.astype(BF16)
            ka_ref[0, :, LANES * h:LANES * (h + 1)] = jnp.where(lane < HEAD_DIM, xk, ek).astype(BF16)

    @pl.when(t == nt - 1)
    def _():
        utail_ref[0] = ubuf[0:CONV_HALO, :]


def _moba_kernel(qt_ref, ka_ref, vt_ref, kmt_ref, o_ref, qh_scr, m_scr, acc_scr, *stage):
    i = pl.program_id(1)
    s_scr, mx_scr = stage[:N_HEADS], stage[N_HEADS:]
    key = lax.broadcasted_iota(jnp.int32, (TM, TM), 0)
    qry = lax.broadcasted_iota(jnp.int32, (TM, TM), 1)
    causal = key <= qry

    def scores(j, h, diagonal):
        q = qt_ref[LANES * h:LANES * (h + 1), :] if diagonal else qh_scr[h]
        s = _dot(ka_ref[j, :, LANES * h:LANES * (h + 1)], q)
        if diagonal:
            s = jnp.where(causal, s, -jnp.inf)
        s_scr[h][...] = s
        mx_scr[h][...] = jnp.max(s, axis=0, keepdims=True)

    for h in range(N_HEADS):
        scores(i, h, True)

    grow = lax.broadcasted_iota(jnp.int32, (N_HEADS * GROUP, TM), 0)
    blk = grow & (GROUP - 1)
    past = blk < i

    gate = _dot(kmt_ref[0], qt_ref[...])
    g = jnp.where(past, gate, -jnp.inf)
    rank = jnp.zeros(g.shape, F32)
    for s in range(1, GROUP):
        up = pltpu.roll(g, N_HEADS * GROUP - s, 0)
        down = pltpu.roll(g, GROUP - s, 0)
        wrapped = blk >= GROUP - s
        partner = jnp.where(wrapped, down, up)
        beats = jnp.where(wrapped, jnp.where(partner >= g, 1.0, 0.0), jnp.where(partner > g, 1.0, 0.0))
        rank = rank + beats
    nsel = jnp.where(past, jnp.where(rank >= MOBA_TOPK, 1.0, 0.0), 0.0).astype(BF16)
    for h in range(N_HEADS):
        r0 = LANES * h
        qh_scr[h, 0:X_SEL, :] = qt_ref[r0:r0 + X_SEL, :]
        qh_scr[h, X_SEL:X_RQ, :] = nsel[GROUP * h:GROUP * (h + 1), :]
        qh_scr[h, X_RQ:LANES, :] = qt_ref[r0 + X_RQ:r0 + LANES, :]

    def accumulate(j, h):
        m_old = m_scr[h]
        m_new = jnp.maximum(m_old, mx_scr[h][...])
        alpha = jnp.exp2(m_old - m_new)
        p = jnp.exp2(s_scr[h][...] - m_new).astype(BF16)
        acc_scr[h] = alpha * acc_scr[h] + _dot(vt_ref[j, V_ROWS * h:V_ROWS * (h + 1), :], p)
        m_scr[h] = m_new

    m_scr[...] = jnp.full(m_scr.shape, -jnp.inf, F32)
    acc_scr[...] = jnp.zeros(acc_scr.shape, F32)

    def step(prev, j):
        for h in range(N_HEADS):
            accumulate(prev, h)
            scores(j, h, False)

    def steps(first, count):
        step(jnp.where(first == 0, i, first - 1), first)
        for extra in range(1, count):
            step(first + extra - 1, first + extra)

    def body(trip, carry):
        steps(trip * BLOCKS_PER_TRIP, BLOCKS_PER_TRIP)
        return carry

    lax.fori_loop(0, i // BLOCKS_PER_TRIP, body, 0)
    size = BLOCKS_PER_TRIP // 2
    while size:
        done = (i // (2 * size)) * (2 * size)

        @pl.when((i // size) % 2 == 1)
        def _(done=done, size=size):
            steps(done, size)

        size //= 2
    last = jnp.where(i == 0, i, i - 1)
    for h in range(N_HEADS):
        accumulate(last, h)
    outs = []
    for h in range(N_HEADS):
        acc = acc_scr[h]
        outs.append(acc[0:HEAD_DIM, :] / acc[HEAD_DIM:HEAD_DIM + 1, :])
    o_ref[...] = jnp.concatenate(outs, axis=0).T.astype(BF16)


def _memkv_kernel(mem_ref, g_ref, w_ref, bd128_ref, mkg_ref, mk_ref, mv_ref, mkb_ref, mvb_ref):
    hn = _rms_rows(mem_ref[...], g_ref[...]).astype(BF16)
    mk = _group_rms(_dot(hn, w_ref[:, 0:MEM_W]), bd128_ref[...], mkg_ref[...])
    mv = _dot(hn, w_ref[:, MEM_W:2 * MEM_W])
    mk_ref[...] = mk
    mv_ref[...] = mv
    mkb_ref[...] = mk.astype(BF16)
    mvb_ref[...] = mv.astype(BF16)


def _mem_attend_rows(mq, mk, mv):
    outs = []
    for hh in range(MEM_HEADS):
        sl = slice(MEM_HEAD_DIM * hh, MEM_HEAD_DIM * (hh + 1))
        s = _dot_nt(mq[:, sl], mk[:, sl]) * (MEM_HEAD_DIM ** -0.5)
        m = jnp.max(s, axis=-1, keepdims=True)
        p = jnp.exp(s - m)
        l = jnp.sum(p, axis=-1, keepdims=True)
        outs.append(_dot(p.astype(BF16), mv[:, sl]) / l)
    return jnp.concatenate(outs, axis=-1)


def _merge_out(x, gl, ya, yb, yc, wout):
    d = x.shape[-1]
    merged = (_sigmoid(gl[:, 0:d]) * ya + _sigmoid(gl[:, d:2 * d]) * yb + _sigmoid(gl[:, 2 * d:3 * d]) * yc)
    return x + _dot(merged.astype(BF16), wout)


def _postmix_kernel(pt_ref, x_ref, cact_ref, ob_ref, mq_ref, mk_ref, mv_ref, g1_ref, wg_ref, bg_ref,
                    wpa_ref, wpb_ref, wpc_ref, wout_ref, q_ref, ck_hbm, x1_ref, sel_ref,
                    gate_scr, merged_scr, page_buf, page_sem, *, pages_per_step, n_blocks):
    step = pl.program_id(0) * pl.num_programs(1) + pl.program_id(1)
    n_steps = pl.num_programs(0) * pl.num_programs(1)
    slot = step % 2

    def page_copy(page, to_slot, r):
        return pltpu.make_async_copy(ck_hbm.at[page], page_buf.at[to_slot, r], page_sem.at[to_slot])

    def fetch(for_step, to_slot):
        for r in range(pages_per_step):
            page_copy(pt_ref[for_step * pages_per_step + r], to_slot, r).start()

    @pl.when(step == 0)
    def _():
        gate_scr[...] = jnp.zeros(gate_scr.shape, F32)
        fetch(0, 0)

    @pl.when(step + 1 < n_steps)
    def _():
        fetch(step + 1, 1 - slot)

    for r in range(pages_per_step):
        page_copy(0, slot, r).wait()
    page_refs = [page_buf.at[slot, r] for r in range(pages_per_step)]

    blocks_per_step = pages_per_step // PAGES_PER_BLOCK
    steps_per_sample = n_blocks // blocks_per_step
    share = step % steps_per_sample
    n_chunks = 4
    chunk = pages_per_step // n_chunks
    assert chunk % PAGES_PER_BLOCK == 0 and chunk * n_chunks == pages_per_step

    def gate_chunk(g, c):
        first = share * blocks_per_step + c * (chunk // PAGES_PER_BLOCK)
        return _gate_scores(g, first, q_ref, page_refs[c * chunk:(c + 1) * chunk])

    g = jnp.where(share == 0, 0.0, gate_scr[...])
    x = x_ref[...]
    d = x.shape[-1]
    hn = _rms_rows(x, g1_ref[...]).astype(BF16)
    oc = _mem_attend_rows(mq_ref[...], mk_ref[...], mv_ref[...]).astype(BF16)
    width = d // n_chunks
    for c in range(n_chunks):
        g = gate_chunk(g, c)
        tie = jnp.concatenate([_exact_zero(g)[0:1, :]] * (width // LANES), axis=1)
        merged = None
        for br, (src, w_ref) in enumerate(((cact_ref[...], wpa_ref), (ob_ref[...], wpb_ref), (oc, wpc_ref))):
            cols = slice(d * br + width * c, d * br + width * (c + 1))
            gl = _dot(hn, wg_ref[:, cols]) + (bg_ref[:, cols] + tie)
            term = _sigmoid(gl) * _dot(src, w_ref[:, width * c:width * (c + 1)])
            merged = term if merged is None else merged + term
        merged_scr[:, width * c:width * (c + 1)] = merged.astype(BF16)
    x1_ref[...] = x + _dot(merged_scr[...], wout_ref[...])
    gate_scr[...] = g
    _gate_select(share == steps_per_sample - 1, gate_scr, sel_ref, n_blocks)


def _ffn_kernel(x1_ref, g2_ref, wup_ref, wdw_ref, bdw_ref, wdown_ref, y_ref, tail_ref, upbuf, *, chunk):
    t = pl.program_id(1)
    nt = pl.num_programs(1)
    dff = wdown_ref.shape[0]

    @pl.when(t == 0)
    def _():
        upbuf[0:SUBLANES, :] = jnp.zeros((SUBLANES, 2 * dff), F32)

    x1 = x1_ref[...]
    hn = _rms_rows(x1, g2_ref[...]).astype(BF16)

    for c in range(0, 2 * dff, chunk):
        upbuf[SUBLANES:SUBLANES + TM, c:c + chunk] = _dot(hn, wup_ref[:, c:c + chunk])

    def conv(c):
        out = bdw_ref[:, c:c + chunk]
        for k in range(FFN_CONV_WIDTH):
            r0 = SUBLANES - (FFN_CONV_WIDTH - 1) + k
            out = out + wdw_ref[k:k + 1, c:c + chunk] * upbuf[r0:r0 + TM, c:c + chunk]
        return out

    y = x1
    for c in range(0, dff, chunk):
        a = conv(c)
        b = conv(dff + c)
        act = (a * _sigmoid(a) * b).astype(BF16)
        y = y + _dot(act, wdown_ref[c:c + chunk, :])
    y_ref[...] = y

    @pl.when(t == nt - 1)
    def _():
        tail_ref[0] = upbuf[TM:TM + SUBLANES, :]

    upbuf[0:SUBLANES, :] = upbuf[TM:TM + SUBLANES, :]


def _premix_s_kernel(x_ref, g1_ref, w_ref, bd64_ref, bd128_ref, qg_ref, kg_ref, mqg_ref,
                     st_ref, wdw_ref, bdw_ref, lng_ref, lnb_ref,
                     u_ref, q_ref, k_ref, v_ref, mq_ref, gl_ref, cact_ref):
    c = CONV_CH
    hn = _rms_rows(x_ref[...], g1_ref[...]).astype(BF16)
    a = _dot(hn, w_ref[:, 0:c])
    g = _dot(hn, w_ref[:, c:2 * c])
    u = a * _sigmoid(g)
    u_ref[...] = u
    acc = bdw_ref[...] + wdw_ref[CONV_WIDTH - 1:CONV_WIDTH, :] * u
    for k in range(CONV_WIDTH - 1):
        acc = acc + wdw_ref[k:k + 1, :] * st_ref[k]
    cact_ref[...] = _layernorm_silu(acc, lng_ref[...], lnb_ref[...]).astype(BF16)

    o = 2 * c
    q_ref[...] = _group_rms(_dot(hn, w_ref[:, o:o + ATTN_W]), bd64_ref[...], qg_ref[...])
    o += ATTN_W
    k_ref[...] = _group_rms(_dot(hn, w_ref[:, o:o + ATTN_W]), bd64_ref[...], kg_ref[...])
    o += ATTN_W
    v_ref[...] = _dot(hn, w_ref[:, o:o + ATTN_W])
    o += ATTN_W
    mq_ref[...] = _group_rms(_dot(hn, w_ref[:, o:o + MEM_W]), bd128_ref[...], mqg_ref[...])
    o += MEM_W
    gl_ref[...] = _dot(hn, w_ref[:, o:])


PAGES_PER_BLOCK = MOBA_BLOCK // LANES


def _gate_scores(g, first_block, q_ref, page_refs):
    head = lax.broadcasted_iota(jnp.int32, (N_HEADS, LANES), 0)
    lane = lax.broadcasted_iota(jnp.int32, (HEAD_DIM, LANES), 1)
    for h in range(N_HEADS):
        kmean = jnp.zeros((HEAD_DIM, LANES), F32)
        for r in range(0, len(page_refs), PAGES_PER_BLOCK):
            ksum = page_refs[r][h]
            for extra in range(1, PAGES_PER_BLOCK):
                ksum = ksum + page_refs[r + extra][h]
            col = jnp.sum(ksum, axis=1, keepdims=True) * (1.0 / MOBA_BLOCK)
            kmean = jnp.where(lane == first_block + r // PAGES_PER_BLOCK, col, kmean)
        prod = q_ref[0, h].astype(BF16).astype(F32) * kmean.astype(BF16).astype(F32)
        g = g + jnp.where(head == h, jnp.sum(prod, axis=0, keepdims=True), 0.0)
    return g


def _gate_select(last, gate_scr, sel_ref, n_blocks):
    @pl.when(last)
    def _():
        g = gate_scr[:, 0:n_blocks]
        bl = lax.broadcasted_iota(jnp.int32, (N_HEADS, n_blocks), 1)
        rank = jnp.zeros((N_HEADS, n_blocks), F32)
        for b in range(n_blocks):
            other = g[:, b:b + 1]
            rank = rank + jnp.where(bl > b, jnp.where(other >= g, 1.0, 0.0), jnp.where(other > g, 1.0, 0.0))
        lane_o = lax.broadcasted_iota(jnp.int32, (N_HEADS, LANES), 1)
        out = jnp.zeros((N_HEADS, LANES), F32)
        blf = bl.astype(F32)
        for r in range(MOBA_TOPK):
            idx = jnp.sum(jnp.where(rank == float(r), blf, 0.0), axis=-1, keepdims=True)
            out = jnp.where(lane_o == r, idx, out)
        sel_ref[0] = out.astype(jnp.int32)


def _decode_attn_kernel(page_ref, sel_ref, slope_ref, q_ref, kown_ref, vown_ref, ck_hbm, cv_hbm, o_ref,
                        k_buf, v_buf, sem, *, past_len):
    pages_per_block = MOBA_BLOCK // LANES
    tiles_per_head = MOBA_TOPK * pages_per_block
    n_tiles = N_HEADS * tiles_per_head
    n = pl.program_id(0)
    slot = n % 2

    def tile_copies(page, to_slot, tile):
        h = tile // tiles_per_head
        return (pltpu.make_async_copy(ck_hbm.at[page, h], k_buf.at[to_slot, tile], sem.at[0, to_slot]),
                pltpu.make_async_copy(cv_hbm.at[page, h], v_buf.at[to_slot, tile], sem.at[1, to_slot]))

    def fetch(sample, to_slot):
        for tile in range(n_tiles):
            for copy in tile_copies(page_ref[sample * n_tiles + tile], to_slot, tile):
                copy.start()

    @pl.when(n == 0)
    def _():
        fetch(0, 0)

    @pl.when(n + 1 < pl.num_programs(0))
    def _():
        fetch(n + 1, 1 - slot)

    for tile in range(n_tiles):
        for copy in tile_copies(0, slot, tile):
            copy.wait()
    k_refs = [k_buf.at[slot, tile] for tile in range(n_tiles)]
    v_refs = [v_buf.at[slot, tile] for tile in range(n_tiles)]

    lane = lax.broadcasted_iota(jnp.int32, (1, LANES), 1)
    lanef = lane.astype(F32)
    zeros = jnp.zeros((HEAD_DIM, tiles_per_head * LANES), BF16)
    scale = HEAD_DIM ** -0.5

    def head_tiles(tile_refs, h):
        t = jnp.concatenate([tile_refs[h * tiles_per_head + idx][...] for idx in range(tiles_per_head)], axis=1)
        t = t.astype(BF16)
        return jnp.concatenate([t, zeros] if h % 2 == 0 else [zeros, t], axis=0)

    def own_half(h):
        return (lane < HEAD_DIM) if h % 2 == 0 else (lane >= HEAD_DIM)

    def pair_lanes(ref, h):
        return ref[0][:, LANES * (h // 2):LANES * (h // 2 + 1)]

    qz, raw = [], []
    for h in range(N_HEADS):
        qz.append(jnp.where(own_half(h), pair_lanes(q_ref, h) * scale, 0.0).astype(BF16))
        raw.append(_dot(jnp.broadcast_to(qz[h], (SUBLANES, LANES)), head_tiles(k_refs, h))[0:1, :])
    probs = []
    for h in range(N_HEADS):
        dist = []
        for r in range(MOBA_TOPK):
            blk = sel_ref[(n * N_HEADS + h) * MOBA_TOPK + r]
            for half in range(pages_per_block):
                pos0 = (blk * MOBA_BLOCK + half * LANES).astype(F32)
                dist.append(float(past_len) - (pos0 + lanef))
        logits = raw[h] - slope_ref[h] * jnp.concatenate(dist, axis=1)
        kown = pair_lanes(kown_ref, h).astype(BF16).astype(F32)
        s_own = jnp.sum(qz[h].astype(F32) * kown, axis=-1, keepdims=True)
        m = jnp.maximum(s_own, jnp.max(logits, axis=-1, keepdims=True))
        p_own = jnp.exp(s_own - m)
        p = jnp.exp(logits - m)
        probs.append((p, p_own, p_own + jnp.sum(p, axis=-1, keepdims=True)))
    outs = []
    for h in range(N_HEADS):
        p, p_own, l = probs[h]
        p8 = jnp.broadcast_to(p.astype(BF16), (SUBLANES, p.shape[1]))
        vown = jnp.where(own_half(h), pair_lanes(vown_ref, h).astype(BF16).astype(F32), 0.0)
        acc = p_own.astype(BF16).astype(F32) * vown + _dot_nt(p8, head_tiles(v_refs, h))[0:1, :]
        outs.append(acc / l)
    o_ref[0] = jnp.concatenate([outs[h] + outs[h + 1] for h in range(0, N_HEADS, 2)], axis=1)


def _memattn_s_kernel(mq_ref, mk_ref, mv_ref, o_ref, *, group, n_mem):
    units = [(s, hh) for s in range(group) for hh in range(MEM_HEADS)]
    scores = []
    for s, hh in units:
        mq = jnp.broadcast_to(mq_ref[s][:, MEM_HEAD_DIM * hh:MEM_HEAD_DIM * (hh + 1)], (SUBLANES, MEM_HEAD_DIM))
        mk = mk_ref[s, pl.ds(hh, n_mem, stride=MEM_HEADS), :].astype(BF16)
        scores.append(_dot_nt(mq.astype(BF16), mk)[0:1, :] * (MEM_HEAD_DIM ** -0.5))
    probs = []
    for sc in scores:
        p = jnp.exp(sc - jnp.max(sc, axis=-1, keepdims=True))
        probs.append((p, jnp.sum(p, axis=-1, keepdims=True)))
    outs = {}
    for (s, hh), (p, l) in zip(units, probs):
        mv = mv_ref[s, pl.ds(hh, n_mem, stride=MEM_HEADS), :].astype(BF16)
        p8 = jnp.broadcast_to(p.astype(BF16), (SUBLANES, n_mem))
        outs[s, hh] = _dot(p8, mv)[0:1, :] / l
    for s in range(group):
        o_ref[s] = jnp.concatenate([outs[s, hh] for hh in range(MEM_HEADS)], axis=-1)


def _post_s_kernel(x_ref, cact_ref, ob_ref, oc_ref, gl_ref, bg_ref, wpa_ref, wpb_ref, wpc_ref, wout_ref,
                   g2_ref, wup_ref, st_ref, wdw_ref, bdw_ref, wdown_ref, y_ref, up_ref):
    x = x_ref[...]
    dff = wdown_ref.shape[0]
    ya = _dot(cact_ref[...], wpa_ref[...])
    yb = _dot(ob_ref[...].astype(BF16), wpb_ref[...])
    yc = _dot(oc_ref[...].astype(BF16), wpc_ref[...])
    x1 = _merge_out(x, gl_ref[...] + bg_ref[...], ya, yb, yc, wout_ref[...])
    up = _dot(_rms_rows(x1, g2_ref[...]).astype(BF16), wup_ref[...])
    up_ref[...] = up
    cv = bdw_ref[...] + wdw_ref[FFN_CONV_WIDTH - 1:FFN_CONV_WIDTH, :] * up
    for k in range(FFN_CONV_WIDTH - 1):
        cv = cv + wdw_ref[k:k + 1, :] * st_ref[k]
    a = cv[:, 0:dff]
    b = cv[:, dff:]
    act = (a * _sigmoid(a) * b).astype(BF16)
    y_ref[...] = x1 + _dot(act, wdown_ref[...])


def _resident(shape):
    nd = len(shape)
    return pl.BlockSpec(shape, lambda *_: (0,) * nd, pipeline_mode=pl.Buffered(1))


def _params(n_axes):
    return pltpu.CompilerParams(dimension_semantics=("arbitrary",) * n_axes, vmem_limit_bytes=VMEM_LIMIT)


def kernel(x_prompt, x_sample, mem_prompt, cache_k, cache_v, page_table, state_conv, state_ffn_conv, cache_mem_k, cache_mem_v, norm1_g, w_in, b_gate, w_dw_a, b_dw_a, ln_a_g, ln_a_b, w_proj_a, q_norm_g, k_norm_g, w_proj_b, mem_norm_g, w_mem_kv, mq_norm_g, mk_norm_g, w_proj_c, w_out, norm2_g, w_up, w_dw_f, b_dw_f, w_down):
    n, t, d = x_prompt.shape
    nd, td, _ = x_sample.shape
    n_mem = mem_prompt.shape[1]
    n_pool, page_size = cache_k.shape[:2]
    n_pages = page_table.shape[1]
    past_len = n_pages * page_size
    dff = w_down.shape[0]
    nt = t // TM
    n_mix = 2 * CONV_CH + 3 * ATTN_W + MEM_W
    assert t % TM == 0 and nt <= GROUP and td == 1
    assert past_len % MOBA_BLOCK == 0 and MOBA_BLOCK == 2 * page_size and page_size == LANES
    n_blocks = past_len // MOBA_BLOCK

    row = lambda v: v.reshape(1, -1).astype(F32)
    w_in_b = w_in.astype(BF16)
    w_mix, w_gate = w_in_b[:, :n_mix], w_in_b[:, n_mix:]
    wpa, wpb, wpc = w_proj_a.astype(BF16), w_proj_b.astype(BF16), w_proj_c.astype(BF16)
    wout, wup, wdown = w_out.astype(BF16), w_up.astype(BF16), w_down.astype(BF16)
    g1, g2 = row(norm1_g), row(norm2_g)
    qg = row(jnp.tile(q_norm_g, N_HEADS))
    kg = row(jnp.tile(k_norm_g, N_HEADS))
    mqg = row(jnp.tile(mq_norm_g, MEM_HEADS))
    mkg = row(jnp.tile(mk_norm_g, MEM_HEADS))
    bdw_a, lng, lnb, bg, bdw_f = row(b_dw_a), row(ln_a_g), row(ln_a_b), row(b_gate), row(b_dw_f)
    grp = np.arange(ATTN_W)
    bd64 = jnp.asarray((grp[:, None] // HEAD_DIM == grp[None, :] // HEAD_DIM) / HEAD_DIM, BF16)
    bd128 = jnp.asarray((grp[:, None] // MEM_HEAD_DIM == grp[None, :] // MEM_HEAD_DIM) / MEM_HEAD_DIM, BF16)
    slopes = 2.0 ** (-8.0 * np.arange(1, N_HEADS + 1) / N_HEADS)
    c_parts, rest = [], jnp.asarray(slopes * LOG2E, F32)
    for _ in range(X_PARTS):
        part = rest.astype(BF16).astype(F32)
        c_parts.append(part)
        rest = rest - part
    col = jnp.arange(N_HEADS * LANES, dtype=jnp.int32)[None, None, :] % LANES
    tile_f = jnp.arange(nt, dtype=F32)[:, None, None]
    row_f = jnp.arange(TM, dtype=F32)[None, :, None]
    zero3 = jnp.zeros((nt, TM, N_HEADS * LANES), F32)
    q_extra = zero3
    k_extra = jnp.where(col == X_SEL + tile_f.astype(jnp.int32), MASK_NEG, zero3)
    for p, c_head in enumerate(c_parts):
        cp = jnp.repeat(c_head, LANES)[None, None, :]
        q_extra = jnp.where(col == X_RQ + p, row_f + zero3, q_extra)
        k_extra = jnp.where(col == X_RQ + p, -cp + zero3, k_extra)
        q_extra = jnp.where(col == X_RK + p, cp + zero3, q_extra)
        k_extra = jnp.where(col == X_RK + p, row_f + zero3, k_extra)
        q_extra = jnp.where(col == X_TQ + p, tile_f + zero3, q_extra)
        k_extra = jnp.where(col == X_TQ + p, -cp * MOBA_BLOCK + zero3, k_extra)
        q_extra = jnp.where(col == X_TK + p, cp + zero3, q_extra)
        k_extra = jnp.where(col == X_TK + p, MOBA_BLOCK * tile_f + zero3, k_extra)
    q_extra, k_extra = q_extra.astype(BF16), k_extra.astype(BF16)

    xp = x_prompt.reshape(n * t, d)
    tile = lambda w: pl.BlockSpec((TM, w), lambda b, i, *_: (b * nt + i, 0))
    aw = N_HEADS * LANES
    vw = N_HEADS * V_ROWS
    blk3 = lambda r, c: pl.BlockSpec((1, r, c), lambda b, i: (b * nt + i, 0, 0))
    k_p, v_p, km, qt, ka, vt, mq, cact, utail = pl.pallas_call(
        _premix_kernel,
        grid=(n, nt),
        in_specs=[tile(d), _resident((1, d)), _resident((d, n_mix)), _resident((ATTN_W, ATTN_W)),
                  _resident((MEM_W, MEM_W)), _resident((1, ATTN_W)), _resident((1, ATTN_W)),
                  _resident((1, MEM_W)), _resident((CONV_WIDTH, CONV_CH)), _resident((1, CONV_CH)),
                  _resident((1, CONV_CH)), _resident((1, CONV_CH)),
                  pl.BlockSpec((1, TM, aw), lambda b, i: (i, 0, 0)),
                  pl.BlockSpec((1, TM, aw), lambda b, i: (i, 0, 0))],
        out_specs=[pl.BlockSpec((ATTN_W, TM), lambda b, i: (b, i)), pl.BlockSpec((ATTN_W, TM), lambda b, i: (b, i)),
                   pl.BlockSpec((1, 1, ATTN_W), lambda b, i: (b * nt + i, 0, 0)),
                   pl.BlockSpec((aw, TM), lambda b, i: (b, i)), blk3(TM, aw), blk3(vw, TM),
                   tile(MEM_W), tile(CONV_CH),
                   pl.BlockSpec((1, CONV_HALO, CONV_CH), lambda b, i: (b, 0, 0))],
        out_shape=[jax.ShapeDtypeStruct((n * ATTN_W, t), F32), jax.ShapeDtypeStruct((n * ATTN_W, t), F32),
                   jax.ShapeDtypeStruct((n * nt, 1, ATTN_W), F32),
                   jax.ShapeDtypeStruct((n * aw, t), BF16), jax.ShapeDtypeStruct((n * nt, TM, aw), BF16),
                   jax.ShapeDtypeStruct((n * nt, vw, TM), BF16), jax.ShapeDtypeStruct((n * t, MEM_W), BF16),
                   jax.ShapeDtypeStruct((n * t, CONV_CH), BF16),
                   jax.ShapeDtypeStruct((n, CONV_HALO, CONV_CH), F32)],
        scratch_shapes=[pltpu.VMEM((CONV_HALO + TM + SUBLANES, CONV_CH), F32)],
        compiler_params=_params(2),
        name="premix",
    )(xp, g1, w_mix, bd64, bd128, qg, kg, mqg, w_dw_a, bdw_a, lng, lnb, q_extra, k_extra)

    km4 = km.reshape(n, nt, N_HEADS, HEAD_DIM).transpose(0, 2, 1, 3)
    km4 = jnp.pad(km4, ((0, 0), (0, 0), (0, GROUP - nt), (0, LANES - HEAD_DIM)))
    kmt = (km4[:, :, :, None, :] * jnp.eye(N_HEADS, dtype=F32)[None, :, None, :, None])
    kmt = kmt.reshape(n, N_HEADS * GROUP, aw).astype(BF16)

    seq3 = lambda r, c: pl.BlockSpec((nt, r, c), lambda b, i: (b, 0, 0))
    ob = pl.pallas_call(
        _moba_kernel,
        grid=(n, nt),
        in_specs=[pl.BlockSpec((aw, TM), lambda b, i: (b, i)), seq3(TM, aw), seq3(vw, TM),
                  pl.BlockSpec((1, N_HEADS * GROUP, aw), lambda b, i: (b, 0, 0))],
        out_specs=tile(ATTN_W),
        out_shape=jax.ShapeDtypeStruct((n * t, ATTN_W), BF16),
        scratch_shapes=[pltpu.VMEM((N_HEADS, LANES, TM), BF16), pltpu.VMEM((N_HEADS, 1, TM), F32),
                        pltpu.VMEM((N_HEADS, V_ROWS, TM), F32)]
        + [pltpu.VMEM((TM, TM), F32)] * N_HEADS + [pltpu.VMEM((1, TM), F32)] * N_HEADS,
        compiler_params=_params(2),
        name="moba",
    )(qt, ka, vt, kmt)

    mem = mem_prompt.reshape(n * n_mem, d)
    mtile = lambda w: pl.BlockSpec((n_mem, w), lambda b: (b, 0))
    mk_p, mv_p, mkb, mvb = pl.pallas_call(
        _memkv_kernel,
        grid=(n,),
        in_specs=[mtile(d), _resident((1, d)), _resident((d, 2 * MEM_W)), _resident((MEM_W, MEM_W)),
                  _resident((1, MEM_W))],
        out_specs=[mtile(MEM_W)] * 4,
        out_shape=[jax.ShapeDtypeStruct((n * n_mem, MEM_W), F32)] * 2
        + [jax.ShapeDtypeStruct((n * n_mem, MEM_W), BF16)] * 2,
        compiler_params=_params(1),
        name="memkv",
    )(mem, row(mem_norm_g), w_mem_kv.astype(BF16), bd128, mkg)

    xs = x_sample.reshape(nd, d)
    st_conv = state_conv.transpose(1, 0, 2)
    st_ffn = state_ffn_conv.transpose(1, 0, 2)
    vm = pltpu.CompilerParams(vmem_limit_bytes=VMEM_LIMIT)
    u_s, q_s, k_s, v_s, mq_s, gl_s, cact_s = pl.pallas_call(
        _premix_s_kernel,
        out_shape=[jax.ShapeDtypeStruct((nd, CONV_CH), F32), jax.ShapeDtypeStruct((nd, ATTN_W), F32),
                   jax.ShapeDtypeStruct((nd, ATTN_W), F32), jax.ShapeDtypeStruct((nd, ATTN_W), F32),
                   jax.ShapeDtypeStruct((nd, MEM_W), F32), jax.ShapeDtypeStruct((nd, 3 * d), F32),
                   jax.ShapeDtypeStruct((nd, CONV_CH), BF16)],
        compiler_params=vm,
        name="premix_s",
    )(xs, g1, w_in_b, bd64, bd128, qg, kg, mqg, st_conv, w_dw_a, bdw_a, lng, lnb)

    ck = cache_k.transpose(0, 2, 3, 1)
    cv = cache_v.transpose(0, 2, 3, 1)
    pt_flat = page_table.reshape(-1).astype(jnp.int32)
    q_cols = jnp.broadcast_to(q_s.reshape(nd, N_HEADS, HEAD_DIM, 1), (nd, N_HEADS, HEAD_DIM, page_size))
    assert (n * nt) % nd == 0 and n_pages % ((n * nt) // nd) == 0 and n_blocks <= LANES
    steps_per_sample = (n * nt) // nd
    pages_per_step = n_pages // steps_per_sample
    assert pages_per_step % (MOBA_BLOCK // page_size) == 0

    memb = pl.BlockSpec((n_mem, MEM_W), lambda b, i, *_: (b, 0))
    x1, sel = pl.pallas_call(
        functools.partial(_postmix_kernel, pages_per_step=pages_per_step, n_blocks=n_blocks),
        grid_spec=pltpu.PrefetchScalarGridSpec(
            num_scalar_prefetch=1,
            grid=(n, nt),
            in_specs=[tile(d), tile(CONV_CH), tile(ATTN_W), tile(MEM_W), memb, memb, _resident((1, d)),
                      _resident((d, 3 * d)), _resident((1, 3 * d)), _resident((CONV_CH, d)),
                      _resident((ATTN_W, d)), _resident((MEM_W, d)), _resident((d, d)),
                      pl.BlockSpec((1, N_HEADS, HEAD_DIM, page_size),
                                   lambda b, i, pt: ((b * nt + i) // steps_per_sample, 0, 0, 0)),
                      pl.BlockSpec(memory_space=pl.ANY)],
            out_specs=[tile(d), pl.BlockSpec((1, N_HEADS, LANES),
                                             lambda b, i, pt: ((b * nt + i) // steps_per_sample, 0, 0))],
            scratch_shapes=[pltpu.VMEM((N_HEADS, LANES), F32), pltpu.VMEM((TM, d), BF16),
                            pltpu.VMEM((2, pages_per_step, N_HEADS, HEAD_DIM, page_size), F32),
                            pltpu.SemaphoreType.DMA((2,))]),
        out_shape=[jax.ShapeDtypeStruct((n * t, d), F32), jax.ShapeDtypeStruct((nd, N_HEADS, LANES), jnp.int32)],
        compiler_params=_params(2),
        name="postmix",
    )(pt_flat, xp, cact, ob, mq, mkb, mvb, g1, w_gate, bg, wpa, wpb, wpc, wout, q_cols, ck)
    sel_flat = sel[:, :, :MOBA_TOPK].reshape(-1)

    chunk = dff // 2
    assert chunk % LANES == 0
    y_p, ftail = pl.pallas_call(
        functools.partial(_ffn_kernel, chunk=chunk),
        grid=(n, nt),
        in_specs=[tile(d), _resident((1, d)), _resident((d, 2 * dff)), _resident((FFN_CONV_WIDTH, 2 * dff)),
                  _resident((1, 2 * dff)), _resident((dff, d))],
        out_specs=[tile(d), pl.BlockSpec((1, SUBLANES, 2 * dff), lambda b, i: (b, 0, 0))],
        out_shape=[jax.ShapeDtypeStruct((n * t, d), F32), jax.ShapeDtypeStruct((n, SUBLANES, 2 * dff), F32)],
        scratch_shapes=[pltpu.VMEM((SUBLANES + TM, 2 * dff), F32)],
        compiler_params=_params(2),
        name="ffn",
    )(x1, g2, wup, w_dw_f, bdw_f, wdown)

    row_spec = pl.BlockSpec((1, 1, ATTN_W), lambda b, *_: (b, 0, 0))
    rows3 = lambda a: a.reshape(nd, 1, -1)

    halves = jnp.arange(PAGES_PER_BLOCK, dtype=jnp.int32)
    sel_pages = PAGES_PER_BLOCK * sel[:, :, :MOBA_TOPK, None] + halves
    page_ids = jnp.take_along_axis(page_table.astype(jnp.int32), sel_pages.reshape(nd, -1), axis=1).reshape(-1)
    tiles_per_sample = N_HEADS * MOBA_TOPK * PAGES_PER_BLOCK
    tile_buf = pltpu.VMEM((2, tiles_per_sample, HEAD_DIM, page_size), F32)
    ob_s = pl.pallas_call(
        functools.partial(_decode_attn_kernel, past_len=past_len),
        grid_spec=pltpu.PrefetchScalarGridSpec(
            num_scalar_prefetch=2,
            grid=(nd,),
            in_specs=[pl.BlockSpec(memory_space=pltpu.SMEM), row_spec, row_spec, row_spec,
                      pl.BlockSpec(memory_space=pl.ANY), pl.BlockSpec(memory_space=pl.ANY)],
            out_specs=row_spec,
            scratch_shapes=[tile_buf, tile_buf, pltpu.SemaphoreType.DMA((2, 2))]),
        out_shape=jax.ShapeDtypeStruct((nd, 1, ATTN_W), F32),
        compiler_params=_params(1),
        name="decode_attn",
    )(page_ids, sel_flat, jnp.asarray(slopes, F32), rows3(q_s), rows3(k_s), rows3(v_s), ck, cv)
    ob_s = ob_s.reshape(nd, ATTN_W)

    group = SUBLANES
    assert nd % group == 0
    cm_spec = pl.BlockSpec((group, n_mem * MEM_HEADS, MEM_HEAD_DIM), lambda b: (b, 0, 0))
    mq_spec = pl.BlockSpec((group, 1, MEM_W), lambda b: (b, 0, 0))
    oc_s = pl.pallas_call(
        functools.partial(_memattn_s_kernel, group=group, n_mem=n_mem),
        grid=(nd // group,),
        in_specs=[mq_spec, cm_spec, cm_spec],
        out_specs=mq_spec,
        out_shape=jax.ShapeDtypeStruct((nd, 1, MEM_W), F32),
        compiler_params=_params(1),
        name="memattn_s",
    )(rows3(mq_s), cache_mem_k.reshape(nd, n_mem * MEM_HEADS, MEM_HEAD_DIM),
      cache_mem_v.reshape(nd, n_mem * MEM_HEADS, MEM_HEAD_DIM))
    oc_s = oc_s.reshape(nd, MEM_W)

    y_s, up_s = pl.pallas_call(
        _post_s_kernel,
        out_shape=[jax.ShapeDtypeStruct((nd, d), F32), jax.ShapeDtypeStruct((nd, 2 * dff), F32)],
        compiler_params=vm,
        name="post_s",
    )(xs, cact_s, ob_s, oc_s, gl_s, bg, wpa, wpb, wpc, wout, g2, wup, st_ffn, w_dw_f, bdw_f, wdown)

    heads = lambda a, b, s: a.reshape(b, s, N_HEADS, HEAD_DIM)
    from_t = lambda a: a.reshape(n, N_HEADS, HEAD_DIM, t).transpose(0, 3, 1, 2)
    conv_p = utail[:, CONV_HALO - (CONV_WIDTH - 1):, :]
    conv_s = jnp.concatenate([state_conv[:, 1:, :], u_s[:, None, :]], axis=1)
    ffn_p = ftail[:, SUBLANES - (FFN_CONV_WIDTH - 1):, :]
    ffn_s = jnp.concatenate([state_ffn_conv[:, 1:, :], up_s[:, None, :]], axis=1)
    return (y_p.reshape(n, t, d), y_s.reshape(nd, td, d),
            from_t(k_p), from_t(v_p), heads(k_s, nd, td), heads(v_s, nd, td),
            conv_p, conv_s, ffn_p, ffn_s,
            mk_p.reshape(n, n_mem, MEM_HEADS, MEM_HEAD_DIM), mv_p.reshape(n, n_mem, MEM_HEADS, MEM_HEAD_DIM))
```

```python
import functools

import numpy as np
import jax
import jax.numpy as jnp
from jax import lax
from jax.experimental import pallas as pl
from jax.experimental.pallas import tpu as pltpu

F32 = jnp.float32
BF16 = jnp.bfloat16

EPS = 1e-6
CONV_CH = 512
CONV_WIDTH = 31
N_HEADS = 8
HEAD_DIM = 64
ATTN_W = N_HEADS * HEAD_DIM
MOBA_BLOCK = 256
MOBA_TOPK = 3
MEM_HEADS = 4
MEM_HEAD_DIM = 128
MEM_W = MEM_HEADS * MEM_HEAD_DIM
FFN_CONV_WIDTH = 3
LANES = 128
SUBLANES = 8
TM = MOBA_BLOCK
CONV_HALO = 32
MASK_NEG = -float(2 ** 30)
GROUP = 16
V_ROWS = HEAD_DIM + 16
BLOCKS_PER_TRIP = 4
FFN_ROWS = 512
VMEM_LIMIT = 56 * 1024 * 1024

X_SEL = HEAD_DIM
X_PARTS = 3
X_RQ = HEAD_DIM + GROUP
X_RK = X_RQ + X_PARTS
X_TQ = X_RK + X_PARTS
X_TK = X_TQ + X_PARTS
LOG2E = 1.4426950408889634


def _dot(a, b):
    return jnp.dot(a, b, preferred_element_type=F32)


def _dot_nt(a, b):
    return lax.dot_general(a, b, (((1,), (1,)), ((), ())), preferred_element_type=F32)


def _rms_rows(x, g):
    return x * lax.rsqrt(jnp.mean(x * x, axis=-1, keepdims=True) + EPS) * g


def _group_rms(z, bd, g):
    sq = z * z
    hi = sq.astype(BF16)
    lo = (sq - hi.astype(F32)).astype(BF16)
    ms = _dot(hi, bd) + _dot(lo, bd)
    return z * lax.rsqrt(ms + EPS) * g


def _exact_zero(v):
    bits = pltpu.bitcast(v, jnp.uint32)
    half = jnp.uint32(16)
    return pltpu.bitcast(lax.shift_right_logical(lax.shift_right_logical(bits, half), half), F32)


def _sigmoid(x):
    return 1.0 / (1.0 + jnp.exp(-x))


def _layernorm_silu(c, g, b):
    mu = jnp.mean(c, axis=-1, keepdims=True)
    xc = c - mu
    var = jnp.mean(xc * xc, axis=-1, keepdims=True)
    y = xc * lax.rsqrt(var + EPS) * g + b
    return y * _sigmoid(y)


def _premix_kernel(x_ref, g1_ref, w_ref, bd64_ref, bd128_ref, qg_ref, kg_ref, mqg_ref,
                   wdw_ref, bdw_ref, lng_ref, lnb_ref, qx_ref, kx_ref,
                   k_ref, v_ref, km_ref, qt_ref, ka_ref, vt_ref, mq_ref, cact_ref, utail_ref,
                   ubuf):
    t = pl.program_id(1)
    nt = pl.num_programs(1)
    c = CONV_CH

    @pl.when(t == 0)
    def _():
        ubuf[0:CONV_HALO, :] = jnp.zeros((CONV_HALO, c), F32)
        ubuf[CONV_HALO + TM:, :] = jnp.zeros((SUBLANES, c), F32)

    hn = _rms_rows(x_ref[...], g1_ref[...]).astype(BF16)

    a = _dot(hn, w_ref[:, 0:c])
    g = _dot(hn, w_ref[:, c:2 * c])
    u = a * _sigmoid(g)
    ubuf[CONV_HALO:CONV_HALO + TM, :] = u
    o = 2 * c
    zq = _dot(hn, w_ref[:, o:o + ATTN_W])
    zk = _dot(hn, w_ref[:, o + ATTN_W:o + 2 * ATTN_W])
    zv = _dot(hn, w_ref[:, o + 2 * ATTN_W:o + 3 * ATTN_W])
    zm = _dot(hn, w_ref[:, o + 3 * ATTN_W:o + 3 * ATTN_W + MEM_W])
    acc = jnp.broadcast_to(bdw_ref[...], (TM, c))
    base = CONV_HALO - (CONV_WIDTH - 1)
    span = TM + 2 * SUBLANES
    for b in range(SUBLANES):
        part = None
        for k in range(b, CONV_WIDTH, SUBLANES):
            term = wdw_ref[k:k + 1, :] * ubuf[k - b:k - b + span, :]
            part = term if part is None else part + term
        acc = acc + part[base + b:base + b + TM, :]
    ubuf[0:CONV_HALO, :] = ubuf[TM:TM + CONV_HALO, :]
    cact_ref[...] = _layernorm_silu(acc, lng_ref[...], lnb_ref[...]).astype(BF16)

    qn = _group_rms(zq, bd64_ref[...], qg_ref[...]) * (HEAD_DIM ** -0.5)
    kn = _group_rms(zk, bd64_ref[...], kg_ref[...])
    mq = _group_rms(zm, bd128_ref[...], mqg_ref[...])

    ones_rows = jnp.where(lax.broadcasted_iota(jnp.int32, (V_ROWS - HEAD_DIM, TM), 0) == 0, 1.0, 0.0).astype(BF16)
    for cb in range(ATTN_W // LANES):
        k_ref[LANES * cb:LANES * (cb + 1), :] = kn[:, LANES * cb:LANES * (cb + 1)].T
        vt = zv[:, LANES * cb:LANES * (cb + 1)].T
        v_ref[LANES * cb:LANES * (cb + 1), :] = vt
        for sub in range(2):
            r0 = V_ROWS * (2 * cb + sub)
            vt_ref[0, r0:r0 + HEAD_DIM, :] = vt[HEAD_DIM * sub:HEAD_DIM * (sub + 1), :].astype(BF16)
            vt_ref[0, r0 + HEAD_DIM:r0 + V_ROWS, :] = ones_rows
    mq_ref[...] = mq.astype(BF16)
    km_ref[0] = jnp.mean(kn, axis=0, keepdims=True)

    lane = lax.broadcasted_iota(jnp.int32, (TM, LANES), 1)
    for hp in range(N_HEADS // 2):
        xq = qn[:, LANES * hp:LANES * (hp + 1)]
        xk = kn[:, LANES * hp:LANES * (hp + 1)] * LOG2E
        for sub in range(2):
            h = 2 * hp + sub
            if sub == 1:
                xq = pltpu.roll(xq, HEAD_DIM, 1)
                xk = pltpu.roll(xk, HEAD_DIM, 1)
            eq = qx_ref[0, :, LANES * h:LANES * (h + 1)].astype(F32)
            ek = kx_ref[0, :, LANES * h:LANES * (h + 1)].astype(F32)
            qt_ref[LANES * h:LANES * (h + 1), :] = jnp.where(lane < HEAD_DIM, xq, eq).T.astype(BF16)
            ka_ref[0, :, LANES * h:LANES * (h + 1)] = jnp.where(lane < HEAD_DIM, xk, ek).astype(BF16)

    @pl.when(t == nt - 1)
    def _():
        utail_ref[0] = ubuf[0:CONV_HALO, :]


def _moba_kernel(qt_ref, ka_ref, vt_ref, kmt_ref, o_ref, qh_scr, m_scr, acc_scr, *stage):
    i = pl.program_id(1)
    s_scr, mx_scr = stage[:N_HEADS], stage[N_HEADS:]
    key = lax.broadcasted_iota(jnp.int32, (TM, TM), 0)
    qry = lax.broadcasted_iota(jnp.int32, (TM, TM), 1)
    causal = key <= qry

    def scores(j, h, diagonal):
        q = qt_ref[LANES * h:LANES * (h + 1), :] if diagonal else qh_scr[h]
        s = _dot(ka_ref[j, :, LANES * h:LANES * (h + 1)], q)
        if diagonal:
            s = jnp.where(causal, s, -jnp.inf)
        s_scr[h][...] = s
        mx_scr[h][...] = jnp.max(s, axis=0, keepdims=True)

    for h in range(N_HEADS):
        scores(i, h, True)

    grow = lax.broadcasted_iota(jnp.int32, (N_HEADS * GROUP, TM), 0)
    blk = grow & (GROUP - 1)
    past = blk < i

    gate = _dot(kmt_ref[0], qt_ref[...])
    g = jnp.where(past, gate, -jnp.inf)
    rank = jnp.zeros(g.shape, F32)
    for s in range(1, GROUP):
        up = pltpu.roll(g, N_HEADS * GROUP - s, 0)
        down = pltpu.roll(g, GROUP - s, 0)
        wrapped = blk >= GROUP - s
        partner = jnp.where(wrapped, down, up)
        beats = jnp.where(wrapped, jnp.where(partner >= g, 1.0, 0.0), jnp.where(partner > g, 1.0, 0.0))
        rank = rank + beats
    nsel = jnp.where(past, jnp.where(rank >= MOBA_TOPK, 1.0, 0.0), 0.0).astype(BF16)
    for h in range(N_HEADS):
        r0 = LANES * h
        qh_scr[h, 0:X_SEL, :] = qt_ref[r0:r0 + X_SEL, :]
        qh_scr[h, X_SEL:X_RQ, :] = nsel[GROUP * h:GROUP * (h + 1), :]
        qh_scr[h, X_RQ:LANES, :] = qt_ref[r0 + X_RQ:r0 + LANES, :]

    def accumulate(j, h):
        m_old = m_scr[h]
        m_new = jnp.maximum(m_old, mx_scr[h][...])
        alpha = jnp.exp2(m_old - m_new)
        p = jnp.exp2(s_scr[h][...] - m_new).astype(BF16)
        acc_scr[h] = alpha * acc_scr[h] + _dot(vt_ref[j, V_ROWS * h:V_ROWS * (h + 1), :], p)
        m_scr[h] = m_new

    m_scr[...] = jnp.full(m_scr.shape, -jnp.inf, F32)
    acc_scr[...] = jnp.zeros(acc_scr.shape, F32)

    def step(prev, j):
        for h in range(N_HEADS):
            accumulate(prev, h)
            scores(j, h, False)

    def steps(first, count):
        step(jnp.where(first == 0, i, first - 1), first)
        for extra in range(1, count):
            step(first + extra - 1, first + extra)

    def body(trip, carry):
        steps(trip * BLOCKS_PER_TRIP, BLOCKS_PER_TRIP)
        return carry

    lax.fori_loop(0, i // BLOCKS_PER_TRIP, body, 0)
    size = BLOCKS_PER_TRIP // 2
    while size:
        done = (i // (2 * size)) * (2 * size)

        @pl.when((i // size) % 2 == 1)
        def _(done=done, size=size):
            steps(done, size)

        size //= 2
    last = jnp.where(i == 0, i, i - 1)
    for h in range(N_HEADS):
        accumulate(last, h)
    outs = []
    for h in range(N_HEADS):
        acc = acc_scr[h]
        outs.append(acc[0:HEAD_DIM, :] / acc[HEAD_DIM:HEAD_DIM + 1, :])
    o_ref[...] = jnp.concatenate(outs, axis=0).T.astype(BF16)


def _memkv_kernel(mem_ref, g_ref, w_ref, bd128_ref, mkg_ref, mk_ref, mv_ref, mkb_ref, mvb_ref):
    hn = _rms_rows(mem_ref[...], g_ref[...]).astype(BF16)
    mk = _group_rms(_dot(hn, w_ref[:, 0:MEM_W]), bd128_ref[...], mkg_ref[...])
    mv = _dot(hn, w_ref[:, MEM_W:2 * MEM_W])
    mk_ref[...] = mk
    mv_ref[...] = mv
    mkb_ref[...] = mk.astype(BF16)
    mvb_ref[...] = mv.astype(BF16)


def _mem_attend_rows(mq, mk, mv):
    heads = [slice(MEM_HEAD_DIM * hh, MEM_HEAD_DIM * (hh + 1)) for hh in range(MEM_HEADS)]
    scores = [_dot_nt(mq[:, sl], mk[:, sl]) * (MEM_HEAD_DIM ** -0.5) for sl in heads]
    probs = []
    for s in scores:
        p = jnp.exp(s - jnp.max(s, axis=-1, keepdims=True))
        probs.append((p.astype(BF16), jnp.sum(p, axis=-1, keepdims=True)))
    return jnp.concatenate([_dot(p, mv[:, sl]) / l for (p, l), sl in zip(probs, heads)], axis=-1)


def _merge_out(x, gl, ya, yb, yc, wout):
    d = x.shape[-1]
    merged = (_sigmoid(gl[:, 0:d]) * ya + _sigmoid(gl[:, d:2 * d]) * yb + _sigmoid(gl[:, 2 * d:3 * d]) * yc)
    return x + _dot(merged.astype(BF16), wout)


def _postmix_kernel(pt_ref, x_ref, cact_ref, ob_ref, mq_ref, mk_ref, mv_ref, g1_ref, wg_ref, bg_ref,
                    wpa_ref, wpb_ref, wpc_ref, wout_ref, q_ref, ck_hbm, x1_ref, sel_ref,
                    gate_scr, merged_scr, page_buf, page_sem, *, pages_per_step, n_blocks):
    step = pl.program_id(0) * pl.num_programs(1) + pl.program_id(1)
    n_steps = pl.num_programs(0) * pl.num_programs(1)
    slot = step % 2

    def page_copy(page, to_slot, r):
        return pltpu.make_async_copy(ck_hbm.at[page], page_buf.at[to_slot, r], page_sem.at[to_slot])

    def fetch(for_step, to_slot):
        for r in range(pages_per_step):
            page_copy(pt_ref[for_step * pages_per_step + r], to_slot, r).start()

    @pl.when(step == 0)
    def _():
        gate_scr[...] = jnp.zeros(gate_scr.shape, F32)
        fetch(0, 0)

    @pl.when(step + 1 < n_steps)
    def _():
        fetch(step + 1, 1 - slot)

    for r in range(pages_per_step):
        page_copy(0, slot, r).wait()
    page_refs = [page_buf.at[slot, r] for r in range(pages_per_step)]

    blocks_per_step = pages_per_step // PAGES_PER_BLOCK
    steps_per_sample = n_blocks // blocks_per_step
    share = step % steps_per_sample
    n_chunks = 4
    chunk = pages_per_step // n_chunks
    assert chunk % PAGES_PER_BLOCK == 0 and chunk * n_chunks == pages_per_step

    def gate_chunk(g, c):
        first = share * blocks_per_step + c * (chunk // PAGES_PER_BLOCK)
        return _gate_scores(g, first, q_ref, page_refs[c * chunk:(c + 1) * chunk])

    g = jnp.where(share == 0, 0.0, gate_scr[...])
    x = x_ref[...]
    d = x.shape[-1]
    hn = _rms_rows(x, g1_ref[...]).astype(BF16)
    oc = _mem_attend_rows(mq_ref[...], mk_ref[...], mv_ref[...]).astype(BF16)
    width = d // n_chunks
    for c in range(n_chunks):
        g = gate_chunk(g, c)
        tie = jnp.concatenate([_exact_zero(g)[0:1, :]] * (width // LANES), axis=1)
        sources = ((cact_ref[...], wpa_ref), (ob_ref[...], wpb_ref), (oc, wpc_ref))
        ys = [_dot(src, w_ref[:, width * c:width * (c + 1)]) for src, w_ref in sources]
        merged = None
        for br in range(len(sources)):
            cols = slice(d * br + width * c, d * br + width * (c + 1))
            gl = _dot(hn, wg_ref[:, cols]) + (bg_ref[:, cols] + tie)
            term = _sigmoid(gl) * ys[br]
            merged = term if merged is None else merged + term
        merged_scr[:, width * c:width * (c + 1)] = merged.astype(BF16)
    x1_ref[...] = x + _dot(merged_scr[...], wout_ref[...])
    gate_scr[...] = g
    _gate_select(share == steps_per_sample - 1, gate_scr, sel_ref, n_blocks)


def _ffn_kernel(x1_ref, g2_ref, wup_ref, wdw_ref, bdw_ref, wdown_ref, y_ref, tail_ref, upbuf, *, chunk):
    rows = x1_ref.shape[0]
    t = pl.program_id(1)
    nt = pl.num_programs(1)
    dff = wdown_ref.shape[0]

    @pl.when(t == 0)
    def _():
        upbuf[0:SUBLANES, :] = jnp.zeros((SUBLANES, 2 * dff), F32)

    x1 = x1_ref[...]
    hn = _rms_rows(x1, g2_ref[...]).astype(BF16)

    for c in range(0, 2 * dff, chunk):
        upbuf[SUBLANES:SUBLANES + rows, c:c + chunk] = _dot(hn, wup_ref[:, c:c + chunk])

    def conv(c):
        out = bdw_ref[:, c:c + chunk]
        for k in range(FFN_CONV_WIDTH):
            r0 = SUBLANES - (FFN_CONV_WIDTH - 1) + k
            out = out + wdw_ref[k:k + 1, c:c + chunk] * upbuf[r0:r0 + rows, c:c + chunk]
        return out

    y = x1
    for c in range(0, dff, chunk):
        a = conv(c)
        b = conv(dff + c)
        act = (a * _sigmoid(a) * b).astype(BF16)
        y = y + _dot(act, wdown_ref[c:c + chunk, :])
    y_ref[...] = y

    @pl.when(t == nt - 1)
    def _():
        tail_ref[0] = upbuf[rows:rows + SUBLANES, :]

    upbuf[0:SUBLANES, :] = upbuf[rows:rows + SUBLANES, :]


def _premix_s_kernel(x_ref, g1_ref, w_ref, bd64_ref, bd128_ref, qg_ref, kg_ref, mqg_ref,
                     st_ref, wdw_ref, bdw_ref, lng_ref, lnb_ref,
                     u_ref, q_ref, k_ref, v_ref, mq_ref, gl_ref, cact_ref):
    c = CONV_CH
    hn = _rms_rows(x_ref[...], g1_ref[...]).astype(BF16)
    a = _dot(hn, w_ref[:, 0:c])
    g = _dot(hn, w_ref[:, c:2 * c])
    u = a * _sigmoid(g)
    u_ref[...] = u
    acc = bdw_ref[...] + wdw_ref[CONV_WIDTH - 1:CONV_WIDTH, :] * u
    for k in range(CONV_WIDTH - 1):
        acc = acc + wdw_ref[k:k + 1, :] * st_ref[k]
    cact_ref[...] = _layernorm_silu(acc, lng_ref[...], lnb_ref[...]).astype(BF16)

    o = 2 * c
    q_ref[...] = _group_rms(_dot(hn, w_ref[:, o:o + ATTN_W]), bd64_ref[...], qg_ref[...])
    o += ATTN_W
    k_ref[...] = _group_rms(_dot(hn, w_ref[:, o:o + ATTN_W]), bd64_ref[...], kg_ref[...])
    o += ATTN_W
    v_ref[...] = _dot(hn, w_ref[:, o:o + ATTN_W])
    o += ATTN_W
    mq_ref[...] = _group_rms(_dot(hn, w_ref[:, o:o + MEM_W]), bd128_ref[...], mqg_ref[...])
    o += MEM_W
    gl_ref[...] = _dot(hn, w_ref[:, o:])


PAGES_PER_BLOCK = MOBA_BLOCK // LANES


def _gate_scores(g, first_block, q_ref, page_refs):
    head = lax.broadcasted_iota(jnp.int32, (N_HEADS, LANES), 0)
    lane = lax.broadcasted_iota(jnp.int32, (HEAD_DIM, LANES), 1)
    for h in range(N_HEADS):
        kmean = jnp.zeros((HEAD_DIM, LANES), F32)
        for r in range(0, len(page_refs), PAGES_PER_BLOCK):
            ksum = page_refs[r][h]
            for extra in range(1, PAGES_PER_BLOCK):
                ksum = ksum + page_refs[r + extra][h]
            col = jnp.sum(ksum, axis=1, keepdims=True) * (1.0 / MOBA_BLOCK)
            kmean = jnp.where(lane == first_block + r // PAGES_PER_BLOCK, col, kmean)
        prod = q_ref[0, h].astype(BF16).astype(F32) * kmean.astype(BF16).astype(F32)
        g = g + jnp.where(head == h, jnp.sum(prod, axis=0, keepdims=True), 0.0)
    return g


def _gate_select(last, gate_scr, sel_ref, n_blocks):
    @pl.when(last)
    def _():
        g = gate_scr[:, 0:n_blocks]
        bl = lax.broadcasted_iota(jnp.int32, (N_HEADS, n_blocks), 1)
        rank = jnp.zeros((N_HEADS, n_blocks), F32)
        for b in range(n_blocks):
            other = g[:, b:b + 1]
            rank = rank + jnp.where(bl > b, jnp.where(other >= g, 1.0, 0.0), jnp.where(other > g, 1.0, 0.0))
        lane_o = lax.broadcasted_iota(jnp.int32, (N_HEADS, LANES), 1)
        out = jnp.zeros((N_HEADS, LANES), F32)
        blf = bl.astype(F32)
        for r in range(MOBA_TOPK):
            idx = jnp.sum(jnp.where(rank == float(r), blf, 0.0), axis=-1, keepdims=True)
            out = jnp.where(lane_o == r, idx, out)
        sel_ref[0] = out.astype(jnp.int32)


def _decode_attn_kernel(page_ref, sel_ref, slope_ref, q_ref, kown_ref, vown_ref, ck_hbm, cv_hbm, o_ref,
                        k_buf, v_buf, sem, *, past_len):
    pages_per_block = MOBA_BLOCK // LANES
    tiles_per_head = MOBA_TOPK * pages_per_block
    n_tiles = N_HEADS * tiles_per_head
    n = pl.program_id(0)
    slot = n % 2

    def tile_copies(page, to_slot, tile):
        h = tile // tiles_per_head
        return (pltpu.make_async_copy(ck_hbm.at[page, h], k_buf.at[to_slot, tile], sem.at[0, to_slot]),
                pltpu.make_async_copy(cv_hbm.at[page, h], v_buf.at[to_slot, tile], sem.at[1, to_slot]))

    def fetch(sample, to_slot):
        for tile in range(n_tiles):
            for copy in tile_copies(page_ref[sample * n_tiles + tile], to_slot, tile):
                copy.start()

    @pl.when(n == 0)
    def _():
        fetch(0, 0)

    @pl.when(n + 1 < pl.num_programs(0))
    def _():
        fetch(n + 1, 1 - slot)

    for tile in range(n_tiles):
        for copy in tile_copies(0, slot, tile):
            copy.wait()
    k_refs = [k_buf.at[slot, tile] for tile in range(n_tiles)]
    v_refs = [v_buf.at[slot, tile] for tile in range(n_tiles)]

    lane = lax.broadcasted_iota(jnp.int32, (1, LANES), 1)
    lanef = lane.astype(F32)
    zeros = jnp.zeros((HEAD_DIM, tiles_per_head * LANES), BF16)
    scale = HEAD_DIM ** -0.5

    def head_tiles(tile_refs, h):
        t = jnp.concatenate([tile_refs[h * tiles_per_head + idx][...] for idx in range(tiles_per_head)], axis=1)
        t = t.astype(BF16)
        return jnp.concatenate([t, zeros] if h % 2 == 0 else [zeros, t], axis=0)

    def own_half(h):
        return (lane < HEAD_DIM) if h % 2 == 0 else (lane >= HEAD_DIM)

    def pair_lanes(ref, h):
        return ref[0][:, LANES * (h // 2):LANES * (h // 2 + 1)]

    qz, raw = [], []
    for h in range(N_HEADS):
        qz.append(jnp.where(own_half(h), pair_lanes(q_ref, h) * scale, 0.0).astype(BF16))
        raw.append(_dot(jnp.broadcast_to(qz[h], (SUBLANES, LANES)), head_tiles(k_refs, h))[0:1, :])
    probs = []
    for h in range(N_HEADS):
        dist = []
        for r in range(MOBA_TOPK):
            blk = sel_ref[(n * N_HEADS + h) * MOBA_TOPK + r]
            for half in range(pages_per_block):
                pos0 = (blk * MOBA_BLOCK + half * LANES).astype(F32)
                dist.append(float(past_len) - (pos0 + lanef))
        logits = raw[h] - slope_ref[h] * jnp.concatenate(dist, axis=1)
        kown = pair_lanes(kown_ref, h).astype(BF16).astype(F32)
        s_own = jnp.sum(qz[h].astype(F32) * kown, axis=-1, keepdims=True)
        m = jnp.maximum(s_own, jnp.max(logits, axis=-1, keepdims=True))
        p_own = jnp.exp(s_own - m)
        p = jnp.exp(logits - m)
        probs.append((p, p_own, p_own + jnp.sum(p, axis=-1, keepdims=True)))
    outs = []
    for h in range(N_HEADS):
        p, p_own, l = probs[h]
        p8 = jnp.broadcast_to(p.astype(BF16), (SUBLANES, p.shape[1]))
        vown = jnp.where(own_half(h), pair_lanes(vown_ref, h).astype(BF16).astype(F32), 0.0)
        acc = p_own.astype(BF16).astype(F32) * vown + _dot_nt(p8, head_tiles(v_refs, h))[0:1, :]
        outs.append(acc / l)
    o_ref[0] = jnp.concatenate([outs[h] + outs[h + 1] for h in range(0, N_HEADS, 2)], axis=1)


def _memattn_s_kernel(mq_ref, mk_ref, mv_ref, o_ref, *, group, n_mem):
    units = [(s, hh) for s in range(group) for hh in range(MEM_HEADS)]
    scores = []
    for s, hh in units:
        mq = jnp.broadcast_to(mq_ref[s][:, MEM_HEAD_DIM * hh:MEM_HEAD_DIM * (hh + 1)], (SUBLANES, MEM_HEAD_DIM))
        mk = mk_ref[s, pl.ds(hh, n_mem, stride=MEM_HEADS), :].astype(BF16)
        scores.append(_dot_nt(mq.astype(BF16), mk)[0:1, :] * (MEM_HEAD_DIM ** -0.5))
    probs = []
    for sc in scores:
        p = jnp.exp(sc - jnp.max(sc, axis=-1, keepdims=True))
        probs.append((p, jnp.sum(p, axis=-1, keepdims=True)))
    outs = {}
    for (s, hh), (p, l) in zip(units, probs):
        mv = mv_ref[s, pl.ds(hh, n_mem, stride=MEM_HEADS), :].astype(BF16)
        p8 = jnp.broadcast_to(p.astype(BF16), (SUBLANES, n_mem))
        outs[s, hh] = _dot(p8, mv)[0:1, :] / l
    for s in range(group):
        o_ref[s] = jnp.concatenate([outs[s, hh] for hh in range(MEM_HEADS)], axis=-1)


def _post_s_kernel(x_ref, cact_ref, ob_ref, oc_ref, gl_ref, bg_ref, wpa_ref, wpb_ref, wpc_ref, wout_ref,
                   g2_ref, wup_ref, st_ref, wdw_ref, bdw_ref, wdown_ref, y_ref, up_ref):
    x = x_ref[...]
    dff = wdown_ref.shape[0]
    ya = _dot(cact_ref[...], wpa_ref[...])
    yb = _dot(ob_ref[...].astype(BF16), wpb_ref[...])
    yc = _dot(oc_ref[...].astype(BF16), wpc_ref[...])
    x1 = _merge_out(x, gl_ref[...] + bg_ref[...], ya, yb, yc, wout_ref[...])
    up = _dot(_rms_rows(x1, g2_ref[...]).astype(BF16), wup_ref[...])
    up_ref[...] = up
    cv = bdw_ref[...] + wdw_ref[FFN_CONV_WIDTH - 1:FFN_CONV_WIDTH, :] * up
    for k in range(FFN_CONV_WIDTH - 1):
        cv = cv + wdw_ref[k:k + 1, :] * st_ref[k]
    a = cv[:, 0:dff]
    b = cv[:, dff:]
    act = (a * _sigmoid(a) * b).astype(BF16)
    y_ref[...] = x1 + _dot(act, wdown_ref[...])


def _round_to_bf16(x):
    bits = np.asarray(x, np.float32).view(np.uint32)
    bits = (bits + np.uint32(0x7FFF) + ((bits >> np.uint32(16)) & np.uint32(1))) & np.uint32(0xFFFF0000)
    return bits.view(np.float32)


def _resident(shape):
    nd = len(shape)
    return pl.BlockSpec(shape, lambda *_: (0,) * nd, pipeline_mode=pl.Buffered(1))


def _params(n_axes):
    return pltpu.CompilerParams(dimension_semantics=("arbitrary",) * n_axes, vmem_limit_bytes=VMEM_LIMIT)


def kernel(x_prompt, x_sample, mem_prompt, cache_k, cache_v, page_table, state_conv, state_ffn_conv, cache_mem_k, cache_mem_v, norm1_g, w_in, b_gate, w_dw_a, b_dw_a, ln_a_g, ln_a_b, w_proj_a, q_norm_g, k_norm_g, w_proj_b, mem_norm_g, w_mem_kv, mq_norm_g, mk_norm_g, w_proj_c, w_out, norm2_g, w_up, w_dw_f, b_dw_f, w_down):
    n, t, d = x_prompt.shape
    nd, td, _ = x_sample.shape
    n_mem = mem_prompt.shape[1]
    n_pool, page_size = cache_k.shape[:2]
    n_pages = page_table.shape[1]
    past_len = n_pages * page_size
    dff = w_down.shape[0]
    nt = t // TM
    n_mix = 2 * CONV_CH + 3 * ATTN_W + MEM_W
    assert t % TM == 0 and nt <= GROUP and td == 1
    assert past_len % MOBA_BLOCK == 0 and MOBA_BLOCK == 2 * page_size and page_size == LANES
    n_blocks = past_len // MOBA_BLOCK

    row = lambda v: v.reshape(1, -1).astype(F32)
    w_in_b = w_in.astype(BF16)
    w_mix, w_gate = w_in_b[:, :n_mix], w_in_b[:, n_mix:]
    wpa, wpb, wpc = w_proj_a.astype(BF16), w_proj_b.astype(BF16), w_proj_c.astype(BF16)
    wout, wup, wdown = w_out.astype(BF16), w_up.astype(BF16), w_down.astype(BF16)
    g1, g2 = row(norm1_g), row(norm2_g)
    qg = row(jnp.tile(q_norm_g, N_HEADS))
    kg = row(jnp.tile(k_norm_g, N_HEADS))
    mqg = row(jnp.tile(mq_norm_g, MEM_HEADS))
    mkg = row(jnp.tile(mk_norm_g, MEM_HEADS))
    bdw_a, lng, lnb, bg, bdw_f = row(b_dw_a), row(ln_a_g), row(ln_a_b), row(b_gate), row(b_dw_f)
    grp = np.arange(ATTN_W)
    bd64 = jnp.asarray((grp[:, None] // HEAD_DIM == grp[None, :] // HEAD_DIM) / HEAD_DIM, BF16)
    bd128 = jnp.asarray((grp[:, None] // MEM_HEAD_DIM == grp[None, :] // MEM_HEAD_DIM) / MEM_HEAD_DIM, BF16)
    slopes = 2.0 ** (-8.0 * np.arange(1, N_HEADS + 1) / N_HEADS)
    aw = N_HEADS * LANES
    coef = {name: np.zeros((N_HEADS, LANES), np.float32) for name in ("qc", "qr", "qt", "kc", "kr", "kt")}
    rest = (slopes * LOG2E).astype(np.float32)
    for p in range(X_PARTS):
        part = _round_to_bf16(rest)
        rest = rest - part
        coef["qr"][:, X_RQ + p] = 1.0
        coef["kc"][:, X_RQ + p] = -part
        coef["qc"][:, X_RK + p] = part
        coef["kr"][:, X_RK + p] = 1.0
        coef["qt"][:, X_TQ + p] = 1.0
        coef["kc"][:, X_TQ + p] = -part * MOBA_BLOCK
        coef["qc"][:, X_TK + p] = part
        coef["kt"][:, X_TK + p] = MOBA_BLOCK
    lanes3 = lambda name: jnp.asarray(coef[name].reshape(1, 1, aw))
    tile_f = jnp.arange(nt, dtype=F32)[:, None, None]
    row_f = jnp.arange(TM, dtype=F32)[None, :, None]
    col = jnp.arange(aw, dtype=jnp.int32)[None, None, :] % LANES
    q_extra = (lanes3("qc") + row_f * lanes3("qr") + tile_f * lanes3("qt")).astype(BF16)
    k_extra = (lanes3("kc") + row_f * lanes3("kr") + tile_f * lanes3("kt")
               + jnp.where(col == X_SEL + tile_f.astype(jnp.int32), MASK_NEG, 0.0)).astype(BF16)

    xp = x_prompt.reshape(n * t, d)
    tile = lambda w: pl.BlockSpec((TM, w), lambda b, i, *_: (b * nt + i, 0))
    aw = N_HEADS * LANES
    vw = N_HEADS * V_ROWS
    blk3 = lambda r, c: pl.BlockSpec((1, r, c), lambda b, i: (b * nt + i, 0, 0))
    k_p, v_p, km, qt, ka, vt, mq, cact, utail = pl.pallas_call(
        _premix_kernel,
        grid=(n, nt),
        in_specs=[tile(d), _resident((1, d)), _resident((d, n_mix)), _resident((ATTN_W, ATTN_W)),
                  _resident((MEM_W, MEM_W)), _resident((1, ATTN_W)), _resident((1, ATTN_W)),
                  _resident((1, MEM_W)), _resident((CONV_WIDTH, CONV_CH)), _resident((1, CONV_CH)),
                  _resident((1, CONV_CH)), _resident((1, CONV_CH)),
                  pl.BlockSpec((1, TM, aw), lambda b, i: (i, 0, 0)),
                  pl.BlockSpec((1, TM, aw), lambda b, i: (i, 0, 0))],
        out_specs=[pl.BlockSpec((ATTN_W, TM), lambda b, i: (b, i)), pl.BlockSpec((ATTN_W, TM), lambda b, i: (b, i)),
                   pl.BlockSpec((1, 1, ATTN_W), lambda b, i: (b * nt + i, 0, 0)),
                   pl.BlockSpec((aw, TM), lambda b, i: (b, i)), blk3(TM, aw), blk3(vw, TM),
                   tile(MEM_W), tile(CONV_CH),
                   pl.BlockSpec((1, CONV_HALO, CONV_CH), lambda b, i: (b, 0, 0))],
        out_shape=[jax.ShapeDtypeStruct((n * ATTN_W, t), F32), jax.ShapeDtypeStruct((n * ATTN_W, t), F32),
                   jax.ShapeDtypeStruct((n * nt, 1, ATTN_W), F32),
                   jax.ShapeDtypeStruct((n * aw, t), BF16), jax.ShapeDtypeStruct((n * nt, TM, aw), BF16),
                   jax.ShapeDtypeStruct((n * nt, vw, TM), BF16), jax.ShapeDtypeStruct((n * t, MEM_W), BF16),
                   jax.ShapeDtypeStruct((n * t, CONV_CH), BF16),
                   jax.ShapeDtypeStruct((n, CONV_HALO, CONV_CH), F32)],
        scratch_shapes=[pltpu.VMEM((CONV_HALO + TM + SUBLANES, CONV_CH), F32)],
        compiler_params=_params(2),
        name="premix",
    )(xp, g1, w_mix, bd64, bd128, qg, kg, mqg, w_dw_a, bdw_a, lng, lnb, q_extra, k_extra)

    km4 = km.reshape(n, nt, N_HEADS, HEAD_DIM).transpose(0, 2, 1, 3)
    km4 = jnp.pad(km4, ((0, 0), (0, 0), (0, GROUP - nt), (0, LANES - HEAD_DIM)))
    kmt = (km4[:, :, :, None, :] * jnp.eye(N_HEADS, dtype=F32)[None, :, None, :, None])
    kmt = kmt.reshape(n, N_HEADS * GROUP, aw).astype(BF16)

    seq3 = lambda r, c: pl.BlockSpec((nt, r, c), lambda b, i: (b, 0, 0))
    ob = pl.pallas_call(
        _moba_kernel,
        grid=(n, nt),
        in_specs=[pl.BlockSpec((aw, TM), lambda b, i: (b, i)), seq3(TM, aw), seq3(vw, TM),
                  pl.BlockSpec((1, N_HEADS * GROUP, aw), lambda b, i: (b, 0, 0))],
        out_specs=tile(ATTN_W),
        out_shape=jax.ShapeDtypeStruct((n * t, ATTN_W), BF16),
        scratch_shapes=[pltpu.VMEM((N_HEADS, LANES, TM), BF16), pltpu.VMEM((N_HEADS, 1, TM), F32),
                        pltpu.VMEM((N_HEADS, V_ROWS, TM), F32)]
        + [pltpu.VMEM((TM, TM), F32)] * N_HEADS + [pltpu.VMEM((1, TM), F32)] * N_HEADS,
        compiler_params=_params(2),
        name="moba",
    )(qt, ka, vt, kmt)

    mem = mem_prompt.reshape(n * n_mem, d)
    mtile = lambda w: pl.BlockSpec((n_mem, w), lambda b: (b, 0))
    mk_p, mv_p, mkb, mvb = pl.pallas_call(
        _memkv_kernel,
        grid=(n,),
        in_specs=[mtile(d), _resident((1, d)), _resident((d, 2 * MEM_W)), _resident((MEM_W, MEM_W)),
                  _resident((1, MEM_W))],
        out_specs=[mtile(MEM_W)] * 4,
        out_shape=[jax.ShapeDtypeStruct((n * n_mem, MEM_W), F32)] * 2
        + [jax.ShapeDtypeStruct((n * n_mem, MEM_W), BF16)] * 2,
        compiler_params=_params(1),
        name="memkv",
    )(mem, row(mem_norm_g), w_mem_kv.astype(BF16), bd128, mkg)

    xs = x_sample.reshape(nd, d)
    st_conv = state_conv.transpose(1, 0, 2)
    st_ffn = state_ffn_conv.transpose(1, 0, 2)
    vm = pltpu.CompilerParams(vmem_limit_bytes=VMEM_LIMIT)
    u_s, q_s, k_s, v_s, mq_s, gl_s, cact_s = pl.pallas_call(
        _premix_s_kernel,
        out_shape=[jax.ShapeDtypeStruct((nd, CONV_CH), F32), jax.ShapeDtypeStruct((nd, ATTN_W), F32),
                   jax.ShapeDtypeStruct((nd, ATTN_W), F32), jax.ShapeDtypeStruct((nd, ATTN_W), F32),
                   jax.ShapeDtypeStruct((nd, MEM_W), F32), jax.ShapeDtypeStruct((nd, 3 * d), F32),
                   jax.ShapeDtypeStruct((nd, CONV_CH), BF16)],
        compiler_params=vm,
        name="premix_s",
    )(xs, g1, w_in_b, bd64, bd128, qg, kg, mqg, st_conv, w_dw_a, bdw_a, lng, lnb)

    ck = cache_k.transpose(0, 2, 3, 1)
    cv = cache_v.transpose(0, 2, 3, 1)
    pt_flat = page_table.reshape(-1).astype(jnp.int32)
    q_cols = jnp.broadcast_to(q_s.reshape(nd, N_HEADS, HEAD_DIM, 1), (nd, N_HEADS, HEAD_DIM, page_size))
    assert (n * nt) % nd == 0 and n_pages % ((n * nt) // nd) == 0 and n_blocks <= LANES
    steps_per_sample = (n * nt) // nd
    pages_per_step = n_pages // steps_per_sample
    assert pages_per_step % (MOBA_BLOCK // page_size) == 0

    memb = pl.BlockSpec((n_mem, MEM_W), lambda b, i, *_: (b, 0))
    x1, sel = pl.pallas_call(
        functools.partial(_postmix_kernel, pages_per_step=pages_per_step, n_blocks=n_blocks),
        grid_spec=pltpu.PrefetchScalarGridSpec(
            num_scalar_prefetch=1,
            grid=(n, nt),
            in_specs=[tile(d), tile(CONV_CH), tile(ATTN_W), tile(MEM_W), memb, memb, _resident((1, d)),
                      _resident((d, 3 * d)), _resident((1, 3 * d)), _resident((CONV_CH, d)),
                      _resident((ATTN_W, d)), _resident((MEM_W, d)), _resident((d, d)),
                      pl.BlockSpec((1, N_HEADS, HEAD_DIM, page_size),
                                   lambda b, i, pt: ((b * nt + i) // steps_per_sample, 0, 0, 0)),
                      pl.BlockSpec(memory_space=pl.ANY)],
            out_specs=[tile(d), pl.BlockSpec((1, N_HEADS, LANES),
                                             lambda b, i, pt: ((b * nt + i) // steps_per_sample, 0, 0))],
            scratch_shapes=[pltpu.VMEM((N_HEADS, LANES), F32), pltpu.VMEM((TM, d), BF16),
                            pltpu.VMEM((2, pages_per_step, N_HEADS, HEAD_DIM, page_size), F32),
                            pltpu.SemaphoreType.DMA((2,))]),
        out_shape=[jax.ShapeDtypeStruct((n * t, d), F32), jax.ShapeDtypeStruct((nd, N_HEADS, LANES), jnp.int32)],
        compiler_params=_params(2),
        name="postmix",
    )(pt_flat, xp, cact, ob, mq, mkb, mvb, g1, w_gate, bg, wpa, wpb, wpc, wout, q_cols, ck)
    sel_flat = sel[:, :, :MOBA_TOPK].reshape(-1)

    chunk = dff // 2
    assert chunk % LANES == 0 and t % FFN_ROWS == 0
    nf = t // FFN_ROWS
    ftile = pl.BlockSpec((FFN_ROWS, d), lambda b, i: (b * nf + i, 0))
    y_p, ftail = pl.pallas_call(
        functools.partial(_ffn_kernel, chunk=chunk),
        grid=(n, nf),
        in_specs=[ftile, _resident((1, d)), _resident((d, 2 * dff)), _resident((FFN_CONV_WIDTH, 2 * dff)),
                  _resident((1, 2 * dff)), _resident((dff, d))],
        out_specs=[ftile, pl.BlockSpec((1, SUBLANES, 2 * dff), lambda b, i: (b, 0, 0))],
        out_shape=[jax.ShapeDtypeStruct((n * t, d), F32), jax.ShapeDtypeStruct((n, SUBLANES, 2 * dff), F32)],
        scratch_shapes=[pltpu.VMEM((SUBLANES + FFN_ROWS, 2 * dff), F32)],
        compiler_params=_params(2),
        name="ffn",
    )(x1, g2, wup, w_dw_f, bdw_f, wdown)

    row_spec = pl.BlockSpec((1, 1, ATTN_W), lambda b, *_: (b, 0, 0))
    rows3 = lambda a: a.reshape(nd, 1, -1)

    halves = jnp.arange(PAGES_PER_BLOCK, dtype=jnp.int32)
    sel_pages = PAGES_PER_BLOCK * sel[:, :, :MOBA_TOPK, None] + halves
    page_ids = jnp.take_along_axis(page_table.astype(jnp.int32), sel_pages.reshape(nd, -1), axis=1).reshape(-1)
    tiles_per_sample = N_HEADS * MOBA_TOPK * PAGES_PER_BLOCK
    tile_buf = pltpu.VMEM((2, tiles_per_sample, HEAD_DIM, page_size), F32)
    ob_s = pl.pallas_call(
        functools.partial(_decode_attn_kernel, past_len=past_len),
        grid_spec=pltpu.PrefetchScalarGridSpec(
            num_scalar_prefetch=2,
            grid=(nd,),
            in_specs=[pl.BlockSpec(memory_space=pltpu.SMEM), row_spec, row_spec, row_spec,
                      pl.BlockSpec(memory_space=pl.ANY), pl.BlockSpec(memory_space=pl.ANY)],
            out_specs=row_spec,
            scratch_shapes=[tile_buf, tile_buf, pltpu.SemaphoreType.DMA((2, 2))]),
        out_shape=jax.ShapeDtypeStruct((nd, 1, ATTN_W), F32),
        compiler_params=_params(1),
        name="decode_attn",
    )(page_ids, sel_flat, jnp.asarray(slopes, F32), rows3(q_s), rows3(k_s), rows3(v_s), ck, cv)
    ob_s = ob_s.reshape(nd, ATTN_W)

    group = SUBLANES
    assert nd % group == 0
    cm_spec = pl.BlockSpec((group, n_mem * MEM_HEADS, MEM_HEAD_DIM), lambda b: (b, 0, 0))
    mq_spec = pl.BlockSpec((group, 1, MEM_W), lambda b: (b, 0, 0))
    oc_s = pl.pallas_call(
        functools.partial(_memattn_s_kernel, group=group, n_mem=n_mem),
        grid=(nd // group,),
        in_specs=[mq_spec, cm_spec, cm_spec],
        out_specs=mq_spec,
        out_shape=jax.ShapeDtypeStruct((nd, 1, MEM_W), F32),
        compiler_params=_params(1),
        name="memattn_s",
    )(rows3(mq_s), cache_mem_k.reshape(nd, n_mem * MEM_HEADS, MEM_HEAD_DIM),
      cache_mem_v.reshape(nd, n_mem * MEM_HEADS, MEM_HEAD_DIM))
    oc_s = oc_s.reshape(nd, MEM_W)

    y_s, up_s = pl.pallas_call(
        _post_s_kernel,
        out_shape=[jax.ShapeDtypeStruct((nd, d), F32), jax.ShapeDtypeStruct((nd, 2 * dff), F32)],
        compiler_params=vm,
        name="post_s",
    )(xs, cact_s, ob_s, oc_s, gl_s, bg, wpa, wpb, wpc, wout, g2, wup, st_ffn, w_dw_f, bdw_f, wdown)

    heads = lambda a, b, s: a.reshape(b, s, N_HEADS, HEAD_DIM)
    from_t = lambda a: a.reshape(n, N_HEADS, HEAD_DIM, t).transpose(0, 3, 1, 2)
    conv_p = utail[:, CONV_HALO - (CONV_WIDTH - 1):, :]
    conv_s = jnp.concatenate([state_conv[:, 1:, :], u_s[:, None, :]], axis=1)
    ffn_p = ftail[:, SUBLANES - (FFN_CONV_WIDTH - 1):, :]
    ffn_s = jnp.concatenate([state_ffn_conv[:, 1:, :], up_s[:, None, :]], axis=1)
    return (y_p.reshape(n, t, d), y_s.reshape(nd, td, d),
            from_t(k_p), from_t(v_p), heads(k_s, nd, td), heads(v_s, nd, td),
            conv_p, conv_s, ffn_p, ffn_s,
            mk_p.reshape(n, n_mem, MEM_HEADS, MEM_HEAD_DIM), mv_p.reshape(n, n_mem, MEM_HEADS, MEM_HEAD_DIM))
```

```python
import functools

import numpy as np
import jax
import jax.numpy as jnp
from jax import lax
from jax.experimental import pallas as pl
from jax.experimental.pallas import tpu as pltpu

F32 = jnp.float32
BF16 = jnp.bfloat16

EPS = 1e-6
CONV_CH = 512
CONV_WIDTH = 31
N_HEADS = 8
HEAD_DIM = 64
ATTN_W = N_HEADS * HEAD_DIM
MOBA_BLOCK = 256
MOBA_TOPK = 3
MEM_HEADS = 4
MEM_HEAD_DIM = 128
MEM_W = MEM_HEADS * MEM_HEAD_DIM
FFN_CONV_WIDTH = 3
LANES = 128
SUBLANES = 8
TM = MOBA_BLOCK
CONV_HALO = 32
MASK_NEG = -float(2 ** 30)
GROUP = 16
V_ROWS = HEAD_DIM + 16
BLOCKS_PER_TRIP = 4
FFN_ROWS = 512
VMEM_LIMIT = 56 * 1024 * 1024

X_SEL = HEAD_DIM
X_PARTS = 3
X_RQ = HEAD_DIM + GROUP
X_RK = X_RQ + X_PARTS
X_TQ = X_RK + X_PARTS
X_TK = X_TQ + X_PARTS
LOG2E = 1.4426950408889634


def _dot(a, b):
    return jnp.dot(a, b, preferred_element_type=F32)


def _dot_nt(a, b):
    return lax.dot_general(a, b, (((1,), (1,)), ((), ())), preferred_element_type=F32)


def _rms_rows(x, g):
    return x * lax.rsqrt(jnp.mean(x * x, axis=-1, keepdims=True) + EPS) * g


def _group_rms(z, bd, g):
    sq = z * z
    hi = sq.astype(BF16)
    lo = (sq - hi.astype(F32)).astype(BF16)
    ms = _dot(hi, bd) + _dot(lo, bd)
    return z * lax.rsqrt(ms + EPS) * g


def _exact_zero(v):
    bits = pltpu.bitcast(v, jnp.uint32)
    half = jnp.uint32(16)
    return pltpu.bitcast(lax.shift_right_logical(lax.shift_right_logical(bits, half), half), F32)


def _sigmoid(x):
    return 1.0 / (1.0 + jnp.exp(-x))


def _layernorm_silu(c, g, b):
    mu = jnp.mean(c, axis=-1, keepdims=True)
    xc = c - mu
    var = jnp.mean(xc * xc, axis=-1, keepdims=True)
    y = xc * lax.rsqrt(var + EPS) * g + b
    return y * _sigmoid(y)


def _premix_kernel(x_ref, g1_ref, w_ref, bd64_ref, bd128_ref, qg_ref, kg_ref, mqg_ref,
                   wdw_ref, bdw_ref, lng_ref, lnb_ref, qx_ref, kx_ref,
                   k_ref, v_ref, km_ref, qt_ref, ka_ref, vt_ref, mq_ref, cact_ref, utail_ref,
                   ubuf):
    t = pl.program_id(1)
    nt = pl.num_programs(1)
    c = CONV_CH

    @pl.when(t == 0)
    def _():
        ubuf[0:CONV_HALO, :] = jnp.zeros((CONV_HALO, c), F32)
        ubuf[CONV_HALO + TM:, :] = jnp.zeros((SUBLANES, c), F32)

    hn = _rms_rows(x_ref[...], g1_ref[...]).astype(BF16)

    a = _dot(hn, w_ref[:, 0:c])
    g = _dot(hn, w_ref[:, c:2 * c])
    u = a * _sigmoid(g)
    ubuf[CONV_HALO:CONV_HALO + TM, :] = u
    o = 2 * c
    zq = _dot(hn, w_ref[:, o:o + ATTN_W])
    zk = _dot(hn, w_ref[:, o + ATTN_W:o + 2 * ATTN_W])
    zv = _dot(hn, w_ref[:, o + 2 * ATTN_W:o + 3 * ATTN_W])
    zm = _dot(hn, w_ref[:, o + 3 * ATTN_W:o + 3 * ATTN_W + MEM_W])
    acc = jnp.broadcast_to(bdw_ref[...], (TM, c))
    base = CONV_HALO - (CONV_WIDTH - 1)
    span = TM + 2 * SUBLANES
    for b in range(SUBLANES):
        part = None
        for k in range(b, CONV_WIDTH, SUBLANES):
            term = wdw_ref[k:k + 1, :] * ubuf[k - b:k - b + span, :]
            part = term if part is None else part + term
        acc = acc + part[base + b:base + b + TM, :]
    ubuf[0:CONV_HALO, :] = ubuf[TM:TM + CONV_HALO, :]
    cact_ref[...] = _layernorm_silu(acc, lng_ref[...], lnb_ref[...]).astype(BF16)

    qn = _group_rms(zq, bd64_ref[...], qg_ref[...]) * (HEAD_DIM ** -0.5)
    kn = _group_rms(zk, bd64_ref[...], kg_ref[...])
    mq = _group_rms(zm, bd128_ref[...], mqg_ref[...])

    ones_rows = jnp.where(lax.broadcasted_iota(jnp.int32, (V_ROWS - HEAD_DIM, TM), 0) == 0, 1.0, 0.0).astype(BF16)
    for cb in range(ATTN_W // LANES):
        k_ref[LANES * cb:LANES * (cb + 1), :] = kn[:, LANES * cb:LANES * (cb + 1)].T
        vt = zv[:, LANES * cb:LANES * (cb + 1)].T
        v_ref[LANES * cb:LANES * (cb + 1), :] = vt
        for sub in range(2):
            r0 = V_ROWS * (2 * cb + sub)
            vt_ref[0, r0:r0 + HEAD_DIM, :] = vt[HEAD_DIM * sub:HEAD_DIM * (sub + 1), :].astype(BF16)
            vt_ref[0, r0 + HEAD_DIM:r0 + V_ROWS, :] = ones_rows
    mq_ref[...] = mq.astype(BF16)
    km_ref[0] = jnp.mean(kn, axis=0, keepdims=True)

    lane = lax.broadcasted_iota(jnp.int32, (TM, LANES), 1)
    for hp in range(N_HEADS // 2):
        xq = qn[:, LANES * hp:LANES * (hp + 1)]
        xk = kn[:, LANES * hp:LANES * (hp + 1)] * LOG2E
        for sub in range(2):
            h = 2 * hp + sub
            if sub == 1:
                xq = pltpu.roll(xq, HEAD_DIM, 1)
                xk = pltpu.roll(xk, HEAD_DIM, 1)
            eq = qx_ref[0, :, LANES * h:LANES * (h + 1)].astype(F32)
            ek = kx_ref[0, :, LANES * h:LANES * (h + 1)].astype(F32)
            qt_ref[LANES * h:LANES * (h + 1), :] = jnp.where(lane < HEAD_DIM, xq, eq).T.astype(BF16)
            ka_ref[0, :, LANES * h:LANES * (h + 1)] = jnp.where(lane < HEAD_DIM, xk, ek).astype(BF16)

    @pl.when(t == nt - 1)
    def _():
        utail_ref[0] = ubuf[0:CONV_HALO, :]


def _moba_kernel(qt_ref, ka_ref, vt_ref, kmt_ref, o_ref, qh_scr, m_scr, acc_scr, *stage):
    i = pl.program_id(1)
    s_scr, mx_scr = stage[:N_HEADS], stage[N_HEADS:]
    key = lax.broadcasted_iota(jnp.int32, (TM, TM), 0)
    qry = lax.broadcasted_iota(jnp.int32, (TM, TM), 1)
    causal = key <= qry

    def scores(j, h, diagonal):
        q = qt_ref[LANES * h:LANES * (h + 1), :] if diagonal else qh_scr[h]
        s = _dot(ka_ref[j, :, LANES * h:LANES * (h + 1)], q)
        if diagonal:
            s = jnp.where(causal, s, -jnp.inf)
        s_scr[h][...] = s
        mx_scr[h][...] = jnp.max(s, axis=0, keepdims=True)

    for h in range(N_HEADS):
        scores(i, h, True)

    blk = lax.broadcasted_iota(jnp.int32, (N_HEADS, GROUP, TM), 1)
    past = blk < i
    gate = _dot(kmt_ref[0], qt_ref[...]).reshape(N_HEADS, GROUP, TM)
    g = jnp.where(past, gate, -jnp.inf)
    unselected = jnp.where(past, 1.0, 0.0)
    for _ in range(MOBA_TOPK):
        best = jnp.max(g, axis=1, keepdims=True)
        first = jnp.min(jnp.where(g == best, blk, GROUP), axis=1, keepdims=True)
        taken = blk == first
        g = jnp.where(taken, -jnp.inf, g)
        unselected = jnp.where(taken, 0.0, unselected)
    nsel = unselected.astype(BF16).reshape(N_HEADS * GROUP, TM)
    for h in range(N_HEADS):
        r0 = LANES * h
        qh_scr[h, 0:X_SEL, :] = qt_ref[r0:r0 + X_SEL, :]
        qh_scr[h, X_SEL:X_RQ, :] = nsel[GROUP * h:GROUP * (h + 1), :]
        qh_scr[h, X_RQ:LANES, :] = qt_ref[r0 + X_RQ:r0 + LANES, :]

    def accumulate(j, h):
        m_old = m_scr[h]
        m_new = jnp.maximum(m_old, mx_scr[h][...])
        alpha = jnp.exp2(m_old - m_new)
        p = jnp.exp2(s_scr[h][...] - m_new).astype(BF16)
        acc_scr[h] = alpha * acc_scr[h] + _dot(vt_ref[j, V_ROWS * h:V_ROWS * (h + 1), :], p)
        m_scr[h] = m_new

    m_scr[...] = jnp.full(m_scr.shape, -jnp.inf, F32)
    acc_scr[...] = jnp.zeros(acc_scr.shape, F32)

    def step(prev, j):
        for h in range(N_HEADS):
            accumulate(prev, h)
            scores(j, h, False)

    def steps(first, count):
        step(jnp.where(first == 0, i, first - 1), first)
        for extra in range(1, count):
            step(first + extra - 1, first + extra)

    def body(trip, carry):
        steps(trip * BLOCKS_PER_TRIP, BLOCKS_PER_TRIP)
        return carry

    lax.fori_loop(0, i // BLOCKS_PER_TRIP, body, 0)
    size = BLOCKS_PER_TRIP // 2
    while size:
        done = (i // (2 * size)) * (2 * size)

        @pl.when((i // size) % 2 == 1)
        def _(done=done, size=size):
            steps(done, size)

        size //= 2
    last = jnp.where(i == 0, i, i - 1)
    for h in range(N_HEADS):
        accumulate(last, h)
    outs = []
    for h in range(N_HEADS):
        acc = acc_scr[h]
        outs.append(acc[0:HEAD_DIM, :] / acc[HEAD_DIM:HEAD_DIM + 1, :])
    o_ref[...] = jnp.concatenate(outs, axis=0).T.astype(BF16)


def _memkv_kernel(mem_ref, g_ref, w_ref, bd128_ref, mkg_ref, mk_ref, mv_ref, mkb_ref, mvb_ref):
    hn = _rms_rows(mem_ref[...], g_ref[...]).astype(BF16)
    mk = _group_rms(_dot(hn, w_ref[:, 0:MEM_W]), bd128_ref[...], mkg_ref[...])
    mv = _dot(hn, w_ref[:, MEM_W:2 * MEM_W])
    mk_ref[...] = mk
    mv_ref[...] = mv
    mkb_ref[...] = mk.astype(BF16)
    mvb_ref[...] = mv.astype(BF16)


def _mem_attend_rows(mq, mk, mv):
    heads = [slice(MEM_HEAD_DIM * hh, MEM_HEAD_DIM * (hh + 1)) for hh in range(MEM_HEADS)]
    scores = [_dot_nt(mq[:, sl], mk[:, sl]) * (MEM_HEAD_DIM ** -0.5) for sl in heads]
    probs = []
    for s in scores:
        p = jnp.exp(s - jnp.max(s, axis=-1, keepdims=True))
        probs.append((p.astype(BF16), jnp.sum(p, axis=-1, keepdims=True)))
    return jnp.concatenate([_dot(p, mv[:, sl]) / l for (p, l), sl in zip(probs, heads)], axis=-1)


def _merge_out(x, gl, ya, yb, yc, wout):
    d = x.shape[-1]
    merged = (_sigmoid(gl[:, 0:d]) * ya + _sigmoid(gl[:, d:2 * d]) * yb + _sigmoid(gl[:, 2 * d:3 * d]) * yc)
    return x + _dot(merged.astype(BF16), wout)


def _postmix_kernel(pt_ref, x_ref, cact_ref, ob_ref, mq_ref, mk_ref, mv_ref, g1_ref, wg_ref, bg_ref,
                    wpa_ref, wpb_ref, wpc_ref, wout_ref, q_ref, ck_hbm, x1_ref, sel_ref,
                    gate_scr, merged_scr, page_buf, page_sem, *, pages_per_step, n_blocks):
    step = pl.program_id(0) * pl.num_programs(1) + pl.program_id(1)
    n_steps = pl.num_programs(0) * pl.num_programs(1)
    slot = step % 2

    def page_copy(page, to_slot, r):
        return pltpu.make_async_copy(ck_hbm.at[page], page_buf.at[to_slot, r], page_sem.at[to_slot])

    def fetch(for_step, to_slot):
        for r in range(pages_per_step):
            page_copy(pt_ref[for_step * pages_per_step + r], to_slot, r).start()

    @pl.when(step == 0)
    def _():
        gate_scr[...] = jnp.zeros(gate_scr.shape, F32)
        fetch(0, 0)

    @pl.when(step + 1 < n_steps)
    def _():
        fetch(step + 1, 1 - slot)

    for r in range(pages_per_step):
        page_copy(0, slot, r).wait()
    page_refs = [page_buf.at[slot, r] for r in range(pages_per_step)]

    blocks_per_step = pages_per_step // PAGES_PER_BLOCK
    steps_per_sample = n_blocks // blocks_per_step
    share = step % steps_per_sample
    n_chunks = 4
    chunk = pages_per_step // n_chunks
    assert chunk % PAGES_PER_BLOCK == 0 and chunk * n_chunks == pages_per_step

    def gate_chunk(g, c):
        first = share * blocks_per_step + c * (chunk // PAGES_PER_BLOCK)
        return _gate_scores(g, first, q_ref, page_refs[c * chunk:(c + 1) * chunk])

    g = jnp.where(share == 0, 0.0, gate_scr[...])
    x = x_ref[...]
    d = x.shape[-1]
    hn = _rms_rows(x, g1_ref[...]).astype(BF16)
    oc = _mem_attend_rows(mq_ref[...], mk_ref[...], mv_ref[...]).astype(BF16)
    width = d // n_chunks
    for c in range(n_chunks):
        g = gate_chunk(g, c)
        tie = jnp.concatenate([_exact_zero(g)[0:1, :]] * (width // LANES), axis=1)
        sources = ((cact_ref[...], wpa_ref), (ob_ref[...], wpb_ref), (oc, wpc_ref))
        ys = [_dot(src, w_ref[:, width * c:width * (c + 1)]) for src, w_ref in sources]
        merged = None
        for br in range(len(sources)):
            cols = slice(d * br + width * c, d * br + width * (c + 1))
            gl = _dot(hn, wg_ref[:, cols]) + (bg_ref[:, cols] + tie)
            term = _sigmoid(gl) * ys[br]
            merged = term if merged is None else merged + term
        merged_scr[:, width * c:width * (c + 1)] = merged.astype(BF16)
    x1_ref[...] = x + _dot(merged_scr[...], wout_ref[...])
    gate_scr[...] = g
    _gate_select(share == steps_per_sample - 1, gate_scr, sel_ref, n_blocks)


def _ffn_kernel(x1_ref, g2_ref, wup_ref, wdw_ref, bdw_ref, wdown_ref, y_ref, tail_ref, upbuf, *, chunk):
    rows = x1_ref.shape[0]
    t = pl.program_id(1)
    nt = pl.num_programs(1)
    dff = wdown_ref.shape[0]

    @pl.when(t == 0)
    def _():
        upbuf[0:SUBLANES, :] = jnp.zeros((SUBLANES, 2 * dff), F32)

    x1 = x1_ref[...]
    hn = _rms_rows(x1, g2_ref[...]).astype(BF16)

    for c in range(0, 2 * dff, chunk):
        upbuf[SUBLANES:SUBLANES + rows, c:c + chunk] = _dot(hn, wup_ref[:, c:c + chunk])

    def conv(c):
        out = bdw_ref[:, c:c + chunk]
        for k in range(FFN_CONV_WIDTH):
            r0 = SUBLANES - (FFN_CONV_WIDTH - 1) + k
            out = out + wdw_ref[k:k + 1, c:c + chunk] * upbuf[r0:r0 + rows, c:c + chunk]
        return out

    y = x1
    for c in range(0, dff, chunk):
        a = conv(c)
        b = conv(dff + c)
        act = (a * _sigmoid(a) * b).astype(BF16)
        y = y + _dot(act, wdown_ref[c:c + chunk, :])
    y_ref[...] = y

    @pl.when(t == nt - 1)
    def _():
        tail_ref[0] = upbuf[rows:rows + SUBLANES, :]

    upbuf[0:SUBLANES, :] = upbuf[rows:rows + SUBLANES, :]


def _premix_s_kernel(x_ref, g1_ref, w_ref, bd64_ref, bd128_ref, qg_ref, kg_ref, mqg_ref,
                     st_ref, wdw_ref, bdw_ref, lng_ref, lnb_ref,
                     u_ref, q_ref, k_ref, v_ref, mq_ref, gl_ref, cact_ref):
    c = CONV_CH
    hn = _rms_rows(x_ref[...], g1_ref[...]).astype(BF16)
    a = _dot(hn, w_ref[:, 0:c])
    g = _dot(hn, w_ref[:, c:2 * c])
    u = a * _sigmoid(g)
    u_ref[...] = u
    acc = bdw_ref[...] + wdw_ref[CONV_WIDTH - 1:CONV_WIDTH, :] * u
    for k in range(CONV_WIDTH - 1):
        acc = acc + wdw_ref[k:k + 1, :] * st_ref[k]
    cact_ref[...] = _layernorm_silu(acc, lng_ref[...], lnb_ref[...]).astype(BF16)

    o = 2 * c
    q_ref[...] = _group_rms(_dot(hn, w_ref[:, o:o + ATTN_W]), bd64_ref[...], qg_ref[...])
    o += ATTN_W
    k_ref[...] = _group_rms(_dot(hn, w_ref[:, o:o + ATTN_W]), bd64_ref[...], kg_ref[...])
    o += ATTN_W
    v_ref[...] = _dot(hn, w_ref[:, o:o + ATTN_W])
    o += ATTN_W
    mq_ref[...] = _group_rms(_dot(hn, w_ref[:, o:o + MEM_W]), bd128_ref[...], mqg_ref[...])
    o += MEM_W
    gl_ref[...] = _dot(hn, w_ref[:, o:])


PAGES_PER_BLOCK = MOBA_BLOCK // LANES


def _gate_scores(g, first_block, q_ref, page_refs):
    head = lax.broadcasted_iota(jnp.int32, (N_HEADS, LANES), 0)
    lane = lax.broadcasted_iota(jnp.int32, (HEAD_DIM, LANES), 1)
    for h in range(N_HEADS):
        kmean = jnp.zeros((HEAD_DIM, LANES), F32)
        for r in range(0, len(page_refs), PAGES_PER_BLOCK):
            ksum = page_refs[r][h]
            for extra in range(1, PAGES_PER_BLOCK):
                ksum = ksum + page_refs[r + extra][h]
            col = jnp.sum(ksum, axis=1, keepdims=True) * (1.0 / MOBA_BLOCK)
            kmean = jnp.where(lane == first_block + r // PAGES_PER_BLOCK, col, kmean)
        prod = q_ref[0, h].astype(BF16).astype(F32) * kmean.astype(BF16).astype(F32)
        g = g + jnp.where(head == h, jnp.sum(prod, axis=0, keepdims=True), 0.0)
    return g


def _gate_select(last, gate_scr, sel_ref, n_blocks):
    @pl.when(last)
    def _():
        g = gate_scr[:, 0:n_blocks]
        bl = lax.broadcasted_iota(jnp.int32, (N_HEADS, n_blocks), 1)
        rank = jnp.zeros((N_HEADS, n_blocks), F32)
        for b in range(n_blocks):
            other = g[:, b:b + 1]
            rank = rank + jnp.where(bl > b, jnp.where(other >= g, 1.0, 0.0), jnp.where(other > g, 1.0, 0.0))
        lane_o = lax.broadcasted_iota(jnp.int32, (N_HEADS, LANES), 1)
        out = jnp.zeros((N_HEADS, LANES), F32)
        blf = bl.astype(F32)
        for r in range(MOBA_TOPK):
            idx = jnp.sum(jnp.where(rank == float(r), blf, 0.0), axis=-1, keepdims=True)
            out = jnp.where(lane_o == r, idx, out)
        sel_ref[0] = out.astype(jnp.int32)


def _decode_attn_kernel(page_ref, sel_ref, slope_ref, q_ref, kown_ref, vown_ref, ck_hbm, cv_hbm, o_ref,
                        k_buf, v_buf, sem, *, past_len):
    pages_per_block = MOBA_BLOCK // LANES
    tiles_per_head = MOBA_TOPK * pages_per_block
    n_tiles = N_HEADS * tiles_per_head
    n = pl.program_id(0)
    slot = n % 2

    def tile_copies(page, to_slot, tile):
        h = tile // tiles_per_head
        return (pltpu.make_async_copy(ck_hbm.at[page, h], k_buf.at[to_slot, tile], sem.at[0, to_slot]),
                pltpu.make_async_copy(cv_hbm.at[page, h], v_buf.at[to_slot, tile], sem.at[1, to_slot]))

    def fetch(sample, to_slot):
        for tile in range(n_tiles):
            for copy in tile_copies(page_ref[sample * n_tiles + tile], to_slot, tile):
                copy.start()

    @pl.when(n == 0)
    def _():
        fetch(0, 0)

    @pl.when(n + 1 < pl.num_programs(0))
    def _():
        fetch(n + 1, 1 - slot)

    for tile in range(n_tiles):
        for copy in tile_copies(0, slot, tile):
            copy.wait()
    k_refs = [k_buf.at[slot, tile] for tile in range(n_tiles)]
    v_refs = [v_buf.at[slot, tile] for tile in range(n_tiles)]

    lane = lax.broadcasted_iota(jnp.int32, (1, LANES), 1)
    lanef = lane.astype(F32)
    zeros = jnp.zeros((HEAD_DIM, tiles_per_head * LANES), BF16)
    scale = HEAD_DIM ** -0.5

    def head_tiles(tile_refs, h):
        t = jnp.concatenate([tile_refs[h * tiles_per_head + idx][...] for idx in range(tiles_per_head)], axis=1)
        t = t.astype(BF16)
        return jnp.concatenate([t, zeros] if h % 2 == 0 else [zeros, t], axis=0)

    def own_half(h):
        return (lane < HEAD_DIM) if h % 2 == 0 else (lane >= HEAD_DIM)

    def pair_lanes(ref, h):
        return ref[0][:, LANES * (h // 2):LANES * (h // 2 + 1)]

    qz, raw = [], []
    for h in range(N_HEADS):
        qz.append(jnp.where(own_half(h), pair_lanes(q_ref, h) * scale, 0.0).astype(BF16))
        raw.append(_dot(jnp.broadcast_to(qz[h], (SUBLANES, LANES)), head_tiles(k_refs, h))[0:1, :])
    probs = []
    for h in range(N_HEADS):
        dist = []
        for r in range(MOBA_TOPK):
            blk = sel_ref[(n * N_HEADS + h) * MOBA_TOPK + r]
            for half in range(pages_per_block):
                pos0 = (blk * MOBA_BLOCK + half * LANES).astype(F32)
                dist.append(float(past_len) - (pos0 + lanef))
        logits = raw[h] - slope_ref[h] * jnp.concatenate(dist, axis=1)
        kown = pair_lanes(kown_ref, h).astype(BF16).astype(F32)
        s_own = jnp.sum(qz[h].astype(F32) * kown, axis=-1, keepdims=True)
        m = jnp.maximum(s_own, jnp.max(logits, axis=-1, keepdims=True))
        p_own = jnp.exp(s_own - m)
        p = jnp.exp(logits - m)
        probs.append((p, p_own, p_own + jnp.sum(p, axis=-1, keepdims=True)))
    outs = []
    for h in range(N_HEADS):
        p, p_own, l = probs[h]
        p8 = jnp.broadcast_to(p.astype(BF16), (SUBLANES, p.shape[1]))
        vown = jnp.where(own_half(h), pair_lanes(vown_ref, h).astype(BF16).astype(F32), 0.0)
        acc = p_own.astype(BF16).astype(F32) * vown + _dot_nt(p8, head_tiles(v_refs, h))[0:1, :]
        outs.append(acc / l)
    o_ref[0] = jnp.concatenate([outs[h] + outs[h + 1] for h in range(0, N_HEADS, 2)], axis=1)


def _memattn_s_kernel(mq_ref, mk_ref, mv_ref, o_ref, *, group, n_mem):
    units = [(s, hh) for s in range(group) for hh in range(MEM_HEADS)]
    scores = []
    for s, hh in units:
        mq = jnp.broadcast_to(mq_ref[s][:, MEM_HEAD_DIM * hh:MEM_HEAD_DIM * (hh + 1)], (SUBLANES, MEM_HEAD_DIM))
        mk = mk_ref[s, pl.ds(hh, n_mem, stride=MEM_HEADS), :].astype(BF16)
        scores.append(_dot_nt(mq.astype(BF16), mk)[0:1, :] * (MEM_HEAD_DIM ** -0.5))
    probs = []
    for sc in scores:
        p = jnp.exp(sc - jnp.max(sc, axis=-1, keepdims=True))
        probs.append((p, jnp.sum(p, axis=-1, keepdims=True)))
    outs = {}
    for (s, hh), (p, l) in zip(units, probs):
        mv = mv_ref[s, pl.ds(hh, n_mem, stride=MEM_HEADS), :].astype(BF16)
        p8 = jnp.broadcast_to(p.astype(BF16), (SUBLANES, n_mem))
        outs[s, hh] = _dot(p8, mv)[0:1, :] / l
    for s in range(group):
        o_ref[s] = jnp.concatenate([outs[s, hh] for hh in range(MEM_HEADS)], axis=-1)


def _post_s_kernel(x_ref, cact_ref, ob_ref, oc_ref, gl_ref, bg_ref, wpa_ref, wpb_ref, wpc_ref, wout_ref,
                   g2_ref, wup_ref, st_ref, wdw_ref, bdw_ref, wdown_ref, y_ref, up_ref):
    x = x_ref[...]
    dff = wdown_ref.shape[0]
    ya = _dot(cact_ref[...], wpa_ref[...])
    yb = _dot(ob_ref[...].astype(BF16), wpb_ref[...])
    yc = _dot(oc_ref[...].astype(BF16), wpc_ref[...])
    x1 = _merge_out(x, gl_ref[...] + bg_ref[...], ya, yb, yc, wout_ref[...])
    up = _dot(_rms_rows(x1, g2_ref[...]).astype(BF16), wup_ref[...])
    up_ref[...] = up
    cv = bdw_ref[...] + wdw_ref[FFN_CONV_WIDTH - 1:FFN_CONV_WIDTH, :] * up
    for k in range(FFN_CONV_WIDTH - 1):
        cv = cv + wdw_ref[k:k + 1, :] * st_ref[k]
    a = cv[:, 0:dff]
    b = cv[:, dff:]
    act = (a * _sigmoid(a) * b).astype(BF16)
    y_ref[...] = x1 + _dot(act, wdown_ref[...])


def _round_to_bf16(x):
    bits = np.asarray(x, np.float32).view(np.uint32)
    bits = (bits + np.uint32(0x7FFF) + ((bits >> np.uint32(16)) & np.uint32(1))) & np.uint32(0xFFFF0000)
    return bits.view(np.float32)


def _resident(shape):
    nd = len(shape)
    return pl.BlockSpec(shape, lambda *_: (0,) * nd, pipeline_mode=pl.Buffered(1))


def _params(n_axes):
    return pltpu.CompilerParams(dimension_semantics=("arbitrary",) * n_axes, vmem_limit_bytes=VMEM_LIMIT)


def kernel(x_prompt, x_sample, mem_prompt, cache_k, cache_v, page_table, state_conv, state_ffn_conv, cache_mem_k, cache_mem_v, norm1_g, w_in, b_gate, w_dw_a, b_dw_a, ln_a_g, ln_a_b, w_proj_a, q_norm_g, k_norm_g, w_proj_b, mem_norm_g, w_mem_kv, mq_norm_g, mk_norm_g, w_proj_c, w_out, norm2_g, w_up, w_dw_f, b_dw_f, w_down):
    n, t, d = x_prompt.shape
    nd, td, _ = x_sample.shape
    n_mem = mem_prompt.shape[1]
    n_pool, page_size = cache_k.shape[:2]
    n_pages = page_table.shape[1]
    past_len = n_pages * page_size
    dff = w_down.shape[0]
    nt = t // TM
    n_mix = 2 * CONV_CH + 3 * ATTN_W + MEM_W
    assert t % TM == 0 and nt <= GROUP and td == 1
    assert past_len % MOBA_BLOCK == 0 and MOBA_BLOCK == 2 * page_size and page_size == LANES
    n_blocks = past_len // MOBA_BLOCK

    row = lambda v: v.reshape(1, -1).astype(F32)
    w_in_b = w_in.astype(BF16)
    w_mix, w_gate = w_in_b[:, :n_mix], w_in_b[:, n_mix:]
    wpa, wpb, wpc = w_proj_a.astype(BF16), w_proj_b.astype(BF16), w_proj_c.astype(BF16)
    wout, wup, wdown = w_out.astype(BF16), w_up.astype(BF16), w_down.astype(BF16)
    g1, g2 = row(norm1_g), row(norm2_g)
    qg = row(jnp.tile(q_norm_g, N_HEADS))
    kg = row(jnp.tile(k_norm_g, N_HEADS))
    mqg = row(jnp.tile(mq_norm_g, MEM_HEADS))
    mkg = row(jnp.tile(mk_norm_g, MEM_HEADS))
    bdw_a, lng, lnb, bg, bdw_f = row(b_dw_a), row(ln_a_g), row(ln_a_b), row(b_gate), row(b_dw_f)
    grp = np.arange(ATTN_W)
    bd64 = jnp.asarray((grp[:, None] // HEAD_DIM == grp[None, :] // HEAD_DIM) / HEAD_DIM, BF16)
    bd128 = jnp.asarray((grp[:, None] // MEM_HEAD_DIM == grp[None, :] // MEM_HEAD_DIM) / MEM_HEAD_DIM, BF16)
    slopes = 2.0 ** (-8.0 * np.arange(1, N_HEADS + 1) / N_HEADS)
    aw = N_HEADS * LANES
    coef = {name: np.zeros((N_HEADS, LANES), np.float32) for name in ("qc", "qr", "qt", "kc", "kr", "kt")}
    rest = (slopes * LOG2E).astype(np.float32)
    for p in range(X_PARTS):
        part = _round_to_bf16(rest)
        rest = rest - part
        coef["qr"][:, X_RQ + p] = 1.0
        coef["kc"][:, X_RQ + p] = -part
        coef["qc"][:, X_RK + p] = part
        coef["kr"][:, X_RK + p] = 1.0
        coef["qt"][:, X_TQ + p] = 1.0
        coef["kc"][:, X_TQ + p] = -part * MOBA_BLOCK
        coef["qc"][:, X_TK + p] = part
        coef["kt"][:, X_TK + p] = MOBA_BLOCK
    lanes3 = lambda name: jnp.asarray(coef[name].reshape(1, 1, aw))
    tile_f = jnp.arange(nt, dtype=F32)[:, None, None]
    row_f = jnp.arange(TM, dtype=F32)[None, :, None]
    col = jnp.arange(aw, dtype=jnp.int32)[None, None, :] % LANES
    q_extra = (lanes3("qc") + row_f * lanes3("qr") + tile_f * lanes3("qt")).astype(BF16)
    k_extra = (lanes3("kc") + row_f * lanes3("kr") + tile_f * lanes3("kt")
               + jnp.where(col == X_SEL + tile_f.astype(jnp.int32), MASK_NEG, 0.0)).astype(BF16)

    xp = x_prompt.reshape(n * t, d)
    tile = lambda w: pl.BlockSpec((TM, w), lambda b, i, *_: (b * nt + i, 0))
    aw = N_HEADS * LANES
    vw = N_HEADS * V_ROWS
    blk3 = lambda r, c: pl.BlockSpec((1, r, c), lambda b, i: (b * nt + i, 0, 0))
    k_p, v_p, km, qt, ka, vt, mq, cact, utail = pl.pallas_call(
        _premix_kernel,
        grid=(n, nt),
        in_specs=[tile(d), _resident((1, d)), _resident((d, n_mix)), _resident((ATTN_W, ATTN_W)),
                  _resident((MEM_W, MEM_W)), _resident((1, ATTN_W)), _resident((1, ATTN_W)),
                  _resident((1, MEM_W)), _resident((CONV_WIDTH, CONV_CH)), _resident((1, CONV_CH)),
                  _resident((1, CONV_CH)), _resident((1, CONV_CH)),
                  pl.BlockSpec((1, TM, aw), lambda b, i: (i, 0, 0)),
                  pl.BlockSpec((1, TM, aw), lambda b, i: (i, 0, 0))],
        out_specs=[pl.BlockSpec((ATTN_W, TM), lambda b, i: (b, i)), pl.BlockSpec((ATTN_W, TM), lambda b, i: (b, i)),
                   pl.BlockSpec((1, 1, ATTN_W), lambda b, i: (b * nt + i, 0, 0)),
                   pl.BlockSpec((aw, TM), lambda b, i: (b, i)), blk3(TM, aw), blk3(vw, TM),
                   tile(MEM_W), tile(CONV_CH),
                   pl.BlockSpec((1, CONV_HALO, CONV_CH), lambda b, i: (b, 0, 0))],
        out_shape=[jax.ShapeDtypeStruct((n * ATTN_W, t), F32), jax.ShapeDtypeStruct((n * ATTN_W, t), F32),
                   jax.ShapeDtypeStruct((n * nt, 1, ATTN_W), F32),
                   jax.ShapeDtypeStruct((n * aw, t), BF16), jax.ShapeDtypeStruct((n * nt, TM, aw), BF16),
                   jax.ShapeDtypeStruct((n * nt, vw, TM), BF16), jax.ShapeDtypeStruct((n * t, MEM_W), BF16),
                   jax.ShapeDtypeStruct((n * t, CONV_CH), BF16),
                   jax.ShapeDtypeStruct((n, CONV_HALO, CONV_CH), F32)],
        scratch_shapes=[pltpu.VMEM((CONV_HALO + TM + SUBLANES, CONV_CH), F32)],
        compiler_params=_params(2),
        name="premix",
    )(xp, g1, w_mix, bd64, bd128, qg, kg, mqg, w_dw_a, bdw_a, lng, lnb, q_extra, k_extra)

    km4 = km.reshape(n, nt, N_HEADS, HEAD_DIM).transpose(0, 2, 1, 3)
    km4 = jnp.pad(km4, ((0, 0), (0, 0), (0, GROUP - nt), (0, LANES - HEAD_DIM)))
    kmt = (km4[:, :, :, None, :] * jnp.eye(N_HEADS, dtype=F32)[None, :, None, :, None])
    kmt = kmt.reshape(n, N_HEADS * GROUP, aw).astype(BF16)

    seq3 = lambda r, c: pl.BlockSpec((nt, r, c), lambda b, i: (b, 0, 0))
    ob = pl.pallas_call(
        _moba_kernel,
        grid=(n, nt),
        in_specs=[pl.BlockSpec((aw, TM), lambda b, i: (b, i)), seq3(TM, aw), seq3(vw, TM),
                  pl.BlockSpec((1, N_HEADS * GROUP, aw), lambda b, i: (b, 0, 0))],
        out_specs=tile(ATTN_W),
        out_shape=jax.ShapeDtypeStruct((n * t, ATTN_W), BF16),
        scratch_shapes=[pltpu.VMEM((N_HEADS, LANES, TM), BF16), pltpu.VMEM((N_HEADS, 1, TM), F32),
                        pltpu.VMEM((N_HEADS, V_ROWS, TM), F32)]
        + [pltpu.VMEM((TM, TM), F32)] * N_HEADS + [pltpu.VMEM((1, TM), F32)] * N_HEADS,
        compiler_params=_params(2),
        name="moba",
    )(qt, ka, vt, kmt)

    mem = mem_prompt.reshape(n * n_mem, d)
    mtile = lambda w: pl.BlockSpec((n_mem, w), lambda b: (b, 0))
    mk_p, mv_p, mkb, mvb = pl.pallas_call(
        _memkv_kernel,
        grid=(n,),
        in_specs=[mtile(d), _resident((1, d)), _resident((d, 2 * MEM_W)), _resident((MEM_W, MEM_W)),
                  _resident((1, MEM_W))],
        out_specs=[mtile(MEM_W)] * 4,
        out_shape=[jax.ShapeDtypeStruct((n * n_mem, MEM_W), F32)] * 2
        + [jax.ShapeDtypeStruct((n * n_mem, MEM_W), BF16)] * 2,
        compiler_params=_params(1),
        name="memkv",
    )(mem, row(mem_norm_g), w_mem_kv.astype(BF16), bd128, mkg)

    xs = x_sample.reshape(nd, d)
    st_conv = state_conv.transpose(1, 0, 2)
    st_ffn = state_ffn_conv.transpose(1, 0, 2)
    vm = pltpu.CompilerParams(vmem_limit_bytes=VMEM_LIMIT)
    u_s, q_s, k_s, v_s, mq_s, gl_s, cact_s = pl.pallas_call(
        _premix_s_kernel,
        out_shape=[jax.ShapeDtypeStruct((nd, CONV_CH), F32), jax.ShapeDtypeStruct((nd, ATTN_W), F32),
                   jax.ShapeDtypeStruct((nd, ATTN_W), F32), jax.ShapeDtypeStruct((nd, ATTN_W), F32),
                   jax.ShapeDtypeStruct((nd, MEM_W), F32), jax.ShapeDtypeStruct((nd, 3 * d), F32),
                   jax.ShapeDtypeStruct((nd, CONV_CH), BF16)],
        compiler_params=vm,
        name="premix_s",
    )(xs, g1, w_in_b, bd64, bd128, qg, kg, mqg, st_conv, w_dw_a, bdw_a, lng, lnb)

    ck = cache_k.transpose(0, 2, 3, 1)
    cv = cache_v.transpose(0, 2, 3, 1)
    pt_flat = page_table.reshape(-1).astype(jnp.int32)
    q_cols = jnp.broadcast_to(q_s.reshape(nd, N_HEADS, HEAD_DIM, 1), (nd, N_HEADS, HEAD_DIM, page_size))
    assert (n * nt) % nd == 0 and n_pages % ((n * nt) // nd) == 0 and n_blocks <= LANES
    steps_per_sample = (n * nt) // nd
    pages_per_step = n_pages // steps_per_sample
    assert pages_per_step % (MOBA_BLOCK // page_size) == 0

    memb = pl.BlockSpec((n_mem, MEM_W), lambda b, i, *_: (b, 0))
    x1, sel = pl.pallas_call(
        functools.partial(_postmix_kernel, pages_per_step=pages_per_step, n_blocks=n_blocks),
        grid_spec=pltpu.PrefetchScalarGridSpec(
            num_scalar_prefetch=1,
            grid=(n, nt),
            in_specs=[tile(d), tile(CONV_CH), tile(ATTN_W), tile(MEM_W), memb, memb, _resident((1, d)),
                      _resident((d, 3 * d)), _resident((1, 3 * d)), _resident((CONV_CH, d)),
                      _resident((ATTN_W, d)), _resident((MEM_W, d)), _resident((d, d)),
                      pl.BlockSpec((1, N_HEADS, HEAD_DIM, page_size),
                                   lambda b, i, pt: ((b * nt + i) // steps_per_sample, 0, 0, 0)),
                      pl.BlockSpec(memory_space=pl.ANY)],
            out_specs=[tile(d), pl.BlockSpec((1, N_HEADS, LANES),
                                             lambda b, i, pt: ((b * nt + i) // steps_per_sample, 0, 0))],
            scratch_shapes=[pltpu.VMEM((N_HEADS, LANES), F32), pltpu.VMEM((TM, d), BF16),
                            pltpu.VMEM((2, pages_per_step, N_HEADS, HEAD_DIM, page_size), F32),
                            pltpu.SemaphoreType.DMA((2,))]),
        out_shape=[jax.ShapeDtypeStruct((n * t, d), F32), jax.ShapeDtypeStruct((nd, N_HEADS, LANES), jnp.int32)],
        compiler_params=_params(2),
        name="postmix",
    )(pt_flat, xp, cact, ob, mq, mkb, mvb, g1, w_gate, bg, wpa, wpb, wpc, wout, q_cols, ck)
    sel_flat = sel[:, :, :MOBA_TOPK].reshape(-1)

    chunk = dff // 2
    assert chunk % LANES == 0 and t % FFN_ROWS == 0
    nf = t // FFN_ROWS
    ftile = pl.BlockSpec((FFN_ROWS, d), lambda b, i: (b * nf + i, 0))
    y_p, ftail = pl.pallas_call(
        functools.partial(_ffn_kernel, chunk=chunk),
        grid=(n, nf),
        in_specs=[ftile, _resident((1, d)), _resident((d, 2 * dff)), _resident((FFN_CONV_WIDTH, 2 * dff)),
                  _resident((1, 2 * dff)), _resident((dff, d))],
        out_specs=[ftile, pl.BlockSpec((1, SUBLANES, 2 * dff), lambda b, i: (b, 0, 0))],
        out_shape=[jax.ShapeDtypeStruct((n * t, d), F32), jax.ShapeDtypeStruct((n, SUBLANES, 2 * dff), F32)],
        scratch_shapes=[pltpu.VMEM((SUBLANES + FFN_ROWS, 2 * dff), F32)],
        compiler_params=_params(2),
        name="ffn",
    )(x1, g2, wup, w_dw_f, bdw_f, wdown)

    row_spec = pl.BlockSpec((1, 1, ATTN_W), lambda b, *_: (b, 0, 0))
    rows3 = lambda a: a.reshape(nd, 1, -1)

    halves = jnp.arange(PAGES_PER_BLOCK, dtype=jnp.int32)
    sel_pages = PAGES_PER_BLOCK * sel[:, :, :MOBA_TOPK, None] + halves
    page_ids = jnp.take_along_axis(page_table.astype(jnp.int32), sel_pages.reshape(nd, -1), axis=1).reshape(-1)
    tiles_per_sample = N_HEADS * MOBA_TOPK * PAGES_PER_BLOCK
    tile_buf = pltpu.VMEM((2, tiles_per_sample, HEAD_DIM, page_size), F32)
    ob_s = pl.pallas_call(
        functools.partial(_decode_attn_kernel, past_len=past_len),
        grid_spec=pltpu.PrefetchScalarGridSpec(
            num_scalar_prefetch=2,
            grid=(nd,),
            in_specs=[pl.BlockSpec(memory_space=pltpu.SMEM), row_spec, row_spec, row_spec,
                      pl.BlockSpec(memory_space=pl.ANY), pl.BlockSpec(memory_space=pl.ANY)],
            out_specs=row_spec,
            scratch_shapes=[tile_buf, tile_buf, pltpu.SemaphoreType.DMA((2, 2))]),
        out_shape=jax.ShapeDtypeStruct((nd, 1, ATTN_W), F32),
        compiler_params=_params(1),
        name="decode_attn",
    )(page_ids, sel_flat, jnp.asarray(slopes, F32), rows3(q_s), rows3(k_s), rows3(v_s), ck, cv)
    ob_s = ob_s.reshape(nd, ATTN_W)

    group = SUBLANES
    assert nd % group == 0
    cm_spec = pl.BlockSpec((group, n_mem * MEM_HEADS, MEM_HEAD_DIM), lambda b: (b, 0, 0))
    mq_spec = pl.BlockSpec((group, 1, MEM_W), lambda b: (b, 0, 0))
    oc_s = pl.pallas_call(
        functools.partial(_memattn_s_kernel, group=group, n_mem=n_mem),
        grid=(nd // group,),
        in_specs=[mq_spec, cm_spec, cm_spec],
        out_specs=mq_spec,
        out_shape=jax.ShapeDtypeStruct((nd, 1, MEM_W), F32),
        compiler_params=_params(1),
        name="memattn_s",
    )(rows3(mq_s), cache_mem_k.reshape(nd, n_mem * MEM_HEADS, MEM_HEAD_DIM),
      cache_mem_v.reshape(nd, n_mem * MEM_HEADS, MEM_HEAD_DIM))
    oc_s = oc_s.reshape(nd, MEM_W)

    y_s, up_s = pl.pallas_call(
        _post_s_kernel,
        out_shape=[jax.ShapeDtypeStruct((nd, d), F32), jax.ShapeDtypeStruct((nd, 2 * dff), F32)],
        compiler_params=vm,
        name="post_s",
    )(xs, cact_s, ob_s, oc_s, gl_s, bg, wpa, wpb, wpc, wout, g2, wup, st_ffn, w_dw_f, bdw_f, wdown)

    heads = lambda a, b, s: a.reshape(b, s, N_HEADS, HEAD_DIM)
    from_t = lambda a: a.reshape(n, N_HEADS, HEAD_DIM, t).transpose(0, 3, 1, 2)
    conv_p = utail[:, CONV_HALO - (CONV_WIDTH - 1):, :]
    conv_s = jnp.concatenate([state_conv[:, 1:, :], u_s[:, None, :]], axis=1)
    ffn_p = ftail[:, SUBLANES - (FFN_CONV_WIDTH - 1):, :]
    ffn_s = jnp.concatenate([state_ffn_conv[:, 1:, :], up_s[:, None, :]], axis=1)
    return (y_p.reshape(n, t, d), y_s.reshape(nd, td, d),
            from_t(k_p), from_t(v_p), heads(k_s, nd, td), heads(v_s, nd, td),
            conv_p, conv_s, ffn_p, ffn_s,
            mk_p.reshape(n, n_mem, MEM_HEADS, MEM_HEAD_DIM), mv_p.reshape(n, n_mem, MEM_HEADS, MEM_HEAD_DIM))
```

```python
import functools

import numpy as np
import jax
import jax.numpy as jnp
from jax import lax
from jax.experimental import pallas as pl
from jax.experimental.pallas import tpu as pltpu

F32 = jnp.float32
BF16 = jnp.bfloat16

EPS = 1e-6
CONV_CH = 512
CONV_WIDTH = 31
N_HEADS = 8
HEAD_DIM = 64
ATTN_W = N_HEADS * HEAD_DIM
MOBA_BLOCK = 256
MOBA_TOPK = 3
MEM_HEADS = 4
MEM_HEAD_DIM = 128
MEM_W = MEM_HEADS * MEM_HEAD_DIM
FFN_CONV_WIDTH = 3
LANES = 128
SUBLANES = 8
TM = MOBA_BLOCK
CONV_HALO = 32
MASK_NEG = -float(2 ** 30)
GROUP = 16
V_ROWS = HEAD_DIM + 16
BLOCKS_PER_TRIP = 4
FFN_ROWS = 512
VMEM_LIMIT = 56 * 1024 * 1024

X_SEL = HEAD_DIM
X_PARTS = 3
X_RQ = HEAD_DIM + GROUP
X_RK = X_RQ + X_PARTS
X_TQ = X_RK + X_PARTS
X_TK = X_TQ + X_PARTS
LOG2E = 1.4426950408889634


def _dot(a, b):
    return jnp.dot(a, b, preferred_element_type=F32)


def _dot_nt(a, b):
    return lax.dot_general(a, b, (((1,), (1,)), ((), ())), preferred_element_type=F32)


def _rms_rows(x, g):
    return x * lax.rsqrt(jnp.mean(x * x, axis=-1, keepdims=True) + EPS) * g


def _group_rms(z, bd, g):
    sq = z * z
    hi = sq.astype(BF16)
    lo = (sq - hi.astype(F32)).astype(BF16)
    ms = _dot(hi, bd) + _dot(lo, bd)
    return z * lax.rsqrt(ms + EPS) * g


def _exact_zero(v):
    bits = pltpu.bitcast(v, jnp.uint32)
    half = jnp.uint32(16)
    return pltpu.bitcast(lax.shift_right_logical(lax.shift_right_logical(bits, half), half), F32)


def _sigmoid(x):
    return 1.0 / (1.0 + jnp.exp(-x))


def _layernorm_silu(c, g, b):
    mu = jnp.mean(c, axis=-1, keepdims=True)
    xc = c - mu
    var = jnp.mean(xc * xc, axis=-1, keepdims=True)
    y = xc * lax.rsqrt(var + EPS) * g + b
    return y * _sigmoid(y)


def _premix_kernel(x_ref, g1_ref, w_ref, bd64_ref, bd128_ref, qg_ref, kg_ref, mqg_ref,
                   wdw_ref, bdw_ref, lng_ref, lnb_ref, qx_ref, kx_ref,
                   k_ref, v_ref, km_ref, qt_ref, ka_ref, vt_ref, mq_ref, cact_ref, utail_ref,
                   ubuf):
    t = pl.program_id(1)
    nt = pl.num_programs(1)
    c = CONV_CH

    @pl.when(t == 0)
    def _():
        ubuf[0:CONV_HALO, :] = jnp.zeros((CONV_HALO, c), F32)
        ubuf[CONV_HALO + TM:, :] = jnp.zeros((SUBLANES, c), F32)

    hn = _rms_rows(x_ref[...], g1_ref[...]).astype(BF16)

    a = _dot(hn, w_ref[:, 0:c])
    g = _dot(hn, w_ref[:, c:2 * c])
    u = a * _sigmoid(g)
    ubuf[CONV_HALO:CONV_HALO + TM, :] = u
    o = 2 * c
    zq = _dot(hn, w_ref[:, o:o + ATTN_W])
    zk = _dot(hn, w_ref[:, o + ATTN_W:o + 2 * ATTN_W])
    zv = _dot(hn, w_ref[:, o + 2 * ATTN_W:o + 3 * ATTN_W])
    zm = _dot(hn, w_ref[:, o + 3 * ATTN_W:o + 3 * ATTN_W + MEM_W])
    acc = jnp.broadcast_to(bdw_ref[...], (TM, c))
    base = CONV_HALO - (CONV_WIDTH - 1)
    span = TM + 2 * SUBLANES
    for b in range(SUBLANES):
        part = None
        for k in range(b, CONV_WIDTH, SUBLANES):
            term = wdw_ref[k:k + 1, :] * ubuf[k - b:k - b + span, :]
            part = term if part is None else part + term
        acc = acc + part[base + b:base + b + TM, :]
    ubuf[0:CONV_HALO, :] = ubuf[TM:TM + CONV_HALO, :]
    cact_ref[...] = _layernorm_silu(acc, lng_ref[...], lnb_ref[...]).astype(BF16)

    qn = _group_rms(zq, bd64_ref[...], qg_ref[...]) * (HEAD_DIM ** -0.5)
    kn = _group_rms(zk, bd64_ref[...], kg_ref[...])
    mq = _group_rms(zm, bd128_ref[...], mqg_ref[...])

    ones_rows = jnp.where(lax.broadcasted_iota(jnp.int32, (V_ROWS - HEAD_DIM, TM), 0) == 0, 1.0, 0.0).astype(BF16)
    for cb in range(ATTN_W // LANES):
        k_ref[LANES * cb:LANES * (cb + 1), :] = kn[:, LANES * cb:LANES * (cb + 1)].T
        vt = zv[:, LANES * cb:LANES * (cb + 1)].T
        v_ref[LANES * cb:LANES * (cb + 1), :] = vt
        for sub in range(2):
            r0 = V_ROWS * (2 * cb + sub)
            vt_ref[0, r0:r0 + HEAD_DIM, :] = vt[HEAD_DIM * sub:HEAD_DIM * (sub + 1), :].astype(BF16)
            vt_ref[0, r0 + HEAD_DIM:r0 + V_ROWS, :] = ones_rows
    mq_ref[...] = mq.astype(BF16)
    km_ref[0] = jnp.mean(kn, axis=0, keepdims=True)

    lane = lax.broadcasted_iota(jnp.int32, (TM, LANES), 1)
    for hp in range(N_HEADS // 2):
        xq = qn[:, LANES * hp:LANES * (hp + 1)]
        xk = kn[:, LANES * hp:LANES * (hp + 1)] * LOG2E
        for sub in range(2):
            h = 2 * hp + sub
            if sub == 1:
                xq = pltpu.roll(xq, HEAD_DIM, 1)
                xk = pltpu.roll(xk, HEAD_DIM, 1)
            eq = qx_ref[0, :, LANES * h:LANES * (h + 1)].astype(F32)
            ek = kx_ref[0, :, LANES * h:LANES * (h + 1)].astype(F32)
            qt_ref[LANES * h:LANES * (h + 1), :] = jnp.where(lane < HEAD_DIM, xq, eq).T.astype(BF16)
            ka_ref[0, :, LANES * h:LANES * (h + 1)] = jnp.where(lane < HEAD_DIM, xk, ek).astype(BF16)

    @pl.when(t == nt - 1)
    def _():
        utail_ref[0] = ubuf[0:CONV_HALO, :]


def _moba_kernel(qt_ref, ka_ref, vt_ref, kmt_ref, o_ref, qh_scr, m_scr, acc_scr, *stage):
    i = pl.program_id(1)
    s_scr, mx_scr = stage[:N_HEADS], stage[N_HEADS:]
    key = lax.broadcasted_iota(jnp.int32, (TM, TM), 0)
    qry = lax.broadcasted_iota(jnp.int32, (TM, TM), 1)
    causal = key <= qry

    def scores(j, h, diagonal):
        q = qt_ref[LANES * h:LANES * (h + 1), :] if diagonal else qh_scr[h]
        s = _dot(ka_ref[j, :, LANES * h:LANES * (h + 1)], q)
        if diagonal:
            s = jnp.where(causal, s, -jnp.inf)
        s_scr[h][...] = s
        mx_scr[h][...] = jnp.max(s, axis=0, keepdims=True)

    for h in range(N_HEADS):
        scores(i, h, True)

    blk = lax.broadcasted_iota(jnp.int32, (N_HEADS, GROUP, TM), 1)
    past = blk < i
    gate = _dot(kmt_ref[0], qt_ref[...]).reshape(N_HEADS, GROUP, TM)
    g = jnp.where(past, gate, -jnp.inf)
    unselected = jnp.where(past, 1.0, 0.0)
    for _ in range(MOBA_TOPK):
        best = jnp.max(g, axis=1, keepdims=True)
        first = jnp.min(jnp.where(g == best, blk, GROUP), axis=1, keepdims=True)
        taken = blk == first
        g = jnp.where(taken, -jnp.inf, g)
        unselected = jnp.where(taken, 0.0, unselected)
    nsel = unselected.astype(BF16).reshape(N_HEADS * GROUP, TM)
    for h in range(N_HEADS):
        r0 = LANES * h
        qh_scr[h, 0:X_SEL, :] = qt_ref[r0:r0 + X_SEL, :]
        qh_scr[h, X_SEL:X_RQ, :] = nsel[GROUP * h:GROUP * (h + 1), :]
        qh_scr[h, X_RQ:LANES, :] = qt_ref[r0 + X_RQ:r0 + LANES, :]

    def accumulate(j, h):
        m_old = m_scr[h]
        m_new = jnp.maximum(m_old, mx_scr[h][...])
        alpha = jnp.exp2(m_old - m_new)
        p = jnp.exp2(s_scr[h][...] - m_new).astype(BF16)
        acc_scr[h] = alpha * acc_scr[h] + _dot(vt_ref[j, V_ROWS * h:V_ROWS * (h + 1), :], p)
        m_scr[h] = m_new

    m_scr[...] = jnp.full(m_scr.shape, -jnp.inf, F32)
    acc_scr[...] = jnp.zeros(acc_scr.shape, F32)

    def step(prev, j):
        for h in range(N_HEADS):
            accumulate(prev, h)
            scores(j, h, False)

    def steps(first, count):
        step(jnp.where(first == 0, i, first - 1), first)
        for extra in range(1, count):
            step(first + extra - 1, first + extra)

    def body(trip, carry):
        steps(trip * BLOCKS_PER_TRIP, BLOCKS_PER_TRIP)
        return carry

    lax.fori_loop(0, i // BLOCKS_PER_TRIP, body, 0)
    size = BLOCKS_PER_TRIP // 2
    while size:
        done = (i // (2 * size)) * (2 * size)

        @pl.when((i // size) % 2 == 1)
        def _(done=done, size=size):
            steps(done, size)

        size //= 2
    last = jnp.where(i == 0, i, i - 1)
    for h in range(N_HEADS):
        accumulate(last, h)
    outs = []
    for h in range(N_HEADS):
        acc = acc_scr[h]
        outs.append(acc[0:HEAD_DIM, :] / acc[HEAD_DIM:HEAD_DIM + 1, :])
    o_ref[...] = jnp.concatenate(outs, axis=0).T.astype(BF16)


def _memkv_kernel(mem_ref, g_ref, w_ref, bd128_ref, mkg_ref, mk_ref, mv_ref, mkb_ref, mvb_ref):
    hn = _rms_rows(mem_ref[...], g_ref[...]).astype(BF16)
    mk = _group_rms(_dot(hn, w_ref[:, 0:MEM_W]), bd128_ref[...], mkg_ref[...])
    mv = _dot(hn, w_ref[:, MEM_W:2 * MEM_W])
    mk_ref[...] = mk
    mv_ref[...] = mv
    mkb_ref[...] = mk.astype(BF16)
    mvb_ref[...] = mv.astype(BF16)


def _mem_attend_rows(mq, mk, mv):
    heads = [slice(MEM_HEAD_DIM * hh, MEM_HEAD_DIM * (hh + 1)) for hh in range(MEM_HEADS)]
    scores = [_dot_nt(mq[:, sl], mk[:, sl]) * (MEM_HEAD_DIM ** -0.5) for sl in heads]
    probs = []
    for s in scores:
        p = jnp.exp(s - jnp.max(s, axis=-1, keepdims=True))
        probs.append((p.astype(BF16), jnp.sum(p, axis=-1, keepdims=True)))
    return jnp.concatenate([_dot(p, mv[:, sl]) / l for (p, l), sl in zip(probs, heads)], axis=-1)


def _merge_out(x, gl, ya, yb, yc, wout):
    d = x.shape[-1]
    merged = (_sigmoid(gl[:, 0:d]) * ya + _sigmoid(gl[:, d:2 * d]) * yb + _sigmoid(gl[:, 2 * d:3 * d]) * yc)
    return x + _dot(merged.astype(BF16), wout)


def _postmix_kernel(pt_ref, x_ref, cact_ref, ob_ref, mq_ref, mk_ref, mv_ref, g1_ref, wg_ref, bg_ref,
                    wpa_ref, wpb_ref, wpc_ref, wout_ref, q_ref, ck_hbm, x1_ref, sel_ref,
                    gate_scr, page_buf, page_sem, *, pages_per_step, n_blocks):
    step = pl.program_id(0) * pl.num_programs(1) + pl.program_id(1)
    n_steps = pl.num_programs(0) * pl.num_programs(1)
    slot = step % 2

    def page_copy(page, to_slot, r):
        return pltpu.make_async_copy(ck_hbm.at[page], page_buf.at[to_slot, r], page_sem.at[to_slot])

    def fetch(for_step, to_slot):
        for r in range(pages_per_step):
            page_copy(pt_ref[for_step * pages_per_step + r], to_slot, r).start()

    @pl.when(step == 0)
    def _():
        gate_scr[...] = jnp.zeros(gate_scr.shape, F32)
        fetch(0, 0)

    for r in range(pages_per_step):
        page_copy(0, slot, r).wait()
    page_refs = [page_buf.at[slot, r] for r in range(pages_per_step)]

    blocks_per_step = pages_per_step // PAGES_PER_BLOCK
    steps_per_sample = n_blocks // blocks_per_step
    share = step % steps_per_sample

    g = jnp.where(share == 0, 0.0, gate_scr[...])
    x = x_ref[...]
    d = x.shape[-1]
    hn = _rms_rows(x, g1_ref[...]).astype(BF16)
    oc = _mem_attend_rows(mq_ref[...], mk_ref[...], mv_ref[...]).astype(BF16)
    fetch(jnp.minimum(step + 1, n_steps - 1), 1 - slot)
    g = _gate_scores(g, share * blocks_per_step, q_ref, page_refs)
    tie = jnp.concatenate([_exact_zero(g)[0:1, :]] * (d // LANES), axis=1)
    sources = ((cact_ref[...], wpa_ref), (ob_ref[...], wpb_ref), (oc, wpc_ref))
    ys = [_dot(src, w_ref[...]) for src, w_ref in sources]
    merged = None
    for br in range(len(sources)):
        cols = slice(d * br, d * (br + 1))
        gl = _dot(hn, wg_ref[:, cols]) + (bg_ref[:, cols] + tie)
        term = _sigmoid(gl) * ys[br]
        merged = term if merged is None else merged + term
    x1_ref[...] = x + _dot(merged.astype(BF16), wout_ref[...])
    gate_scr[...] = g
    _gate_select(share == steps_per_sample - 1, gate_scr, sel_ref, n_blocks)

    @pl.when(step == n_steps - 1)
    def _():
        for r in range(pages_per_step):
            page_copy(0, 1 - slot, r).wait()


def _ffn_kernel(x1_ref, g2_ref, wup_ref, wdw_ref, bdw_ref, wdown_ref, y_ref, tail_ref, upbuf, *, chunk):
    rows = x1_ref.shape[0]
    t = pl.program_id(1)
    nt = pl.num_programs(1)
    dff = wdown_ref.shape[0]

    @pl.when(t == 0)
    def _():
        upbuf[0:SUBLANES, :] = jnp.zeros((SUBLANES, 2 * dff), F32)

    x1 = x1_ref[...]
    hn = _rms_rows(x1, g2_ref[...]).astype(BF16)

    for c in range(0, 2 * dff, chunk):
        upbuf[SUBLANES:SUBLANES + rows, c:c + chunk] = _dot(hn, wup_ref[:, c:c + chunk])

    def conv(c):
        out = bdw_ref[:, c:c + chunk]
        for k in range(FFN_CONV_WIDTH):
            r0 = SUBLANES - (FFN_CONV_WIDTH - 1) + k
            out = out + wdw_ref[k:k + 1, c:c + chunk] * upbuf[r0:r0 + rows, c:c + chunk]
        return out

    y = x1
    for c in range(0, dff, chunk):
        a = conv(c)
        b = conv(dff + c)
        act = (a * _sigmoid(a) * b).astype(BF16)
        y = y + _dot(act, wdown_ref[c:c + chunk, :])
    y_ref[...] = y

    @pl.when(t == nt - 1)
    def _():
        tail_ref[0] = upbuf[rows:rows + SUBLANES, :]

    upbuf[0:SUBLANES, :] = upbuf[rows:rows + SUBLANES, :]


def _premix_s_kernel(x_ref, g1_ref, w_ref, bd64_ref, bd128_ref, qg_ref, kg_ref, mqg_ref,
                     st_ref, wdw_ref, bdw_ref, lng_ref, lnb_ref,
                     u_ref, q_ref, k_ref, v_ref, mq_ref, gl_ref, cact_ref):
    c = CONV_CH
    hn = _rms_rows(x_ref[...], g1_ref[...]).astype(BF16)
    a = _dot(hn, w_ref[:, 0:c])
    g = _dot(hn, w_ref[:, c:2 * c])
    u = a * _sigmoid(g)
    u_ref[...] = u
    acc = bdw_ref[...] + wdw_ref[CONV_WIDTH - 1:CONV_WIDTH, :] * u
    for k in range(CONV_WIDTH - 1):
        acc = acc + wdw_ref[k:k + 1, :] * st_ref[k]
    cact_ref[...] = _layernorm_silu(acc, lng_ref[...], lnb_ref[...]).astype(BF16)

    o = 2 * c
    q_ref[...] = _group_rms(_dot(hn, w_ref[:, o:o + ATTN_W]), bd64_ref[...], qg_ref[...])
    o += ATTN_W
    k_ref[...] = _group_rms(_dot(hn, w_ref[:, o:o + ATTN_W]), bd64_ref[...], kg_ref[...])
    o += ATTN_W
    v_ref[...] = _dot(hn, w_ref[:, o:o + ATTN_W])
    o += ATTN_W
    mq_ref[...] = _group_rms(_dot(hn, w_ref[:, o:o + MEM_W]), bd128_ref[...], mqg_ref[...])
    o += MEM_W
    gl_ref[...] = _dot(hn, w_ref[:, o:])


PAGES_PER_BLOCK = MOBA_BLOCK // LANES


def _gate_scores(g, first_block, q_ref, page_refs):
    head = lax.broadcasted_iota(jnp.int32, (N_HEADS, LANES), 0)
    lane = lax.broadcasted_iota(jnp.int32, (HEAD_DIM, LANES), 1)
    for h in range(N_HEADS):
        kmean = jnp.zeros((HEAD_DIM, LANES), F32)
        for r in range(0, len(page_refs), PAGES_PER_BLOCK):
            ksum = page_refs[r][h]
            for extra in range(1, PAGES_PER_BLOCK):
                ksum = ksum + page_refs[r + extra][h]
            col = jnp.sum(ksum, axis=1, keepdims=True) * (1.0 / MOBA_BLOCK)
            kmean = jnp.where(lane == first_block + r // PAGES_PER_BLOCK, col, kmean)
        prod = q_ref[0, h].astype(BF16).astype(F32) * kmean.astype(BF16).astype(F32)
        g = g + jnp.where(head == h, jnp.sum(prod, axis=0, keepdims=True), 0.0)
    return g


def _gate_select(last, gate_scr, sel_ref, n_blocks):
    @pl.when(last)
    def _():
        g = gate_scr[:, 0:n_blocks]
        bl = lax.broadcasted_iota(jnp.int32, (N_HEADS, n_blocks), 1)
        rank = jnp.zeros((N_HEADS, n_blocks), F32)
        for b in range(n_blocks):
            other = g[:, b:b + 1]
            rank = rank + jnp.where(bl > b, jnp.where(other >= g, 1.0, 0.0), jnp.where(other > g, 1.0, 0.0))
        lane_o = lax.broadcasted_iota(jnp.int32, (N_HEADS, LANES), 1)
        out = jnp.zeros((N_HEADS, LANES), F32)
        blf = bl.astype(F32)
        for r in range(MOBA_TOPK):
            idx = jnp.sum(jnp.where(rank == float(r), blf, 0.0), axis=-1, keepdims=True)
            out = jnp.where(lane_o == r, idx, out)
        sel_ref[0] = out.astype(jnp.int32)


def _decode_attn_kernel(page_ref, sel_ref, slope_ref, q_ref, kown_ref, vown_ref, ck_hbm, cv_hbm, o_ref,
                        k_buf, v_buf, sem, *, past_len):
    pages_per_block = MOBA_BLOCK // LANES
    tiles_per_head = MOBA_TOPK * pages_per_block
    n_tiles = N_HEADS * tiles_per_head
    n = pl.program_id(0)
    slot = n % 2

    def tile_copies(page, to_slot, tile):
        h = tile // tiles_per_head
        return (pltpu.make_async_copy(ck_hbm.at[page, h], k_buf.at[to_slot, tile], sem.at[0, to_slot]),
                pltpu.make_async_copy(cv_hbm.at[page, h], v_buf.at[to_slot, tile], sem.at[1, to_slot]))

    def fetch(sample, to_slot):
        for tile in range(n_tiles):
            for copy in tile_copies(page_ref[sample * n_tiles + tile], to_slot, tile):
                copy.start()

    @pl.when(n == 0)
    def _():
        fetch(0, 0)

    for tile in range(n_tiles):
        for copy in tile_copies(0, slot, tile):
            copy.wait()
    k_refs = [k_buf.at[slot, tile] for tile in range(n_tiles)]
    v_refs = [v_buf.at[slot, tile] for tile in range(n_tiles)]

    lane = lax.broadcasted_iota(jnp.int32, (1, LANES), 1)
    lanef = lane.astype(F32)
    zeros = jnp.zeros((HEAD_DIM, tiles_per_head * LANES), BF16)
    scale = HEAD_DIM ** -0.5

    def head_tiles(tile_refs, h):
        t = jnp.concatenate([tile_refs[h * tiles_per_head + idx][...] for idx in range(tiles_per_head)], axis=1)
        t = t.astype(BF16)
        return jnp.concatenate([t, zeros] if h % 2 == 0 else [zeros, t], axis=0)

    def own_half(h):
        return (lane < HEAD_DIM) if h % 2 == 0 else (lane >= HEAD_DIM)

    def pair_lanes(ref, h):
        return ref[0][:, LANES * (h // 2):LANES * (h // 2 + 1)]

    qz, raw = [], []
    for h in range(N_HEADS):
        qz.append(jnp.where(own_half(h), pair_lanes(q_ref, h) * scale, 0.0).astype(BF16))
        raw.append(_dot(jnp.broadcast_to(qz[h], (SUBLANES, LANES)), head_tiles(k_refs, h))[0:1, :])
    last = pl.num_programs(0) - 1
    fetch(jnp.minimum(n + 1, last), 1 - slot)
    probs = []
    for h in range(N_HEADS):
        dist = []
        for r in range(MOBA_TOPK):
            blk = sel_ref[(n * N_HEADS + h) * MOBA_TOPK + r]
            for half in range(pages_per_block):
                pos0 = (blk * MOBA_BLOCK + half * LANES).astype(F32)
                dist.append(float(past_len) - (pos0 + lanef))
        logits = raw[h] - slope_ref[h] * jnp.concatenate(dist, axis=1)
        kown = pair_lanes(kown_ref, h).astype(BF16).astype(F32)
        s_own = jnp.sum(qz[h].astype(F32) * kown, axis=-1, keepdims=True)
        m = jnp.maximum(s_own, jnp.max(logits, axis=-1, keepdims=True))
        p_own = jnp.exp(s_own - m)
        p = jnp.exp(logits - m)
        probs.append((p, p_own, p_own + jnp.sum(p, axis=-1, keepdims=True)))
    outs = []
    for h in range(N_HEADS):
        p, p_own, l = probs[h]
        p8 = jnp.broadcast_to(p.astype(BF16), (SUBLANES, p.shape[1]))
        vown = jnp.where(own_half(h), pair_lanes(vown_ref, h).astype(BF16).astype(F32), 0.0)
        acc = p_own.astype(BF16).astype(F32) * vown + _dot_nt(p8, head_tiles(v_refs, h))[0:1, :]
        outs.append(acc / l)
    o_ref[0] = jnp.concatenate([outs[h] + outs[h + 1] for h in range(0, N_HEADS, 2)], axis=1)

    @pl.when(n == last)
    def _():
        for tile in range(n_tiles):
            for copy in tile_copies(0, 1 - slot, tile):
                copy.wait()


def _memattn_s_kernel(mq_ref, mk_ref, mv_ref, o_ref, *, group, n_mem):
    units = [(s, hh) for s in range(group) for hh in range(MEM_HEADS)]
    scores = []
    for s, hh in units:
        mq = jnp.broadcast_to(mq_ref[s][:, MEM_HEAD_DIM * hh:MEM_HEAD_DIM * (hh + 1)], (SUBLANES, MEM_HEAD_DIM))
        mk = mk_ref[s, pl.ds(hh, n_mem, stride=MEM_HEADS), :].astype(BF16)
        scores.append(_dot_nt(mq.astype(BF16), mk)[0:1, :] * (MEM_HEAD_DIM ** -0.5))
    probs = []
    for sc in scores:
        p = jnp.exp(sc - jnp.max(sc, axis=-1, keepdims=True))
        probs.append((p, jnp.sum(p, axis=-1, keepdims=True)))
    outs = {}
    for (s, hh), (p, l) in zip(units, probs):
        mv = mv_ref[s, pl.ds(hh, n_mem, stride=MEM_HEADS), :].astype(BF16)
        p8 = jnp.broadcast_to(p.astype(BF16), (SUBLANES, n_mem))
        outs[s, hh] = _dot(p8, mv)[0:1, :] / l
    for s in range(group):
        o_ref[s] = jnp.concatenate([outs[s, hh] for hh in range(MEM_HEADS)], axis=-1)


def _post_s_kernel(x_ref, cact_ref, ob_ref, oc_ref, gl_ref, bg_ref, wpa_ref, wpb_ref, wpc_ref, wout_ref,
                   g2_ref, wup_ref, st_ref, wdw_ref, bdw_ref, wdown_ref, y_ref, up_ref):
    x = x_ref[...]
    dff = wdown_ref.shape[0]
    ya = _dot(cact_ref[...], wpa_ref[...])
    yb = _dot(ob_ref[...].astype(BF16), wpb_ref[...])
    yc = _dot(oc_ref[...].astype(BF16), wpc_ref[...])
    x1 = _merge_out(x, gl_ref[...] + bg_ref[...], ya, yb, yc, wout_ref[...])
    up = _dot(_rms_rows(x1, g2_ref[...]).astype(BF16), wup_ref[...])
    up_ref[...] = up
    cv = bdw_ref[...] + wdw_ref[FFN_CONV_WIDTH - 1:FFN_CONV_WIDTH, :] * up
    for k in range(FFN_CONV_WIDTH - 1):
        cv = cv + wdw_ref[k:k + 1, :] * st_ref[k]
    a = cv[:, 0:dff]
    b = cv[:, dff:]
    act = (a * _sigmoid(a) * b).astype(BF16)
    y_ref[...] = x1 + _dot(act, wdown_ref[...])


def _round_to_bf16(x):
    bits = np.asarray(x, np.float32).view(np.uint32)
    bits = (bits + np.uint32(0x7FFF) + ((bits >> np.uint32(16)) & np.uint32(1))) & np.uint32(0xFFFF0000)
    return bits.view(np.float32)


def _resident(shape):
    nd = len(shape)
    return pl.BlockSpec(shape, lambda *_: (0,) * nd, pipeline_mode=pl.Buffered(1))


def _params(n_axes):
    return pltpu.CompilerParams(dimension_semantics=("arbitrary",) * n_axes, vmem_limit_bytes=VMEM_LIMIT)


def kernel(x_prompt, x_sample, mem_prompt, cache_k, cache_v, page_table, state_conv, state_ffn_conv, cache_mem_k, cache_mem_v, norm1_g, w_in, b_gate, w_dw_a, b_dw_a, ln_a_g, ln_a_b, w_proj_a, q_norm_g, k_norm_g, w_proj_b, mem_norm_g, w_mem_kv, mq_norm_g, mk_norm_g, w_proj_c, w_out, norm2_g, w_up, w_dw_f, b_dw_f, w_down):
    n, t, d = x_prompt.shape
    nd, td, _ = x_sample.shape
    n_mem = mem_prompt.shape[1]
    n_pool, page_size = cache_k.shape[:2]
    n_pages = page_table.shape[1]
    past_len = n_pages * page_size
    dff = w_down.shape[0]
    nt = t // TM
    n_mix = 2 * CONV_CH + 3 * ATTN_W + MEM_W
    assert t % TM == 0 and nt <= GROUP and td == 1
    assert past_len % MOBA_BLOCK == 0 and MOBA_BLOCK == 2 * page_size and page_size == LANES
    n_blocks = past_len // MOBA_BLOCK

    row = lambda v: v.reshape(1, -1).astype(F32)
    w_in_b = w_in.astype(BF16)
    w_mix, w_gate = w_in_b[:, :n_mix], w_in_b[:, n_mix:]
    wpa, wpb, wpc = w_proj_a.astype(BF16), w_proj_b.astype(BF16), w_proj_c.astype(BF16)
    wout, wup, wdown = w_out.astype(BF16), w_up.astype(BF16), w_down.astype(BF16)
    g1, g2 = row(norm1_g), row(norm2_g)
    qg = row(jnp.tile(q_norm_g, N_HEADS))
    kg = row(jnp.tile(k_norm_g, N_HEADS))
    mqg = row(jnp.tile(mq_norm_g, MEM_HEADS))
    mkg = row(jnp.tile(mk_norm_g, MEM_HEADS))
    bdw_a, lng, lnb, bg, bdw_f = row(b_dw_a), row(ln_a_g), row(ln_a_b), row(b_gate), row(b_dw_f)
    grp = np.arange(ATTN_W)
    bd64 = jnp.asarray((grp[:, None] // HEAD_DIM == grp[None, :] // HEAD_DIM) / HEAD_DIM, BF16)
    bd128 = jnp.asarray((grp[:, None] // MEM_HEAD_DIM == grp[None, :] // MEM_HEAD_DIM) / MEM_HEAD_DIM, BF16)
    slopes = 2.0 ** (-8.0 * np.arange(1, N_HEADS + 1) / N_HEADS)
    aw = N_HEADS * LANES
    coef = {name: np.zeros((N_HEADS, LANES), np.float32) for name in ("qc", "qr", "qt", "kc", "kr", "kt")}
    rest = (slopes * LOG2E).astype(np.float32)
    for p in range(X_PARTS):
        part = _round_to_bf16(rest)
        rest = rest - part
        coef["qr"][:, X_RQ + p] = 1.0
        coef["kc"][:, X_RQ + p] = -part
        coef["qc"][:, X_RK + p] = part
        coef["kr"][:, X_RK + p] = 1.0
        coef["qt"][:, X_TQ + p] = 1.0
        coef["kc"][:, X_TQ + p] = -part * MOBA_BLOCK
        coef["qc"][:, X_TK + p] = part
        coef["kt"][:, X_TK + p] = MOBA_BLOCK
    lanes3 = lambda name: jnp.asarray(coef[name].reshape(1, 1, aw))
    tile_f = jnp.arange(nt, dtype=F32)[:, None, None]
    row_f = jnp.arange(TM, dtype=F32)[None, :, None]
    col = jnp.arange(aw, dtype=jnp.int32)[None, None, :] % LANES
    q_extra = (lanes3("qc") + row_f * lanes3("qr") + tile_f * lanes3("qt")).astype(BF16)
    k_extra = (lanes3("kc") + row_f * lanes3("kr") + tile_f * lanes3("kt")
               + jnp.where(col == X_SEL + tile_f.astype(jnp.int32), MASK_NEG, 0.0)).astype(BF16)

    xp = x_prompt.reshape(n * t, d)
    tile = lambda w: pl.BlockSpec((TM, w), lambda b, i, *_: (b * nt + i, 0))
    aw = N_HEADS * LANES
    vw = N_HEADS * V_ROWS
    blk3 = lambda r, c: pl.BlockSpec((1, r, c), lambda b, i: (b * nt + i, 0, 0))
    k_p, v_p, km, qt, ka, vt, mq, cact, utail = pl.pallas_call(
        _premix_kernel,
        grid=(n, nt),
        in_specs=[tile(d), _resident((1, d)), _resident((d, n_mix)), _resident((ATTN_W, ATTN_W)),
                  _resident((MEM_W, MEM_W)), _resident((1, ATTN_W)), _resident((1, ATTN_W)),
                  _resident((1, MEM_W)), _resident((CONV_WIDTH, CONV_CH)), _resident((1, CONV_CH)),
                  _resident((1, CONV_CH)), _resident((1, CONV_CH)),
                  pl.BlockSpec((1, TM, aw), lambda b, i: (i, 0, 0)),
                  pl.BlockSpec((1, TM, aw), lambda b, i: (i, 0, 0))],
        out_specs=[pl.BlockSpec((ATTN_W, TM), lambda b, i: (b, i)), pl.BlockSpec((ATTN_W, TM), lambda b, i: (b, i)),
                   pl.BlockSpec((1, 1, ATTN_W), lambda b, i: (b * nt + i, 0, 0)),
                   pl.BlockSpec((aw, TM), lambda b, i: (b, i)), blk3(TM, aw), blk3(vw, TM),
                   tile(MEM_W), tile(CONV_CH),
                   pl.BlockSpec((1, CONV_HALO, CONV_CH), lambda b, i: (b, 0, 0))],
        out_shape=[jax.ShapeDtypeStruct((n * ATTN_W, t), F32), jax.ShapeDtypeStruct((n * ATTN_W, t), F32),
                   jax.ShapeDtypeStruct((n * nt, 1, ATTN_W), F32),
                   jax.ShapeDtypeStruct((n * aw, t), BF16), jax.ShapeDtypeStruct((n * nt, TM, aw), BF16),
                   jax.ShapeDtypeStruct((n * nt, vw, TM), BF16), jax.ShapeDtypeStruct((n * t, MEM_W), BF16),
                   jax.ShapeDtypeStruct((n * t, CONV_CH), BF16),
                   jax.ShapeDtypeStruct((n, CONV_HALO, CONV_CH), F32)],
        scratch_shapes=[pltpu.VMEM((CONV_HALO + TM + SUBLANES, CONV_CH), F32)],
        compiler_params=_params(2),
        name="premix",
    )(xp, g1, w_mix, bd64, bd128, qg, kg, mqg, w_dw_a, bdw_a, lng, lnb, q_extra, k_extra)

    km4 = km.reshape(n, nt, N_HEADS, HEAD_DIM).transpose(0, 2, 1, 3)
    km4 = jnp.pad(km4, ((0, 0), (0, 0), (0, GROUP - nt), (0, LANES - HEAD_DIM)))
    kmt = (km4[:, :, :, None, :] * jnp.eye(N_HEADS, dtype=F32)[None, :, None, :, None])
    kmt = kmt.reshape(n, N_HEADS * GROUP, aw).astype(BF16)

    seq3 = lambda r, c: pl.BlockSpec((nt, r, c), lambda b, i: (b, 0, 0))
    ob = pl.pallas_call(
        _moba_kernel,
        grid=(n, nt),
        in_specs=[pl.BlockSpec((aw, TM), lambda b, i: (b, i)), seq3(TM, aw), seq3(vw, TM),
                  pl.BlockSpec((1, N_HEADS * GROUP, aw), lambda b, i: (b, 0, 0))],
        out_specs=tile(ATTN_W),
        out_shape=jax.ShapeDtypeStruct((n * t, ATTN_W), BF16),
        scratch_shapes=[pltpu.VMEM((N_HEADS, LANES, TM), BF16), pltpu.VMEM((N_HEADS, 1, TM), F32),
                        pltpu.VMEM((N_HEADS, V_ROWS, TM), F32)]
        + [pltpu.VMEM((TM, TM), F32)] * N_HEADS + [pltpu.VMEM((1, TM), F32)] * N_HEADS,
        compiler_params=_params(2),
        name="moba",
    )(qt, ka, vt, kmt)

    mem = mem_prompt.reshape(n * n_mem, d)
    mtile = lambda w: pl.BlockSpec((n_mem, w), lambda b: (b, 0))
    mk_p, mv_p, mkb, mvb = pl.pallas_call(
        _memkv_kernel,
        grid=(n,),
        in_specs=[mtile(d), _resident((1, d)), _resident((d, 2 * MEM_W)), _resident((MEM_W, MEM_W)),
                  _resident((1, MEM_W))],
        out_specs=[mtile(MEM_W)] * 4,
        out_shape=[jax.ShapeDtypeStruct((n * n_mem, MEM_W), F32)] * 2
        + [jax.ShapeDtypeStruct((n * n_mem, MEM_W), BF16)] * 2,
        compiler_params=_params(1),
        name="memkv",
    )(mem, row(mem_norm_g), w_mem_kv.astype(BF16), bd128, mkg)

    xs = x_sample.reshape(nd, d)
    st_conv = state_conv.transpose(1, 0, 2)
    st_ffn = state_ffn_conv.transpose(1, 0, 2)
    vm = pltpu.CompilerParams(vmem_limit_bytes=VMEM_LIMIT)
    u_s, q_s, k_s, v_s, mq_s, gl_s, cact_s = pl.pallas_call(
        _premix_s_kernel,
        out_shape=[jax.ShapeDtypeStruct((nd, CONV_CH), F32), jax.ShapeDtypeStruct((nd, ATTN_W), F32),
                   jax.ShapeDtypeStruct((nd, ATTN_W), F32), jax.ShapeDtypeStruct((nd, ATTN_W), F32),
                   jax.ShapeDtypeStruct((nd, MEM_W), F32), jax.ShapeDtypeStruct((nd, 3 * d), F32),
                   jax.ShapeDtypeStruct((nd, CONV_CH), BF16)],
        compiler_params=vm,
        name="premix_s",
    )(xs, g1, w_in_b, bd64, bd128, qg, kg, mqg, st_conv, w_dw_a, bdw_a, lng, lnb)

    ck = cache_k.transpose(0, 2, 3, 1)
    cv = cache_v.transpose(0, 2, 3, 1)
    pt_flat = page_table.reshape(-1).astype(jnp.int32)
    q_cols = jnp.broadcast_to(q_s.reshape(nd, N_HEADS, HEAD_DIM, 1), (nd, N_HEADS, HEAD_DIM, page_size))
    assert (n * nt) % nd == 0 and n_pages % ((n * nt) // nd) == 0 and n_blocks <= LANES
    steps_per_sample = (n * nt) // nd
    pages_per_step = n_pages // steps_per_sample
    assert pages_per_step % (MOBA_BLOCK // page_size) == 0

    memb = pl.BlockSpec((n_mem, MEM_W), lambda b, i, *_: (b, 0))
    x1, sel = pl.pallas_call(
        functools.partial(_postmix_kernel, pages_per_step=pages_per_step, n_blocks=n_blocks),
        grid_spec=pltpu.PrefetchScalarGridSpec(
            num_scalar_prefetch=1,
            grid=(n, nt),
            in_specs=[tile(d), tile(CONV_CH), tile(ATTN_W), tile(MEM_W), memb, memb, _resident((1, d)),
                      _resident((d, 3 * d)), _resident((1, 3 * d)), _resident((CONV_CH, d)),
                      _resident((ATTN_W, d)), _resident((MEM_W, d)), _resident((d, d)),
                      pl.BlockSpec((1, N_HEADS, HEAD_DIM, page_size),
                                   lambda b, i, pt: ((b * nt + i) // steps_per_sample, 0, 0, 0)),
                      pl.BlockSpec(memory_space=pl.ANY)],
            out_specs=[tile(d), pl.BlockSpec((1, N_HEADS, LANES),
                                             lambda b, i, pt: ((b * nt + i) // steps_per_sample, 0, 0))],
            scratch_shapes=[pltpu.VMEM((N_HEADS, LANES), F32),
                            pltpu.VMEM((2, pages_per_step, N_HEADS, HEAD_DIM, page_size), F32),
                            pltpu.SemaphoreType.DMA((2,))]),
        out_shape=[jax.ShapeDtypeStruct((n * t, d), F32), jax.ShapeDtypeStruct((nd, N_HEADS, LANES), jnp.int32)],
        compiler_params=_params(2),
        name="postmix",
    )(pt_flat, xp, cact, ob, mq, mkb, mvb, g1, w_gate, bg, wpa, wpb, wpc, wout, q_cols, ck)
    sel_flat = sel[:, :, :MOBA_TOPK].reshape(-1)

    chunk = dff // 2
    assert chunk % LANES == 0 and t % FFN_ROWS == 0
    nf = t // FFN_ROWS
    ftile = pl.BlockSpec((FFN_ROWS, d), lambda b, i: (b * nf + i, 0))
    y_p, ftail = pl.pallas_call(
        functools.partial(_ffn_kernel, chunk=chunk),
        grid=(n, nf),
        in_specs=[ftile, _resident((1, d)), _resident((d, 2 * dff)), _resident((FFN_CONV_WIDTH, 2 * dff)),
                  _resident((1, 2 * dff)), _resident((dff, d))],
        out_specs=[ftile, pl.BlockSpec((1, SUBLANES, 2 * dff), lambda b, i: (b, 0, 0))],
        out_shape=[jax.ShapeDtypeStruct((n * t, d), F32), jax.ShapeDtypeStruct((n, SUBLANES, 2 * dff), F32)],
        scratch_shapes=[pltpu.VMEM((SUBLANES + FFN_ROWS, 2 * dff), F32)],
        compiler_params=_params(2),
        name="ffn",
    )(x1, g2, wup, w_dw_f, bdw_f, wdown)

    row_spec = pl.BlockSpec((1, 1, ATTN_W), lambda b, *_: (b, 0, 0))
    rows3 = lambda a: a.reshape(nd, 1, -1)

    halves = jnp.arange(PAGES_PER_BLOCK, dtype=jnp.int32)
    sel_pages = PAGES_PER_BLOCK * sel[:, :, :MOBA_TOPK, None] + halves
    page_ids = jnp.take_along_axis(page_table.astype(jnp.int32), sel_pages.reshape(nd, -1), axis=1).reshape(-1)
    tiles_per_sample = N_HEADS * MOBA_TOPK * PAGES_PER_BLOCK
    tile_buf = pltpu.VMEM((2, tiles_per_sample, HEAD_DIM, page_size), F32)
    ob_s = pl.pallas_call(
        functools.partial(_decode_attn_kernel, past_len=past_len),
        grid_spec=pltpu.PrefetchScalarGridSpec(
            num_scalar_prefetch=2,
            grid=(nd,),
            in_specs=[pl.BlockSpec(memory_space=pltpu.SMEM), row_spec, row_spec, row_spec,
                      pl.BlockSpec(memory_space=pl.ANY), pl.BlockSpec(memory_space=pl.ANY)],
            out_specs=row_spec,
            scratch_shapes=[tile_buf, tile_buf, pltpu.SemaphoreType.DMA((2, 2))]),
        out_shape=jax.ShapeDtypeStruct((nd, 1, ATTN_W), F32),
        compiler_params=_params(1),
        name="decode_attn",
    )(page_ids, sel_flat, jnp.asarray(slopes, F32), rows3(q_s), rows3(k_s), rows3(v_s), ck, cv)
    ob_s = ob_s.reshape(nd, ATTN_W)

    group = SUBLANES
    assert nd % group == 0
    cm_spec = pl.BlockSpec((group, n_mem * MEM_HEADS, MEM_HEAD_DIM), lambda b: (b, 0, 0))
    mq_spec = pl.BlockSpec((group, 1, MEM_W), lambda b: (b, 0, 0))
    oc_s = pl.pallas_call(
        functools.partial(_memattn_s_kernel, group=group, n_mem=n_mem),
        grid=(nd // group,),
        in_specs=[mq_spec, cm_spec, cm_spec],
        out_specs=mq_spec,
        out_shape=jax.ShapeDtypeStruct((nd, 1, MEM_W), F32),
        compiler_params=_params(1),
        name="memattn_s",
    )(rows3(mq_s), cache_mem_k.reshape(nd, n_mem * MEM_HEADS, MEM_HEAD_DIM),
      cache_mem_v.reshape(nd, n_mem * MEM_HEADS, MEM_HEAD_DIM))
    oc_s = oc_s.reshape(nd, MEM_W)

    y_s, up_s = pl.pallas_call(
        _post_s_kernel,
        out_shape=[jax.ShapeDtypeStruct((nd, d), F32), jax.ShapeDtypeStruct((nd, 2 * dff), F32)],
        compiler_params=vm,
        name="post_s",
    )(xs, cact_s, ob_s, oc_s, gl_s, bg, wpa, wpb, wpc, wout, g2, wup, st_ffn, w_dw_f, bdw_f, wdown)

    heads = lambda a, b, s: a.reshape(b, s, N_HEADS, HEAD_DIM)
    from_t = lambda a: a.reshape(n, N_HEADS, HEAD_DIM, t).transpose(0, 3, 1, 2)
    conv_p = utail[:, CONV_HALO - (CONV_WIDTH - 1):, :]
    conv_s = jnp.concatenate([state_conv[:, 1:, :], u_s[:, None, :]], axis=1)
    ffn_p = ftail[:, SUBLANES - (FFN_CONV_WIDTH - 1):, :]
    ffn_s = jnp.concatenate([state_ffn_conv[:, 1:, :], up_s[:, None, :]], axis=1)
    return (y_p.reshape(n, t, d), y_s.reshape(nd, td, d),
            from_t(k_p), from_t(v_p), heads(k_s, nd, td), heads(v_s, nd, td),
            conv_p, conv_s, ffn_p, ffn_s,
            mk_p.reshape(n, n_mem, MEM_HEADS, MEM_HEAD_DIM), mv_p.reshape(n, n_mem, MEM_HEADS, MEM_HEAD_DIM))
```

```python
import functools

import numpy as np
import jax
import jax.numpy as jnp
from jax import lax
from jax.experimental import pallas as pl
from jax.experimental.pallas import tpu as pltpu

F32 = jnp.float32
BF16 = jnp.bfloat16

EPS = 1e-6
CONV_CH = 512
CONV_WIDTH = 31
N_HEADS = 8
HEAD_DIM = 64
ATTN_W = N_HEADS * HEAD_DIM
MOBA_BLOCK = 256
MOBA_TOPK = 3
MEM_HEADS = 4
MEM_HEAD_DIM = 128
MEM_W = MEM_HEADS * MEM_HEAD_DIM
FFN_CONV_WIDTH = 3
LANES = 128
SUBLANES = 8
TM = MOBA_BLOCK
CONV_HALO = 32
MASK_NEG = -float(2 ** 30)
GROUP = 16
V_ROWS = HEAD_DIM + 16
BLOCKS_PER_TRIP = 4
FFN_ROWS = 512
VMEM_LIMIT = 56 * 1024 * 1024

X_SEL = HEAD_DIM
X_PARTS = 3
X_RQ = HEAD_DIM + GROUP
X_RK = X_RQ + X_PARTS
X_TQ = X_RK + X_PARTS
X_TK = X_TQ + X_PARTS
LOG2E = 1.4426950408889634


def _dot(a, b):
    return jnp.dot(a, b, preferred_element_type=F32)


def _dot_nt(a, b):
    return lax.dot_general(a, b, (((1,), (1,)), ((), ())), preferred_element_type=F32)


def _rms_rows(x, g):
    return x * lax.rsqrt(jnp.mean(x * x, axis=-1, keepdims=True) + EPS) * g


def _group_rms(z, bd, g):
    sq = z * z
    hi = sq.astype(BF16)
    lo = (sq - hi.astype(F32)).astype(BF16)
    ms = _dot(hi, bd) + _dot(lo, bd)
    return z * lax.rsqrt(ms + EPS) * g


def _exact_zero(v):
    bits = pltpu.bitcast(v, jnp.uint32)
    half = jnp.uint32(16)
    return pltpu.bitcast(lax.shift_right_logical(lax.shift_right_logical(bits, half), half), F32)


def _sigmoid(x):
    return 1.0 / (1.0 + jnp.exp(-x))


def _layernorm_silu(c, g, b):
    mu = jnp.mean(c, axis=-1, keepdims=True)
    xc = c - mu
    var = jnp.mean(xc * xc, axis=-1, keepdims=True)
    y = xc * lax.rsqrt(var + EPS) * g + b
    return y * _sigmoid(y)


def _premix_kernel(x_ref, g1_ref, w_ref, bd64_ref, bd128_ref, qg_ref, kg_ref, mqg_ref,
                   wdw_ref, bdw_ref, lng_ref, lnb_ref, qx_ref, kx_ref,
                   k_ref, v_ref, km_ref, qt_ref, ka_ref, vt_ref, mq_ref, cact_ref, utail_ref,
                   ubuf):
    t = pl.program_id(1)
    nt = pl.num_programs(1)
    c = CONV_CH

    @pl.when(t == 0)
    def _():
        ubuf[0:CONV_HALO, :] = jnp.zeros((CONV_HALO, c), F32)
        ubuf[CONV_HALO + TM:, :] = jnp.zeros((SUBLANES, c), F32)

    hn = _rms_rows(x_ref[...], g1_ref[...]).astype(BF16)

    a = _dot(hn, w_ref[:, 0:c])
    g = _dot(hn, w_ref[:, c:2 * c])
    u = a * _sigmoid(g)
    ubuf[CONV_HALO:CONV_HALO + TM, :] = u
    o = 2 * c
    zq = _dot(hn, w_ref[:, o:o + ATTN_W])
    zk = _dot(hn, w_ref[:, o + ATTN_W:o + 2 * ATTN_W])
    zv = _dot(hn, w_ref[:, o + 2 * ATTN_W:o + 3 * ATTN_W])
    zm = _dot(hn, w_ref[:, o + 3 * ATTN_W:o + 3 * ATTN_W + MEM_W])
    acc = jnp.broadcast_to(bdw_ref[...], (TM, c))
    base = CONV_HALO - (CONV_WIDTH - 1)
    span = TM + 2 * SUBLANES
    for b in range(SUBLANES):
        part = None
        for k in range(b, CONV_WIDTH, SUBLANES):
            term = wdw_ref[k:k + 1, :] * ubuf[k - b:k - b + span, :]
            part = term if part is None else part + term
        acc = acc + part[base + b:base + b + TM, :]
    ubuf[0:CONV_HALO, :] = ubuf[TM:TM + CONV_HALO, :]
    cact_ref[...] = _layernorm_silu(acc, lng_ref[...], lnb_ref[...]).astype(BF16)

    qn = _group_rms(zq, bd64_ref[...], qg_ref[...]) * (HEAD_DIM ** -0.5)
    kn = _group_rms(zk, bd64_ref[...], kg_ref[...])
    mq = _group_rms(zm, bd128_ref[...], mqg_ref[...])

    ones_rows = jnp.where(lax.broadcasted_iota(jnp.int32, (V_ROWS - HEAD_DIM, TM), 0) == 0, 1.0, 0.0).astype(BF16)
    for cb in range(ATTN_W // LANES):
        k_ref[LANES * cb:LANES * (cb + 1), :] = kn[:, LANES * cb:LANES * (cb + 1)].T
        vt = zv[:, LANES * cb:LANES * (cb + 1)].T
        v_ref[LANES * cb:LANES * (cb + 1), :] = vt
        for sub in range(2):
            r0 = V_ROWS * (2 * cb + sub)
            vt_ref[0, r0:r0 + HEAD_DIM, :] = vt[HEAD_DIM * sub:HEAD_DIM * (sub + 1), :].astype(BF16)
            vt_ref[0, r0 + HEAD_DIM:r0 + V_ROWS, :] = ones_rows
    mq_ref[...] = mq.astype(BF16)
    km_ref[0] = jnp.mean(kn, axis=0, keepdims=True)

    lane = lax.broadcasted_iota(jnp.int32, (TM, LANES), 1)
    for hp in range(N_HEADS // 2):
        xq = qn[:, LANES * hp:LANES * (hp + 1)]
        xk = kn[:, LANES * hp:LANES * (hp + 1)] * LOG2E
        for sub in range(2):
            h = 2 * hp + sub
            if sub == 1:
                xq = pltpu.roll(xq, HEAD_DIM, 1)
                xk = pltpu.roll(xk, HEAD_DIM, 1)
            eq = qx_ref[0, :, LANES * h:LANES * (h + 1)].astype(F32)
            ek = kx_ref[0, :, LANES * h:LANES * (h + 1)].astype(F32)
            qt_ref[LANES * h:LANES * (h + 1), :] = jnp.where(lane < HEAD_DIM, xq, eq).T.astype(BF16)
            ka_ref[0, :, LANES * h:LANES * (h + 1)] = jnp.where(lane < HEAD_DIM, xk, ek).astype(BF16)

    @pl.when(t == nt - 1)
    def _():
        utail_ref[0] = ubuf[0:CONV_HALO, :]


def _moba_kernel(qt_ref, ka_ref, vt_ref, kmt_ref, o_ref, qh_scr, m_scr, acc_scr, *stage):
    i = pl.program_id(1)
    s_scr, mx_scr = stage[:N_HEADS], stage[N_HEADS:]
    key = lax.broadcasted_iota(jnp.int32, (TM, TM), 0)
    qry = lax.broadcasted_iota(jnp.int32, (TM, TM), 1)
    causal = key <= qry

    def scores(j, h, diagonal):
        q = qt_ref[LANES * h:LANES * (h + 1), :] if diagonal else qh_scr[h]
        s = _dot(ka_ref[j, :, LANES * h:LANES * (h + 1)], q)
        if diagonal:
            s = jnp.where(causal, s, -jnp.inf)
        s_scr[h][...] = s
        mx_scr[h][...] = jnp.max(s, axis=0, keepdims=True)

    for h in range(N_HEADS):
        scores(i, h, True)

    blk = lax.broadcasted_iota(jnp.int32, (N_HEADS, GROUP, TM), 1)
    past = blk < i
    gate = _dot(kmt_ref[0], qt_ref[...]).reshape(N_HEADS, GROUP, TM)
    g = jnp.where(past, gate, -jnp.inf)
    unselected = jnp.where(past, 1.0, 0.0)
    for _ in range(MOBA_TOPK):
        best = jnp.max(g, axis=1, keepdims=True)
        first = jnp.min(jnp.where(g == best, blk, GROUP), axis=1, keepdims=True)
        taken = blk == first
        g = jnp.where(taken, -jnp.inf, g)
        unselected = jnp.where(taken, 0.0, unselected)
    nsel = unselected.astype(BF16).reshape(N_HEADS * GROUP, TM)
    for h in range(N_HEADS):
        r0 = LANES * h
        qh_scr[h, 0:X_SEL, :] = qt_ref[r0:r0 + X_SEL, :]
        qh_scr[h, X_SEL:X_RQ, :] = nsel[GROUP * h:GROUP * (h + 1), :]
        qh_scr[h, X_RQ:LANES, :] = qt_ref[r0 + X_RQ:r0 + LANES, :]

    def accumulate(j, h):
        m_old = m_scr[h]
        m_new = jnp.maximum(m_old, mx_scr[h][...])
        alpha = jnp.exp2(m_old - m_new)
        p = jnp.exp2(s_scr[h][...] - m_new).astype(BF16)
        acc_scr[h] = alpha * acc_scr[h] + _dot(vt_ref[j, V_ROWS * h:V_ROWS * (h + 1), :], p)
        m_scr[h] = m_new

    m_scr[...] = jnp.full(m_scr.shape, -jnp.inf, F32)
    acc_scr[...] = jnp.zeros(acc_scr.shape, F32)

    def step(prev, j):
        for h in range(N_HEADS):
            accumulate(prev, h)
            scores(j, h, False)

    def steps(first, count):
        step(jnp.where(first == 0, i, first - 1), first)
        for extra in range(1, count):
            step(first + extra - 1, first + extra)

    def body(trip, carry):
        steps(trip * BLOCKS_PER_TRIP, BLOCKS_PER_TRIP)
        return carry

    lax.fori_loop(0, i // BLOCKS_PER_TRIP, body, 0)
    size = BLOCKS_PER_TRIP // 2
    while size:
        done = (i // (2 * size)) * (2 * size)

        @pl.when((i // size) % 2 == 1)
        def _(done=done, size=size):
            steps(done, size)

        size //= 2
    last = jnp.where(i == 0, i, i - 1)
    for h in range(N_HEADS):
        accumulate(last, h)
    outs = []
    for h in range(N_HEADS):
        acc = acc_scr[h]
        outs.append(acc[0:HEAD_DIM, :] / acc[HEAD_DIM:HEAD_DIM + 1, :])
    o_ref[...] = jnp.concatenate(outs, axis=0).T.astype(BF16)


def _memkv_kernel(mem_ref, g_ref, w_ref, bd128_ref, mkg_ref, mk_ref, mv_ref, mkb_ref, mvb_ref):
    hn = _rms_rows(mem_ref[...], g_ref[...]).astype(BF16)
    mk = _group_rms(_dot(hn, w_ref[:, 0:MEM_W]), bd128_ref[...], mkg_ref[...])
    mv = _dot(hn, w_ref[:, MEM_W:2 * MEM_W])
    mk_ref[...] = mk
    mv_ref[...] = mv
    mkb_ref[...] = mk.astype(BF16)
    mvb_ref[...] = mv.astype(BF16)


def _mem_attend_rows(mq, mk, mv):
    heads = [slice(MEM_HEAD_DIM * hh, MEM_HEAD_DIM * (hh + 1)) for hh in range(MEM_HEADS)]
    scores = [_dot_nt(mq[:, sl], mk[:, sl]) * (MEM_HEAD_DIM ** -0.5) for sl in heads]
    probs = []
    for s in scores:
        p = jnp.exp(s - jnp.max(s, axis=-1, keepdims=True))
        probs.append((p.astype(BF16), jnp.sum(p, axis=-1, keepdims=True)))
    return jnp.concatenate([_dot(p, mv[:, sl]) / l for (p, l), sl in zip(probs, heads)], axis=-1)


def _merge_out(x, gl, ya, yb, yc, wout):
    d = x.shape[-1]
    merged = (_sigmoid(gl[:, 0:d]) * ya + _sigmoid(gl[:, d:2 * d]) * yb + _sigmoid(gl[:, 2 * d:3 * d]) * yc)
    return x + _dot(merged.astype(BF16), wout)


def _postmix_kernel(pt_ref, x_ref, cact_ref, ob_ref, mq_ref, mk_ref, mv_ref, g1_ref, wg_ref, bg_ref,
                    wpa_ref, wpb_ref, wpc_ref, wout_ref, q_ref, ck_hbm, x1_ref, sel_ref,
                    gate_scr, page_buf, page_sem, *, pages_per_step, n_blocks):
    step = pl.program_id(0) * pl.num_programs(1) + pl.program_id(1)
    n_steps = pl.num_programs(0) * pl.num_programs(1)
    slot = step % 2

    def page_copy(page, to_slot, r):
        return pltpu.make_async_copy(ck_hbm.at[page], page_buf.at[to_slot, r], page_sem.at[to_slot])

    def fetch(for_step, to_slot):
        for r in range(pages_per_step):
            page_copy(pt_ref[for_step * pages_per_step + r], to_slot, r).start()

    @pl.when(step == 0)
    def _():
        gate_scr[...] = jnp.zeros(gate_scr.shape, F32)
        fetch(0, 0)

    for r in range(pages_per_step):
        page_copy(0, slot, r).wait()
    page_refs = [page_buf.at[slot, r] for r in range(pages_per_step)]

    blocks_per_step = pages_per_step // PAGES_PER_BLOCK
    steps_per_sample = n_blocks // blocks_per_step
    share = step % steps_per_sample

    g = jnp.where(share == 0, 0.0, gate_scr[...])
    x = x_ref[...]
    d = x.shape[-1]
    hn = _rms_rows(x, g1_ref[...]).astype(BF16)
    oc = _mem_attend_rows(mq_ref[...], mk_ref[...], mv_ref[...]).astype(BF16)
    fetch(jnp.minimum(step + 1, n_steps - 1), 1 - slot)
    g = _gate_scores(g, share * blocks_per_step, q_ref, page_refs)
    tie = jnp.concatenate([_exact_zero(g)[0:1, :]] * (d // LANES), axis=1)
    sources = ((cact_ref[...], wpa_ref), (ob_ref[...], wpb_ref), (oc, wpc_ref))
    ys = [_dot(src, w_ref[...]) for src, w_ref in sources]
    merged = None
    for br in range(len(sources)):
        cols = slice(d * br, d * (br + 1))
        gl = _dot(hn, wg_ref[:, cols]) + (bg_ref[:, cols] + tie)
        term = _sigmoid(gl) * ys[br]
        merged = term if merged is None else merged + term
    x1_ref[...] = x + _dot(merged.astype(BF16), wout_ref[...])
    gate_scr[...] = g
    _gate_select(share == steps_per_sample - 1, gate_scr, sel_ref, n_blocks)

    @pl.when(step == n_steps - 1)
    def _():
        for r in range(pages_per_step):
            page_copy(0, 1 - slot, r).wait()


def _ffn_kernel(x1_ref, g2_ref, wup_ref, wdw_ref, bdw_ref, wdown_ref, y_ref, tail_ref, upbuf, *, chunk):
    rows = x1_ref.shape[0]
    t = pl.program_id(1)
    nt = pl.num_programs(1)
    dff = wdown_ref.shape[0]

    @pl.when(t == 0)
    def _():
        upbuf[0:SUBLANES, :] = jnp.zeros((SUBLANES, 2 * dff), F32)

    x1 = x1_ref[...]
    hn = _rms_rows(x1, g2_ref[...]).astype(BF16)

    for c in range(0, 2 * dff, chunk):
        upbuf[SUBLANES:SUBLANES + rows, c:c + chunk] = _dot(hn, wup_ref[:, c:c + chunk])

    def conv(c):
        out = bdw_ref[:, c:c + chunk]
        for k in range(FFN_CONV_WIDTH):
            r0 = SUBLANES - (FFN_CONV_WIDTH - 1) + k
            out = out + wdw_ref[k:k + 1, c:c + chunk] * upbuf[r0:r0 + rows, c:c + chunk]
        return out

    y = x1
    for c in range(0, dff, chunk):
        a = conv(c)
        b = conv(dff + c)
        act = (a * _sigmoid(a) * b).astype(BF16)
        y = y + _dot(act, wdown_ref[c:c + chunk, :])
    y_ref[...] = y

    @pl.when(t == nt - 1)
    def _():
        tail_ref[0] = upbuf[rows:rows + SUBLANES, :]

    upbuf[0:SUBLANES, :] = upbuf[rows:rows + SUBLANES, :]


def _premix_s_kernel(x_ref, g1_ref, w_ref, bd64_ref, bd128_ref, qg_ref, kg_ref, mqg_ref,
                     st_ref, wdw_ref, bdw_ref, lng_ref, lnb_ref,
                     u_ref, q_ref, k_ref, v_ref, mq_ref, gl_ref, cact_ref):
    c = CONV_CH
    hn = _rms_rows(x_ref[...], g1_ref[...]).astype(BF16)
    a = _dot(hn, w_ref[:, 0:c])
    g = _dot(hn, w_ref[:, c:2 * c])
    u = a * _sigmoid(g)
    u_ref[...] = u
    acc = bdw_ref[...] + wdw_ref[CONV_WIDTH - 1:CONV_WIDTH, :] * u
    for k in range(CONV_WIDTH - 1):
        acc = acc + wdw_ref[k:k + 1, :] * st_ref[k]
    cact_ref[...] = _layernorm_silu(acc, lng_ref[...], lnb_ref[...]).astype(BF16)

    o = 2 * c
    q_ref[...] = _group_rms(_dot(hn, w_ref[:, o:o + ATTN_W]), bd64_ref[...], qg_ref[...])
    o += ATTN_W
    k_ref[...] = _group_rms(_dot(hn, w_ref[:, o:o + ATTN_W]), bd64_ref[...], kg_ref[...])
    o += ATTN_W
    v_ref[...] = _dot(hn, w_ref[:, o:o + ATTN_W])
    o += ATTN_W
    mq_ref[...] = _group_rms(_dot(hn, w_ref[:, o:o + MEM_W]), bd128_ref[...], mqg_ref[...])
    o += MEM_W
    gl_ref[...] = _dot(hn, w_ref[:, o:])


PAGES_PER_BLOCK = MOBA_BLOCK // LANES


def _gate_scores(g, first_block, q_ref, page_refs):
    head = lax.broadcasted_iota(jnp.int32, (N_HEADS, LANES), 0)
    lane = lax.broadcasted_iota(jnp.int32, (HEAD_DIM, LANES), 1)
    for h in range(N_HEADS):
        kmean = jnp.zeros((HEAD_DIM, LANES), F32)
        for r in range(0, len(page_refs), PAGES_PER_BLOCK):
            ksum = page_refs[r][h]
            for extra in range(1, PAGES_PER_BLOCK):
                ksum = ksum + page_refs[r + extra][h]
            col = jnp.sum(ksum, axis=1, keepdims=True) * (1.0 / MOBA_BLOCK)
            kmean = jnp.where(lane == first_block + r // PAGES_PER_BLOCK, col, kmean)
        prod = q_ref[0, h].astype(BF16).astype(F32) * kmean.astype(BF16).astype(F32)
        g = g + jnp.where(head == h, jnp.sum(prod, axis=0, keepdims=True), 0.0)
    return g


def _gate_select(last, gate_scr, sel_ref, n_blocks):
    @pl.when(last)
    def _():
        g = gate_scr[:, 0:n_blocks]
        bl = lax.broadcasted_iota(jnp.int32, (N_HEADS, n_blocks), 1)
        rank = jnp.zeros((N_HEADS, n_blocks), F32)
        for b in range(n_blocks):
            other = g[:, b:b + 1]
            rank = rank + jnp.where(bl > b, jnp.where(other >= g, 1.0, 0.0), jnp.where(other > g, 1.0, 0.0))
        lane_o = lax.broadcasted_iota(jnp.int32, (N_HEADS, LANES), 1)
        out = jnp.zeros((N_HEADS, LANES), F32)
        blf = bl.astype(F32)
        for r in range(MOBA_TOPK):
            idx = jnp.sum(jnp.where(rank == float(r), blf, 0.0), axis=-1, keepdims=True)
            out = jnp.where(lane_o == r, idx, out)
        sel_ref[0] = out.astype(jnp.int32)


def _decode_attn_kernel(page_ref, sel_ref, slope_ref, q_ref, kown_ref, vown_ref, ck_hbm, cv_hbm, o_ref,
                        k_buf, v_buf, sem, *, past_len):
    pages_per_block = MOBA_BLOCK // LANES
    tiles_per_head = MOBA_TOPK * pages_per_block
    n_tiles = N_HEADS * tiles_per_head
    n = pl.program_id(0)
    slot = n % 2

    def tile_copies(page, to_slot, tile):
        h = tile // tiles_per_head
        return (pltpu.make_async_copy(ck_hbm.at[page, h], k_buf.at[to_slot, tile], sem.at[0, to_slot]),
                pltpu.make_async_copy(cv_hbm.at[page, h], v_buf.at[to_slot, tile], sem.at[1, to_slot]))

    def fetch(sample, to_slot):
        for tile in range(n_tiles):
            for copy in tile_copies(page_ref[sample * n_tiles + tile], to_slot, tile):
                copy.start()

    @pl.when(n == 0)
    def _():
        fetch(0, 0)

    @pl.when(n + 1 < pl.num_programs(0))
    def _():
        fetch(n + 1, 1 - slot)

    for tile in range(n_tiles):
        for copy in tile_copies(0, slot, tile):
            copy.wait()
    k_refs = [k_buf.at[slot, tile] for tile in range(n_tiles)]
    v_refs = [v_buf.at[slot, tile] for tile in range(n_tiles)]

    lane = lax.broadcasted_iota(jnp.int32, (1, LANES), 1)
    lanef = lane.astype(F32)
    zeros = jnp.zeros((HEAD_DIM, tiles_per_head * LANES), BF16)
    scale = HEAD_DIM ** -0.5

    def head_tiles(tile_refs, h):
        t = jnp.concatenate([tile_refs[h * tiles_per_head + idx][...] for idx in range(tiles_per_head)], axis=1)
        t = t.astype(BF16)
        return jnp.concatenate([t, zeros] if h % 2 == 0 else [zeros, t], axis=0)

    def own_half(h):
        return (lane < HEAD_DIM) if h % 2 == 0 else (lane >= HEAD_DIM)

    def pair_lanes(ref, h):
        return ref[0][:, LANES * (h // 2):LANES * (h // 2 + 1)]

    qz, raw = [], []
    for h in range(N_HEADS):
        qz.append(jnp.where(own_half(h), pair_lanes(q_ref, h) * scale, 0.0).astype(BF16))
        raw.append(_dot(jnp.broadcast_to(qz[h], (SUBLANES, LANES)), head_tiles(k_refs, h))[0:1, :])
    probs = []
    for h in range(N_HEADS):
        dist = []
        for r in range(MOBA_TOPK):
            blk = sel_ref[(n * N_HEADS + h) * MOBA_TOPK + r]
            for half in range(pages_per_block):
                pos0 = (blk * MOBA_BLOCK + half * LANES).astype(F32)
                dist.append(float(past_len) - (pos0 + lanef))
        logits = raw[h] - slope_ref[h] * jnp.concatenate(dist, axis=1)
        kown = pair_lanes(kown_ref, h).astype(BF16).astype(F32)
        s_own = jnp.sum(qz[h].astype(F32) * kown, axis=-1, keepdims=True)
        m = jnp.maximum(s_own, jnp.max(logits, axis=-1, keepdims=True))
        p_own = jnp.exp(s_own - m)
        p = jnp.exp(logits - m)
        probs.append((p, p_own, p_own + jnp.sum(p, axis=-1, keepdims=True)))
    outs = []
    for h in range(N_HEADS):
        p, p_own, l = probs[h]
        p8 = jnp.broadcast_to(p.astype(BF16), (SUBLANES, p.shape[1]))
        vown = jnp.where(own_half(h), pair_lanes(vown_ref, h).astype(BF16).astype(F32), 0.0)
        acc = p_own.astype(BF16).astype(F32) * vown + _dot_nt(p8, head_tiles(v_refs, h))[0:1, :]
        outs.append(acc / l)
    o_ref[0] = jnp.concatenate([outs[h] + outs[h + 1] for h in range(0, N_HEADS, 2)], axis=1)


def _memattn_s_kernel(mq_ref, mk_ref, mv_ref, o_ref, *, group, n_mem):
    units = [(s, hh) for s in range(group) for hh in range(MEM_HEADS)]
    scores = []
    for s, hh in units:
        mq = jnp.broadcast_to(mq_ref[s][:, MEM_HEAD_DIM * hh:MEM_HEAD_DIM * (hh + 1)], (SUBLANES, MEM_HEAD_DIM))
        mk = mk_ref[s, pl.ds(hh, n_mem, stride=MEM_HEADS), :].astype(BF16)
        scores.append(_dot_nt(mq.astype(BF16), mk)[0:1, :] * (MEM_HEAD_DIM ** -0.5))
    probs = []
    for sc in scores:
        p = jnp.exp(sc - jnp.max(sc, axis=-1, keepdims=True))
        probs.append((p, jnp.sum(p, axis=-1, keepdims=True)))
    outs = {}
    for (s, hh), (p, l) in zip(units, probs):
        mv = mv_ref[s, pl.ds(hh, n_mem, stride=MEM_HEADS), :].astype(BF16)
        p8 = jnp.broadcast_to(p.astype(BF16), (SUBLANES, n_mem))
        outs[s, hh] = _dot(p8, mv)[0:1, :] / l
    for s in range(group):
        o_ref[s] = jnp.concatenate([outs[s, hh] for hh in range(MEM_HEADS)], axis=-1)


def _post_s_kernel(x_ref, cact_ref, ob_ref, oc_ref, gl_ref, bg_ref, wpa_ref, wpb_ref, wpc_ref, wout_ref,
                   g2_ref, wup_ref, st_ref, wdw_ref, bdw_ref, wdown_ref, y_ref, up_ref):
    x = x_ref[...]
    dff = wdown_ref.shape[0]
    ya = _dot(cact_ref[...], wpa_ref[...])
    yb = _dot(ob_ref[...].astype(BF16), wpb_ref[...])
    yc = _dot(oc_ref[...].astype(BF16), wpc_ref[...])
    x1 = _merge_out(x, gl_ref[...] + bg_ref[...], ya, yb, yc, wout_ref[...])
    up = _dot(_rms_rows(x1, g2_ref[...]).astype(BF16), wup_ref[...])
    up_ref[...] = up
    cv = bdw_ref[...] + wdw_ref[FFN_CONV_WIDTH - 1:FFN_CONV_WIDTH, :] * up
    for k in range(FFN_CONV_WIDTH - 1):
        cv = cv + wdw_ref[k:k + 1, :] * st_ref[k]
    a = cv[:, 0:dff]
    b = cv[:, dff:]
    act = (a * _sigmoid(a) * b).astype(BF16)
    y_ref[...] = x1 + _dot(act, wdown_ref[...])


def _round_to_bf16(x):
    bits = np.asarray(x, np.float32).view(np.uint32)
    bits = (bits + np.uint32(0x7FFF) + ((bits >> np.uint32(16)) & np.uint32(1))) & np.uint32(0xFFFF0000)
    return bits.view(np.float32)


def _resident(shape):
    nd = len(shape)
    return pl.BlockSpec(shape, lambda *_: (0,) * nd, pipeline_mode=pl.Buffered(1))


def _params(n_axes):
    return pltpu.CompilerParams(dimension_semantics=("arbitrary",) * n_axes, vmem_limit_bytes=VMEM_LIMIT)


def kernel(x_prompt, x_sample, mem_prompt, cache_k, cache_v, page_table, state_conv, state_ffn_conv, cache_mem_k, cache_mem_v, norm1_g, w_in, b_gate, w_dw_a, b_dw_a, ln_a_g, ln_a_b, w_proj_a, q_norm_g, k_norm_g, w_proj_b, mem_norm_g, w_mem_kv, mq_norm_g, mk_norm_g, w_proj_c, w_out, norm2_g, w_up, w_dw_f, b_dw_f, w_down):
    n, t, d = x_prompt.shape
    nd, td, _ = x_sample.shape
    n_mem = mem_prompt.shape[1]
    n_pool, page_size = cache_k.shape[:2]
    n_pages = page_table.shape[1]
    past_len = n_pages * page_size
    dff = w_down.shape[0]
    nt = t // TM
    n_mix = 2 * CONV_CH + 3 * ATTN_W + MEM_W
    assert t % TM == 0 and nt <= GROUP and td == 1
    assert past_len % MOBA_BLOCK == 0 and MOBA_BLOCK == 2 * page_size and page_size == LANES
    n_blocks = past_len // MOBA_BLOCK

    row = lambda v: v.reshape(1, -1).astype(F32)
    w_in_b = w_in.astype(BF16)
    w_mix, w_gate = w_in_b[:, :n_mix], w_in_b[:, n_mix:]
    wpa, wpb, wpc = w_proj_a.astype(BF16), w_proj_b.astype(BF16), w_proj_c.astype(BF16)
    wout, wup, wdown = w_out.astype(BF16), w_up.astype(BF16), w_down.astype(BF16)
    g1, g2 = row(norm1_g), row(norm2_g)
    qg = row(jnp.tile(q_norm_g, N_HEADS))
    kg = row(jnp.tile(k_norm_g, N_HEADS))
    mqg = row(jnp.tile(mq_norm_g, MEM_HEADS))
    mkg = row(jnp.tile(mk_norm_g, MEM_HEADS))
    bdw_a, lng, lnb, bg, bdw_f = row(b_dw_a), row(ln_a_g), row(ln_a_b), row(b_gate), row(b_dw_f)
    grp = np.arange(ATTN_W)
    bd64 = jnp.asarray((grp[:, None] // HEAD_DIM == grp[None, :] // HEAD_DIM) / HEAD_DIM, BF16)
    bd128 = jnp.asarray((grp[:, None] // MEM_HEAD_DIM == grp[None, :] // MEM_HEAD_DIM) / MEM_HEAD_DIM, BF16)
    slopes = 2.0 ** (-8.0 * np.arange(1, N_HEADS + 1) / N_HEADS)
    aw = N_HEADS * LANES
    coef = {name: np.zeros((N_HEADS, LANES), np.float32) for name in ("qc", "qr", "qt", "kc", "kr", "kt")}
    rest = (slopes * LOG2E).astype(np.float32)
    for p in range(X_PARTS):
        part = _round_to_bf16(rest)
        rest = rest - part
        coef["qr"][:, X_RQ + p] = 1.0
        coef["kc"][:, X_RQ + p] = -part
        coef["qc"][:, X_RK + p] = part
        coef["kr"][:, X_RK + p] = 1.0
        coef["qt"][:, X_TQ + p] = 1.0
        coef["kc"][:, X_TQ + p] = -part * MOBA_BLOCK
        coef["qc"][:, X_TK + p] = part
        coef["kt"][:, X_TK + p] = MOBA_BLOCK
    lanes3 = lambda name: jnp.asarray(coef[name].reshape(1, 1, aw))
    tile_f = jnp.arange(nt, dtype=F32)[:, None, None]
    row_f = jnp.arange(TM, dtype=F32)[None, :, None]
    col = jnp.arange(aw, dtype=jnp.int32)[None, None, :] % LANES
    q_extra = (lanes3("qc") + row_f * lanes3("qr") + tile_f * lanes3("qt")).astype(BF16)
    k_extra = (lanes3("kc") + row_f * lanes3("kr") + tile_f * lanes3("kt")
               + jnp.where(col == X_SEL + tile_f.astype(jnp.int32), MASK_NEG, 0.0)).astype(BF16)

    xp = x_prompt.reshape(n * t, d)
    tile = lambda w: pl.BlockSpec((TM, w), lambda b, i, *_: (b * nt + i, 0))
    aw = N_HEADS * LANES
    vw = N_HEADS * V_ROWS
    blk3 = lambda r, c: pl.BlockSpec((1, r, c), lambda b, i: (b * nt + i, 0, 0))
    k_p, v_p, km, qt, ka, vt, mq, cact, utail = pl.pallas_call(
        _premix_kernel,
        grid=(n, nt),
        in_specs=[tile(d), _resident((1, d)), _resident((d, n_mix)), _resident((ATTN_W, ATTN_W)),
                  _resident((MEM_W, MEM_W)), _resident((1, ATTN_W)), _resident((1, ATTN_W)),
                  _resident((1, MEM_W)), _resident((CONV_WIDTH, CONV_CH)), _resident((1, CONV_CH)),
                  _resident((1, CONV_CH)), _resident((1, CONV_CH)),
                  pl.BlockSpec((1, TM, aw), lambda b, i: (i, 0, 0)),
                  pl.BlockSpec((1, TM, aw), lambda b, i: (i, 0, 0))],
        out_specs=[pl.BlockSpec((ATTN_W, TM), lambda b, i: (b, i)), pl.BlockSpec((ATTN_W, TM), lambda b, i: (b, i)),
                   pl.BlockSpec((1, 1, ATTN_W), lambda b, i: (b * nt + i, 0, 0)),
                   pl.BlockSpec((aw, TM), lambda b, i: (b, i)), blk3(TM, aw), blk3(vw, TM),
                   tile(MEM_W), tile(CONV_CH),
                   pl.BlockSpec((1, CONV_HALO, CONV_CH), lambda b, i: (b, 0, 0))],
        out_shape=[jax.ShapeDtypeStruct((n * ATTN_W, t), F32), jax.ShapeDtypeStruct((n * ATTN_W, t), F32),
                   jax.ShapeDtypeStruct((n * nt, 1, ATTN_W), F32),
                   jax.ShapeDtypeStruct((n * aw, t), BF16), jax.ShapeDtypeStruct((n * nt, TM, aw), BF16),
                   jax.ShapeDtypeStruct((n * nt, vw, TM), BF16), jax.ShapeDtypeStruct((n * t, MEM_W), BF16),
                   jax.ShapeDtypeStruct((n * t, CONV_CH), BF16),
                   jax.ShapeDtypeStruct((n, CONV_HALO, CONV_CH), F32)],
        scratch_shapes=[pltpu.VMEM((CONV_HALO + TM + SUBLANES, CONV_CH), F32)],
        compiler_params=_params(2),
        name="premix",
    )(xp, g1, w_mix, bd64, bd128, qg, kg, mqg, w_dw_a, bdw_a, lng, lnb, q_extra, k_extra)

    km4 = km.reshape(n, nt, N_HEADS, HEAD_DIM).transpose(0, 2, 1, 3)
    km4 = jnp.pad(km4, ((0, 0), (0, 0), (0, GROUP - nt), (0, LANES - HEAD_DIM)))
    kmt = (km4[:, :, :, None, :] * jnp.eye(N_HEADS, dtype=F32)[None, :, None, :, None])
    kmt = kmt.reshape(n, N_HEADS * GROUP, aw).astype(BF16)

    seq3 = lambda r, c: pl.BlockSpec((nt, r, c), lambda b, i: (b, 0, 0))
    ob = pl.pallas_call(
        _moba_kernel,
        grid=(n, nt),
        in_specs=[pl.BlockSpec((aw, TM), lambda b, i: (b, i)), seq3(TM, aw), seq3(vw, TM),
                  pl.BlockSpec((1, N_HEADS * GROUP, aw), lambda b, i: (b, 0, 0))],
        out_specs=tile(ATTN_W),
        out_shape=jax.ShapeDtypeStruct((n * t, ATTN_W), BF16),
        scratch_shapes=[pltpu.VMEM((N_HEADS, LANES, TM), BF16), pltpu.VMEM((N_HEADS, 1, TM), F32),
                        pltpu.VMEM((N_HEADS, V_ROWS, TM), F32)]
        + [pltpu.VMEM((TM, TM), F32)] * N_HEADS + [pltpu.VMEM((1, TM), F32)] * N_HEADS,
        compiler_params=_params(2),
        name="moba",
    )(qt, ka, vt, kmt)

    mem = mem_prompt.reshape(n * n_mem, d)
    mtile = lambda w: pl.BlockSpec((n_mem, w), lambda b: (b, 0))
    mk_p, mv_p, mkb, mvb = pl.pallas_call(
        _memkv_kernel,
        grid=(n,),
        in_specs=[mtile(d), _resident((1, d)), _resident((d, 2 * MEM_W)), _resident((MEM_W, MEM_W)),
                  _resident((1, MEM_W))],
        out_specs=[mtile(MEM_W)] * 4,
        out_shape=[jax.ShapeDtypeStruct((n * n_mem, MEM_W), F32)] * 2
        + [jax.ShapeDtypeStruct((n * n_mem, MEM_W), BF16)] * 2,
        compiler_params=_params(1),
        name="memkv",
    )(mem, row(mem_norm_g), w_mem_kv.astype(BF16), bd128, mkg)

    xs = x_sample.reshape(nd, d)
    st_conv = state_conv.transpose(1, 0, 2)
    st_ffn = state_ffn_conv.transpose(1, 0, 2)
    vm = pltpu.CompilerParams(vmem_limit_bytes=VMEM_LIMIT)
    u_s, q_s, k_s, v_s, mq_s, gl_s, cact_s = pl.pallas_call(
        _premix_s_kernel,
        out_shape=[jax.ShapeDtypeStruct((nd, CONV_CH), F32), jax.ShapeDtypeStruct((nd, ATTN_W), F32),
                   jax.ShapeDtypeStruct((nd, ATTN_W), F32), jax.ShapeDtypeStruct((nd, ATTN_W), F32),
                   jax.ShapeDtypeStruct((nd, MEM_W), F32), jax.ShapeDtypeStruct((nd, 3 * d), F32),
                   jax.ShapeDtypeStruct((nd, CONV_CH), BF16)],
        compiler_params=vm,
        name="premix_s",
    )(xs, g1, w_in_b, bd64, bd128, qg, kg, mqg, st_conv, w_dw_a, bdw_a, lng, lnb)

    ck = cache_k.transpose(0, 2, 3, 1)
    cv = cache_v.transpose(0, 2, 3, 1)
    pt_flat = page_table.reshape(-1).astype(jnp.int32)
    q_cols = jnp.broadcast_to(q_s.reshape(nd, N_HEADS, HEAD_DIM, 1), (nd, N_HEADS, HEAD_DIM, page_size))
    assert (n * nt) % nd == 0 and n_pages % ((n * nt) // nd) == 0 and n_blocks <= LANES
    steps_per_sample = (n * nt) // nd
    pages_per_step = n_pages // steps_per_sample
    assert pages_per_step % (MOBA_BLOCK // page_size) == 0

    memb = pl.BlockSpec((n_mem, MEM_W), lambda b, i, *_: (b, 0))
    x1, sel = pl.pallas_call(
        functools.partial(_postmix_kernel, pages_per_step=pages_per_step, n_blocks=n_blocks),
        grid_spec=pltpu.PrefetchScalarGridSpec(
            num_scalar_prefetch=1,
            grid=(n, nt),
            in_specs=[tile(d), tile(CONV_CH), tile(ATTN_W), tile(MEM_W), memb, memb, _resident((1, d)),
                      _resident((d, 3 * d)), _resident((1, 3 * d)), _resident((CONV_CH, d)),
                      _resident((ATTN_W, d)), _resident((MEM_W, d)), _resident((d, d)),
                      pl.BlockSpec((1, N_HEADS, HEAD_DIM, page_size),
                                   lambda b, i, pt: ((b * nt + i) // steps_per_sample, 0, 0, 0)),
                      pl.BlockSpec(memory_space=pl.ANY)],
            out_specs=[tile(d), pl.BlockSpec((1, N_HEADS, LANES),
                                             lambda b, i, pt: ((b * nt + i) // steps_per_sample, 0, 0))],
            scratch_shapes=[pltpu.VMEM((N_HEADS, LANES), F32),
                            pltpu.VMEM((2, pages_per_step, N_HEADS, HEAD_DIM, page_size), F32),
                            pltpu.SemaphoreType.DMA((2,))]),
        out_shape=[jax.ShapeDtypeStruct((n * t, d), F32), jax.ShapeDtypeStruct((nd, N_HEADS, LANES), jnp.int32)],
        compiler_params=_params(2),
        name="postmix",
    )(pt_flat, xp, cact, ob, mq, mkb, mvb, g1, w_gate, bg, wpa, wpb, wpc, wout, q_cols, ck)
    sel_flat = sel[:, :, :MOBA_TOPK].reshape(-1)

    chunk = dff // 2
    assert chunk % LANES == 0 and t % FFN_ROWS == 0
    nf = t // FFN_ROWS
    ftile = pl.BlockSpec((FFN_ROWS, d), lambda b, i: (b * nf + i, 0))
    y_p, ftail = pl.pallas_call(
        functools.partial(_ffn_kernel, chunk=chunk),
        grid=(n, nf),
        in_specs=[ftile, _resident((1, d)), _resident((d, 2 * dff)), _resident((FFN_CONV_WIDTH, 2 * dff)),
                  _resident((1, 2 * dff)), _resident((dff, d))],
        out_specs=[ftile, pl.BlockSpec((1, SUBLANES, 2 * dff), lambda b, i: (b, 0, 0))],
        out_shape=[jax.ShapeDtypeStruct((n * t, d), F32), jax.ShapeDtypeStruct((n, SUBLANES, 2 * dff), F32)],
        scratch_shapes=[pltpu.VMEM((SUBLANES + FFN_ROWS, 2 * dff), F32)],
        compiler_params=_params(2),
        name="ffn",
    )(x1, g2, wup, w_dw_f, bdw_f, wdown)

    row_spec = pl.BlockSpec((1, 1, ATTN_W), lambda b, *_: (b, 0, 0))
    rows3 = lambda a: a.reshape(nd, 1, -1)

    halves = jnp.arange(PAGES_PER_BLOCK, dtype=jnp.int32)
    sel_pages = PAGES_PER_BLOCK * sel[:, :, :MOBA_TOPK, None] + halves
    page_ids = jnp.take_along_axis(page_table.astype(jnp.int32), sel_pages.reshape(nd, -1), axis=1).reshape(-1)
    tiles_per_sample = N_HEADS * MOBA_TOPK * PAGES_PER_BLOCK
    tile_buf = pltpu.VMEM((2, tiles_per_sample, HEAD_DIM, page_size), F32)
    ob_s = pl.pallas_call(
        functools.partial(_decode_attn_kernel, past_len=past_len),
        grid_spec=pltpu.PrefetchScalarGridSpec(
            num_scalar_prefetch=2,
            grid=(nd,),
            in_specs=[pl.BlockSpec(memory_space=pltpu.SMEM), row_spec, row_spec, row_spec,
                      pl.BlockSpec(memory_space=pl.ANY), pl.BlockSpec(memory_space=pl.ANY)],
            out_specs=row_spec,
            scratch_shapes=[tile_buf, tile_buf, pltpu.SemaphoreType.DMA((2, 2))]),
        out_shape=jax.ShapeDtypeStruct((nd, 1, ATTN_W), F32),
        compiler_params=_params(1),
        name="decode_attn",
    )(page_ids, sel_flat, jnp.asarray(slopes, F32), rows3(q_s), rows3(k_s), rows3(v_s), ck, cv)
    ob_s = ob_s.reshape(nd, ATTN_W)

    group = SUBLANES
    assert nd % group == 0
    cm_spec = pl.BlockSpec((group, n_mem * MEM_HEADS, MEM_HEAD_DIM), lambda b: (b, 0, 0))
    mq_spec = pl.BlockSpec((group, 1, MEM_W), lambda b: (b, 0, 0))
    oc_s = pl.pallas_call(
        functools.partial(_memattn_s_kernel, group=group, n_mem=n_mem),
        grid=(nd // group,),
        in_specs=[mq_spec, cm_spec, cm_spec],
        out_specs=mq_spec,
        out_shape=jax.ShapeDtypeStruct((nd, 1, MEM_W), F32),
        compiler_params=_params(1),
        name="memattn_s",
    )(rows3(mq_s), cache_mem_k.reshape(nd, n_mem * MEM_HEADS, MEM_HEAD_DIM),
      cache_mem_v.reshape(nd, n_mem * MEM_HEADS, MEM_HEAD_DIM))
    oc_s = oc_s.reshape(nd, MEM_W)

    y_s, up_s = pl.pallas_call(
        _post_s_kernel,
        out_shape=[jax.ShapeDtypeStruct((nd, d), F32), jax.ShapeDtypeStruct((nd, 2 * dff), F32)],
        compiler_params=vm,
        name="post_s",
    )(xs, cact_s, ob_s, oc_s, gl_s, bg, wpa, wpb, wpc, wout, g2, wup, st_ffn, w_dw_f, bdw_f, wdown)

    heads = lambda a, b, s: a.reshape(b, s, N_HEADS, HEAD_DIM)
    from_t = lambda a: a.reshape(n, N_HEADS, HEAD_DIM, t).transpose(0, 3, 1, 2)
    conv_p = utail[:, CONV_HALO - (CONV_WIDTH - 1):, :]
    conv_s = jnp.concatenate([state_conv[:, 1:, :], u_s[:, None, :]], axis=1)
    ffn_p = ftail[:, SUBLANES - (FFN_CONV_WIDTH - 1):, :]
    ffn_s = jnp.concatenate([state_ffn_conv[:, 1:, :], up_s[:, None, :]], axis=1)
    return (y_p.reshape(n, t, d), y_s.reshape(nd, td, d),
            from_t(k_p), from_t(v_p), heads(k_s, nd, td), heads(v_s, nd, td),
            conv_p, conv_s, ffn_p, ffn_s,
            mk_p.reshape(n, n_mem, MEM_HEADS, MEM_HEAD_DIM), mv_p.reshape(n, n_mem, MEM_HEADS, MEM_HEAD_DIM))
```

```python
import functools

import numpy as np
import jax
import jax.numpy as jnp
from jax import lax
from jax.experimental import pallas as pl
from jax.experimental.pallas import tpu as pltpu

F32 = jnp.float32
BF16 = jnp.bfloat16

EPS = 1e-6
CONV_CH = 512
CONV_WIDTH = 31
N_HEADS = 8
HEAD_DIM = 64
ATTN_W = N_HEADS * HEAD_DIM
MOBA_BLOCK = 256
MOBA_TOPK = 3
MEM_HEADS = 4
MEM_HEAD_DIM = 128
MEM_W = MEM_HEADS * MEM_HEAD_DIM
FFN_CONV_WIDTH = 3
LANES = 128
SUBLANES = 8
TM = MOBA_BLOCK
CONV_HALO = 32
MASK_NEG = -float(2 ** 30)
GROUP = 16
V_ROWS = HEAD_DIM + 16
BLOCKS_PER_TRIP = 4
FFN_ROWS = 512
VMEM_LIMIT = 56 * 1024 * 1024

X_SEL = HEAD_DIM
X_PARTS = 3
X_RQ = HEAD_DIM + GROUP
X_RK = X_RQ + X_PARTS
X_TQ = X_RK + X_PARTS
X_TK = X_TQ + X_PARTS
LOG2E = 1.4426950408889634


def _dot(a, b):
    return jnp.dot(a, b, preferred_element_type=F32)


def _dot_nt(a, b):
    return lax.dot_general(a, b, (((1,), (1,)), ((), ())), preferred_element_type=F32)


def _rms_rows(x, g):
    return x * lax.rsqrt(jnp.mean(x * x, axis=-1, keepdims=True) + EPS) * g


def _group_rms(z, bd, g):
    sq = z * z
    hi = sq.astype(BF16)
    lo = (sq - hi.astype(F32)).astype(BF16)
    ms = _dot(hi, bd) + _dot(lo, bd)
    return z * lax.rsqrt(ms + EPS) * g


def _exact_zero(v):
    bits = pltpu.bitcast(v, jnp.uint32)
    half = jnp.uint32(16)
    return pltpu.bitcast(lax.shift_right_logical(lax.shift_right_logical(bits, half), half), F32)


def _sigmoid(x):
    return 1.0 / (1.0 + jnp.exp(-x))


def _layernorm_silu(c, g, b):
    mu = jnp.mean(c, axis=-1, keepdims=True)
    xc = c - mu
    var = jnp.mean(xc * xc, axis=-1, keepdims=True)
    y = xc * lax.rsqrt(var + EPS) * g + b
    return y * _sigmoid(y)


def _premix_kernel(x_ref, g1_ref, w_ref, bd64_ref, bd128_ref, qg_ref, kg_ref, mqg_ref,
                   wdw_ref, bdw_ref, lng_ref, lnb_ref, qx_ref, kx_ref,
                   k_ref, v_ref, km_ref, qt_ref, ka_ref, vt_ref, mq_ref, cact_ref, utail_ref,
                   ubuf):
    t = pl.program_id(1)
    nt = pl.num_programs(1)
    c = CONV_CH

    @pl.when(t == 0)
    def _():
        ubuf[0:CONV_HALO, :] = jnp.zeros((CONV_HALO, c), F32)
        ubuf[CONV_HALO + TM:, :] = jnp.zeros((SUBLANES, c), F32)

    hn = _rms_rows(x_ref[...], g1_ref[...]).astype(BF16)

    a = _dot(hn, w_ref[:, 0:c])
    g = _dot(hn, w_ref[:, c:2 * c])
    u = a * _sigmoid(g)
    ubuf[CONV_HALO:CONV_HALO + TM, :] = u
    o = 2 * c
    zq = _dot(hn, w_ref[:, o:o + ATTN_W])
    zk = _dot(hn, w_ref[:, o + ATTN_W:o + 2 * ATTN_W])
    zv = _dot(hn, w_ref[:, o + 2 * ATTN_W:o + 3 * ATTN_W])
    zm = _dot(hn, w_ref[:, o + 3 * ATTN_W:o + 3 * ATTN_W + MEM_W])
    acc = jnp.broadcast_to(bdw_ref[...], (TM, c))
    base = CONV_HALO - (CONV_WIDTH - 1)
    span = TM + 2 * SUBLANES
    for b in range(SUBLANES):
        part = None
        for k in range(b, CONV_WIDTH, SUBLANES):
            term = wdw_ref[k:k + 1, :] * ubuf[k - b:k - b + span, :]
            part = term if part is None else part + term
        acc = acc + part[base + b:base + b + TM, :]
    ubuf[0:CONV_HALO, :] = ubuf[TM:TM + CONV_HALO, :]
    cact_ref[...] = _layernorm_silu(acc, lng_ref[...], lnb_ref[...]).astype(BF16)

    qn = _group_rms(zq, bd64_ref[...], qg_ref[...]) * (HEAD_DIM ** -0.5)
    kn = _group_rms(zk, bd64_ref[...], kg_ref[...])
    mq = _group_rms(zm, bd128_ref[...], mqg_ref[...])

    ones_rows = jnp.where(lax.broadcasted_iota(jnp.int32, (V_ROWS - HEAD_DIM, TM), 0) == 0, 1.0, 0.0).astype(BF16)
    for cb in range(ATTN_W // LANES):
        k_ref[LANES * cb:LANES * (cb + 1), :] = kn[:, LANES * cb:LANES * (cb + 1)].T
        vt = zv[:, LANES * cb:LANES * (cb + 1)].T
        v_ref[LANES * cb:LANES * (cb + 1), :] = vt
        for sub in range(2):
            r0 = V_ROWS * (2 * cb + sub)
            vt_ref[0, r0:r0 + HEAD_DIM, :] = vt[HEAD_DIM * sub:HEAD_DIM * (sub + 1), :].astype(BF16)
            vt_ref[0, r0 + HEAD_DIM:r0 + V_ROWS, :] = ones_rows
    mq_ref[...] = mq.astype(BF16)
    km_ref[0] = jnp.mean(kn, axis=0, keepdims=True)

    lane = lax.broadcasted_iota(jnp.int32, (TM, LANES), 1)
    for hp in range(N_HEADS // 2):
        xq = qn[:, LANES * hp:LANES * (hp + 1)]
        xk = kn[:, LANES * hp:LANES * (hp + 1)] * LOG2E
        for sub in range(2):
            h = 2 * hp + sub
            if sub == 1:
                xq = pltpu.roll(xq, HEAD_DIM, 1)
                xk = pltpu.roll(xk, HEAD_DIM, 1)
            eq = qx_ref[0, :, LANES * h:LANES * (h + 1)].astype(F32)
            ek = kx_ref[0, :, LANES * h:LANES * (h + 1)].astype(F32)
            qt_ref[LANES * h:LANES * (h + 1), :] = jnp.where(lane < HEAD_DIM, xq, eq).T.astype(BF16)
            ka_ref[0, :, LANES * h:LANES * (h + 1)] = jnp.where(lane < HEAD_DIM, xk, ek).astype(BF16)

    @pl.when(t == nt - 1)
    def _():
        utail_ref[0] = ubuf[0:CONV_HALO, :]


def _moba_kernel(qt_ref, ka_ref, vt_ref, kmt_ref, o_ref, qh_scr, m_scr, acc_scr, *stage):
    i = pl.program_id(1)
    s_scr, mx_scr = stage[:N_HEADS], stage[N_HEADS:]
    key = lax.broadcasted_iota(jnp.int32, (TM, TM), 0)
    qry = lax.broadcasted_iota(jnp.int32, (TM, TM), 1)
    causal = key <= qry

    def scores(j, h, diagonal):
        q = qt_ref[LANES * h:LANES * (h + 1), :] if diagonal else qh_scr[h]
        s = _dot(ka_ref[j, :, LANES * h:LANES * (h + 1)], q)
        if diagonal:
            s = jnp.where(causal, s, -jnp.inf)
        s_scr[h][...] = s
        mx_scr[h][...] = jnp.max(s, axis=0, keepdims=True)

    for h in range(N_HEADS):
        scores(i, h, True)

    blk = lax.broadcasted_iota(jnp.int32, (N_HEADS, GROUP, TM), 1)
    past = blk < i
    gate = _dot(kmt_ref[0], qt_ref[...]).reshape(N_HEADS, GROUP, TM)
    g = jnp.where(past, gate, -jnp.inf)
    unselected = jnp.where(past, 1.0, 0.0)
    for _ in range(MOBA_TOPK):
        best = jnp.max(g, axis=1, keepdims=True)
        first = jnp.min(jnp.where(g == best, blk, GROUP), axis=1, keepdims=True)
        taken = blk == first
        g = jnp.where(taken, -jnp.inf, g)
        unselected = jnp.where(taken, 0.0, unselected)
    nsel = unselected.astype(BF16).reshape(N_HEADS * GROUP, TM)
    for h in range(N_HEADS):
        r0 = LANES * h
        qh_scr[h, 0:X_SEL, :] = qt_ref[r0:r0 + X_SEL, :]
        qh_scr[h, X_SEL:X_RQ, :] = nsel[GROUP * h:GROUP * (h + 1), :]
        qh_scr[h, X_RQ:LANES, :] = qt_ref[r0 + X_RQ:r0 + LANES, :]

    def accumulate(j, h):
        m_old = m_scr[h]
        m_new = jnp.maximum(m_old, mx_scr[h][...])
        alpha = jnp.exp2(m_old - m_new)
        p = jnp.exp2(s_scr[h][...] - m_new).astype(BF16)
        acc_scr[h] = alpha * acc_scr[h] + _dot(vt_ref[j, V_ROWS * h:V_ROWS * (h + 1), :], p)
        m_scr[h] = m_new

    m_scr[...] = jnp.full(m_scr.shape, -jnp.inf, F32)
    acc_scr[...] = jnp.zeros(acc_scr.shape, F32)

    def step(prev, j):
        for h in range(N_HEADS):
            accumulate(prev, h)
            scores(j, h, False)

    def steps(first, count):
        step(jnp.where(first == 0, i, first - 1), first)
        for extra in range(1, count):
            step(first + extra - 1, first + extra)

    def body(trip, carry):
        steps(trip * BLOCKS_PER_TRIP, BLOCKS_PER_TRIP)
        return carry

    lax.fori_loop(0, i // BLOCKS_PER_TRIP, body, 0)
    size = BLOCKS_PER_TRIP // 2
    while size:
        done = (i // (2 * size)) * (2 * size)

        @pl.when((i // size) % 2 == 1)
        def _(done=done, size=size):
            steps(done, size)

        size //= 2
    last = jnp.where(i == 0, i, i - 1)
    for h in range(N_HEADS):
        accumulate(last, h)
    outs = []
    for h in range(N_HEADS):
        acc = acc_scr[h]
        outs.append(acc[0:HEAD_DIM, :] / acc[HEAD_DIM:HEAD_DIM + 1, :])
    o_ref[...] = jnp.concatenate(outs, axis=0).T.astype(BF16)


def _memkv_kernel(mem_ref, g_ref, w_ref, bd128_ref, mkg_ref, mk_ref, mv_ref, mkb_ref, mvb_ref):
    hn = _rms_rows(mem_ref[...], g_ref[...]).astype(BF16)
    mk = _group_rms(_dot(hn, w_ref[:, 0:MEM_W]), bd128_ref[...], mkg_ref[...])
    mv = _dot(hn, w_ref[:, MEM_W:2 * MEM_W])
    mk_ref[...] = mk
    mv_ref[...] = mv
    mkb_ref[...] = mk.astype(BF16)
    mvb_ref[...] = mv.astype(BF16)


def _mem_attend_rows(mq, mk, mv):
    heads = [slice(MEM_HEAD_DIM * hh, MEM_HEAD_DIM * (hh + 1)) for hh in range(MEM_HEADS)]
    scores = [_dot_nt(mq[:, sl], mk[:, sl]) * (MEM_HEAD_DIM ** -0.5) for sl in heads]
    probs = []
    for s in scores:
        p = jnp.exp(s - jnp.max(s, axis=-1, keepdims=True))
        probs.append((p.astype(BF16), jnp.sum(p, axis=-1, keepdims=True)))
    return jnp.concatenate([_dot(p, mv[:, sl]) / l for (p, l), sl in zip(probs, heads)], axis=-1)


def _merge_out(x, gl, ya, yb, yc, wout):
    d = x.shape[-1]
    merged = (_sigmoid(gl[:, 0:d]) * ya + _sigmoid(gl[:, d:2 * d]) * yb + _sigmoid(gl[:, 2 * d:3 * d]) * yc)
    return x + _dot(merged.astype(BF16), wout)


def _postmix_kernel(pt_ref, x_ref, cact_ref, ob_ref, mq_ref, mk_ref, mv_ref, g1_ref, wg_ref, bg_ref,
                    wpa_ref, wpb_ref, wpc_ref, wout_ref, q_ref, ck_hbm, x1_ref, sel_ref,
                    gate_scr, page_buf, page_sem, *, pages_per_step, n_blocks):
    step = pl.program_id(0) * pl.num_programs(1) + pl.program_id(1)
    n_steps = pl.num_programs(0) * pl.num_programs(1)
    slot = step % 2

    def page_copy(page, to_slot, r):
        return pltpu.make_async_copy(ck_hbm.at[page], page_buf.at[to_slot, r], page_sem.at[to_slot])

    def fetch(for_step, to_slot):
        for r in range(pages_per_step):
            page_copy(pt_ref[for_step * pages_per_step + r], to_slot, r).start()

    @pl.when(step == 0)
    def _():
        gate_scr[...] = jnp.zeros(gate_scr.shape, F32)
        fetch(0, 0)

    for r in range(pages_per_step):
        page_copy(0, slot, r).wait()
    page_refs = [page_buf.at[slot, r] for r in range(pages_per_step)]

    blocks_per_step = pages_per_step // PAGES_PER_BLOCK
    steps_per_sample = n_blocks // blocks_per_step
    share = step % steps_per_sample

    g = jnp.where(share == 0, 0.0, gate_scr[...])
    x = x_ref[...]
    d = x.shape[-1]
    hn = _rms_rows(x, g1_ref[...]).astype(BF16)
    oc = _mem_attend_rows(mq_ref[...], mk_ref[...], mv_ref[...]).astype(BF16)
    fetch(jnp.minimum(step + 1, n_steps - 1), 1 - slot)
    g = _gate_scores(g, share * blocks_per_step, q_ref, page_refs)
    tie = jnp.concatenate([_exact_zero(g)[0:1, :]] * (d // LANES), axis=1)
    sources = ((cact_ref[...], wpa_ref), (ob_ref[...], wpb_ref), (oc, wpc_ref))
    ys = [_dot(src, w_ref[...]) for src, w_ref in sources]
    merged = None
    for br in range(len(sources)):
        cols = slice(d * br, d * (br + 1))
        gl = _dot(hn, wg_ref[:, cols]) + (bg_ref[:, cols] + tie)
        term = _sigmoid(gl) * ys[br]
        merged = term if merged is None else merged + term
    x1_ref[...] = x + _dot(merged.astype(BF16), wout_ref[...])
    gate_scr[...] = g
    _gate_select(share == steps_per_sample - 1, gate_scr, sel_ref, n_blocks)

    @pl.when(step == n_steps - 1)
    def _():
        for r in range(pages_per_step):
            page_copy(0, 1 - slot, r).wait()


def _ffn_kernel(x1_ref, g2_ref, wup_ref, wdw_ref, bdw_ref, wdown_ref, y_ref, tail_ref, upbuf, *, chunk):
    rows = x1_ref.shape[0]
    t = pl.program_id(1)
    nt = pl.num_programs(1)
    dff = wdown_ref.shape[0]

    @pl.when(t == 0)
    def _():
        upbuf[0:SUBLANES, :] = jnp.zeros((SUBLANES, 2 * dff), F32)

    x1 = x1_ref[...]
    hn = _rms_rows(x1, g2_ref[...]).astype(BF16)

    for c in range(0, 2 * dff, chunk):
        upbuf[SUBLANES:SUBLANES + rows, c:c + chunk] = _dot(hn, wup_ref[:, c:c + chunk])

    def conv(c):
        out = bdw_ref[:, c:c + chunk]
        for k in range(FFN_CONV_WIDTH):
            r0 = SUBLANES - (FFN_CONV_WIDTH - 1) + k
            out = out + wdw_ref[k:k + 1, c:c + chunk] * upbuf[r0:r0 + rows, c:c + chunk]
        return out

    y = x1
    for c in range(0, dff, chunk):
        a = conv(c)
        b = conv(dff + c)
        act = (a * _sigmoid(a) * b).astype(BF16)
        y = y + _dot(act, wdown_ref[c:c + chunk, :])
    y_ref[...] = y

    @pl.when(t == nt - 1)
    def _():
        tail_ref[0] = upbuf[rows:rows + SUBLANES, :]

    upbuf[0:SUBLANES, :] = upbuf[rows:rows + SUBLANES, :]


def _premix_s_kernel(x_ref, g1_ref, w_ref, bd64_ref, bd128_ref, qg_ref, kg_ref, mqg_ref,
                     st_ref, wdw_ref, bdw_ref, lng_ref, lnb_ref,
                     u_ref, q_ref, k_ref, v_ref, mq_ref, gl_ref, cact_ref):
    c = CONV_CH
    hn = _rms_rows(x_ref[...], g1_ref[...]).astype(BF16)
    a = _dot(hn, w_ref[:, 0:c])
    g = _dot(hn, w_ref[:, c:2 * c])
    u = a * _sigmoid(g)
    u_ref[...] = u
    acc = bdw_ref[...] + wdw_ref[CONV_WIDTH - 1:CONV_WIDTH, :] * u
    for k in range(CONV_WIDTH - 1):
        acc = acc + wdw_ref[k:k + 1, :] * st_ref[k]
    cact_ref[...] = _layernorm_silu(acc, lng_ref[...], lnb_ref[...]).astype(BF16)

    o = 2 * c
    q_ref[...] = _group_rms(_dot(hn, w_ref[:, o:o + ATTN_W]), bd64_ref[...], qg_ref[...])
    o += ATTN_W
    k_ref[...] = _group_rms(_dot(hn, w_ref[:, o:o + ATTN_W]), bd64_ref[...], kg_ref[...])
    o += ATTN_W
    v_ref[...] = _dot(hn, w_ref[:, o:o + ATTN_W])
    o += ATTN_W
    mq_ref[...] = _group_rms(_dot(hn, w_ref[:, o:o + MEM_W]), bd128_ref[...], mqg_ref[...])
    o += MEM_W
    gl_ref[...] = _dot(hn, w_ref[:, o:])


PAGES_PER_BLOCK = MOBA_BLOCK // LANES


def _gate_scores(g, first_block, q_ref, page_refs):
    head = lax.broadcasted_iota(jnp.int32, (N_HEADS, LANES), 0)
    lane = lax.broadcasted_iota(jnp.int32, (HEAD_DIM, LANES), 1)
    for h in range(N_HEADS):
        kmean = jnp.zeros((HEAD_DIM, LANES), F32)
        for r in range(0, len(page_refs), PAGES_PER_BLOCK):
            ksum = page_refs[r][h]
            for extra in range(1, PAGES_PER_BLOCK):
                ksum = ksum + page_refs[r + extra][h]
            col = jnp.sum(ksum, axis=1, keepdims=True) * (1.0 / MOBA_BLOCK)
            kmean = jnp.where(lane == first_block + r // PAGES_PER_BLOCK, col, kmean)
        prod = q_ref[0, h].astype(BF16).astype(F32) * kmean.astype(BF16).astype(F32)
        g = g + jnp.where(head == h, jnp.sum(prod, axis=0, keepdims=True), 0.0)
    return g


def _gate_select(last, gate_scr, sel_ref, n_blocks):
    @pl.when(last)
    def _():
        g = gate_scr[:, 0:n_blocks]
        bl = lax.broadcasted_iota(jnp.int32, (N_HEADS, n_blocks), 1)
        rank = jnp.zeros((N_HEADS, n_blocks), F32)
        for b in range(n_blocks):
            other = g[:, b:b + 1]
            rank = rank + jnp.where(bl > b, jnp.where(other >= g, 1.0, 0.0), jnp.where(other > g, 1.0, 0.0))
        lane_o = lax.broadcasted_iota(jnp.int32, (N_HEADS, LANES), 1)
        out = jnp.zeros((N_HEADS, LANES), F32)
        blf = bl.astype(F32)
        for r in range(MOBA_TOPK):
            idx = jnp.sum(jnp.where(rank == float(r), blf, 0.0), axis=-1, keepdims=True)
            out = jnp.where(lane_o == r, idx, out)
        sel_ref[0] = out.astype(jnp.int32)


def _decode_attn_kernel(page_ref, sel_ref, slope_ref, q_ref, kown_ref, vown_ref, mq_ref, mk_ref, mv_ref,
                        ck_hbm, cv_hbm, o_ref, oc_ref, k_buf, v_buf, sem, *, past_len, n_mem):
    pages_per_block = MOBA_BLOCK // LANES
    tiles_per_head = MOBA_TOPK * pages_per_block
    n_tiles = N_HEADS * tiles_per_head
    n = pl.program_id(0)
    slot = n % 2

    def tile_copies(page, to_slot, tile):
        h = tile // tiles_per_head
        return (pltpu.make_async_copy(ck_hbm.at[page, h], k_buf.at[to_slot, tile], sem.at[0, to_slot]),
                pltpu.make_async_copy(cv_hbm.at[page, h], v_buf.at[to_slot, tile], sem.at[1, to_slot]))

    def fetch(sample, to_slot):
        for tile in range(n_tiles):
            for copy in tile_copies(page_ref[sample * n_tiles + tile], to_slot, tile):
                copy.start()

    @pl.when(n == 0)
    def _():
        fetch(0, 0)

    @pl.when(n + 1 < pl.num_programs(0))
    def _():
        fetch(n + 1, 1 - slot)

    for tile in range(n_tiles):
        for copy in tile_copies(0, slot, tile):
            copy.wait()
    k_refs = [k_buf.at[slot, tile] for tile in range(n_tiles)]
    v_refs = [v_buf.at[slot, tile] for tile in range(n_tiles)]

    lane = lax.broadcasted_iota(jnp.int32, (1, LANES), 1)
    lanef = lane.astype(F32)
    zeros = jnp.zeros((HEAD_DIM, tiles_per_head * LANES), BF16)
    scale = HEAD_DIM ** -0.5

    def head_tiles(tile_refs, h):
        t = jnp.concatenate([tile_refs[h * tiles_per_head + idx][...] for idx in range(tiles_per_head)], axis=1)
        t = t.astype(BF16)
        return jnp.concatenate([t, zeros] if h % 2 == 0 else [zeros, t], axis=0)

    def own_half(h):
        return (lane < HEAD_DIM) if h % 2 == 0 else (lane >= HEAD_DIM)

    def pair_lanes(ref, h):
        return ref[0][:, LANES * (h // 2):LANES * (h // 2 + 1)]

    qz, raw = [], []
    for h in range(N_HEADS):
        qz.append(jnp.where(own_half(h), pair_lanes(q_ref, h) * scale, 0.0).astype(BF16))
        raw.append(_dot(jnp.broadcast_to(qz[h], (SUBLANES, LANES)), head_tiles(k_refs, h))[0:1, :])
    mem_scores = []
    for hh in range(MEM_HEADS):
        mq = jnp.broadcast_to(mq_ref[0][:, MEM_HEAD_DIM * hh:MEM_HEAD_DIM * (hh + 1)], (SUBLANES, MEM_HEAD_DIM))
        mk = mk_ref[0, pl.ds(hh, n_mem, stride=MEM_HEADS), :].astype(BF16)
        mem_scores.append(_dot_nt(mq.astype(BF16), mk)[0:1, :] * (MEM_HEAD_DIM ** -0.5))
    mem_probs = []
    for sc in mem_scores:
        p = jnp.exp(sc - jnp.max(sc, axis=-1, keepdims=True))
        mem_probs.append((p, jnp.sum(p, axis=-1, keepdims=True)))
    probs = []
    for h in range(N_HEADS):
        dist = []
        for r in range(MOBA_TOPK):
            blk = sel_ref[(n * N_HEADS + h) * MOBA_TOPK + r]
            for half in range(pages_per_block):
                pos0 = (blk * MOBA_BLOCK + half * LANES).astype(F32)
                dist.append(float(past_len) - (pos0 + lanef))
        logits = raw[h] - slope_ref[h] * jnp.concatenate(dist, axis=1)
        kown = pair_lanes(kown_ref, h).astype(BF16).astype(F32)
        s_own = jnp.sum(qz[h].astype(F32) * kown, axis=-1, keepdims=True)
        m = jnp.maximum(s_own, jnp.max(logits, axis=-1, keepdims=True))
        p_own = jnp.exp(s_own - m)
        p = jnp.exp(logits - m)
        probs.append((p, p_own, p_own + jnp.sum(p, axis=-1, keepdims=True)))
    outs = []
    for h in range(N_HEADS):
        p, p_own, l = probs[h]
        p8 = jnp.broadcast_to(p.astype(BF16), (SUBLANES, p.shape[1]))
        vown = jnp.where(own_half(h), pair_lanes(vown_ref, h).astype(BF16).astype(F32), 0.0)
        acc = p_own.astype(BF16).astype(F32) * vown + _dot_nt(p8, head_tiles(v_refs, h))[0:1, :]
        outs.append(acc / l)
    o_ref[0] = jnp.concatenate([outs[h] + outs[h + 1] for h in range(0, N_HEADS, 2)], axis=1)
    mem_outs = []
    for hh, (p, l) in enumerate(mem_probs):
        mv = mv_ref[0, pl.ds(hh, n_mem, stride=MEM_HEADS), :].astype(BF16)
        p8 = jnp.broadcast_to(p.astype(BF16), (SUBLANES, n_mem))
        mem_outs.append(_dot(p8, mv)[0:1, :] / l)
    oc_ref[0] = jnp.concatenate(mem_outs, axis=-1)


def _post_s_kernel(x_ref, cact_ref, ob_ref, oc_ref, gl_ref, bg_ref, wpa_ref, wpb_ref, wpc_ref, wout_ref,
                   g2_ref, wup_ref, st_ref, wdw_ref, bdw_ref, wdown_ref, y_ref, up_ref):
    x = x_ref[...]
    dff = wdown_ref.shape[0]
    ya = _dot(cact_ref[...], wpa_ref[...])
    yb = _dot(ob_ref[...].astype(BF16), wpb_ref[...])
    yc = _dot(oc_ref[...].astype(BF16), wpc_ref[...])
    x1 = _merge_out(x, gl_ref[...] + bg_ref[...], ya, yb, yc, wout_ref[...])
    up = _dot(_rms_rows(x1, g2_ref[...]).astype(BF16), wup_ref[...])
    up_ref[...] = up
    cv = bdw_ref[...] + wdw_ref[FFN_CONV_WIDTH - 1:FFN_CONV_WIDTH, :] * up
    for k in range(FFN_CONV_WIDTH - 1):
        cv = cv + wdw_ref[k:k + 1, :] * st_ref[k]
    a = cv[:, 0:dff]
    b = cv[:, dff:]
    act = (a * _sigmoid(a) * b).astype(BF16)
    y_ref[...] = x1 + _dot(act, wdown_ref[...])


def _round_to_bf16(x):
    bits = np.asarray(x, np.float32).view(np.uint32)
    bits = (bits + np.uint32(0x7FFF) + ((bits >> np.uint32(16)) & np.uint32(1))) & np.uint32(0xFFFF0000)
    return bits.view(np.float32)


def _resident(shape):
    nd = len(shape)
    return pl.BlockSpec(shape, lambda *_: (0,) * nd, pipeline_mode=pl.Buffered(1))


def _params(n_axes):
    return pltpu.CompilerParams(dimension_semantics=("arbitrary",) * n_axes, vmem_limit_bytes=VMEM_LIMIT)


def kernel(x_prompt, x_sample, mem_prompt, cache_k, cache_v, page_table, state_conv, state_ffn_conv, cache_mem_k, cache_mem_v, norm1_g, w_in, b_gate, w_dw_a, b_dw_a, ln_a_g, ln_a_b, w_proj_a, q_norm_g, k_norm_g, w_proj_b, mem_norm_g, w_mem_kv, mq_norm_g, mk_norm_g, w_proj_c, w_out, norm2_g, w_up, w_dw_f, b_dw_f, w_down):
    n, t, d = x_prompt.shape
    nd, td, _ = x_sample.shape
    n_mem = mem_prompt.shape[1]
    n_pool, page_size = cache_k.shape[:2]
    n_pages = page_table.shape[1]
    past_len = n_pages * page_size
    dff = w_down.shape[0]
    nt = t // TM
    n_mix = 2 * CONV_CH + 3 * ATTN_W + MEM_W
    assert t % TM == 0 and nt <= GROUP and td == 1
    assert past_len % MOBA_BLOCK == 0 and MOBA_BLOCK == 2 * page_size and page_size == LANES
    n_blocks = past_len // MOBA_BLOCK

    row = lambda v: v.reshape(1, -1).astype(F32)
    w_in_b = w_in.astype(BF16)
    w_mix, w_gate = w_in_b[:, :n_mix], w_in_b[:, n_mix:]
    wpa, wpb, wpc = w_proj_a.astype(BF16), w_proj_b.astype(BF16), w_proj_c.astype(BF16)
    wout, wup, wdown = w_out.astype(BF16), w_up.astype(BF16), w_down.astype(BF16)
    g1, g2 = row(norm1_g), row(norm2_g)
    qg = row(jnp.tile(q_norm_g, N_HEADS))
    kg = row(jnp.tile(k_norm_g, N_HEADS))
    mqg = row(jnp.tile(mq_norm_g, MEM_HEADS))
    mkg = row(jnp.tile(mk_norm_g, MEM_HEADS))
    bdw_a, lng, lnb, bg, bdw_f = row(b_dw_a), row(ln_a_g), row(ln_a_b), row(b_gate), row(b_dw_f)
    grp = np.arange(ATTN_W)
    bd64 = jnp.asarray((grp[:, None] // HEAD_DIM == grp[None, :] // HEAD_DIM) / HEAD_DIM, BF16)
    bd128 = jnp.asarray((grp[:, None] // MEM_HEAD_DIM == grp[None, :] // MEM_HEAD_DIM) / MEM_HEAD_DIM, BF16)
    slopes = 2.0 ** (-8.0 * np.arange(1, N_HEADS + 1) / N_HEADS)
    aw = N_HEADS * LANES
    coef = {name: np.zeros((N_HEADS, LANES), np.float32) for name in ("qc", "qr", "qt", "kc", "kr", "kt")}
    rest = (slopes * LOG2E).astype(np.float32)
    for p in range(X_PARTS):
        part = _round_to_bf16(rest)
        rest = rest - part
        coef["qr"][:, X_RQ + p] = 1.0
        coef["kc"][:, X_RQ + p] = -part
        coef["qc"][:, X_RK + p] = part
        coef["kr"][:, X_RK + p] = 1.0
        coef["qt"][:, X_TQ + p] = 1.0
        coef["kc"][:, X_TQ + p] = -part * MOBA_BLOCK
        coef["qc"][:, X_TK + p] = part
        coef["kt"][:, X_TK + p] = MOBA_BLOCK
    lanes3 = lambda name: jnp.asarray(coef[name].reshape(1, 1, aw))
    tile_f = jnp.arange(nt, dtype=F32)[:, None, None]
    row_f = jnp.arange(TM, dtype=F32)[None, :, None]
    col = jnp.arange(aw, dtype=jnp.int32)[None, None, :] % LANES
    q_extra = (lanes3("qc") + row_f * lanes3("qr") + tile_f * lanes3("qt")).astype(BF16)
    k_extra = (lanes3("kc") + row_f * lanes3("kr") + tile_f * lanes3("kt")
               + jnp.where(col == X_SEL + tile_f.astype(jnp.int32), MASK_NEG, 0.0)).astype(BF16)

    xp = x_prompt.reshape(n * t, d)
    tile = lambda w: pl.BlockSpec((TM, w), lambda b, i, *_: (b * nt + i, 0))
    aw = N_HEADS * LANES
    vw = N_HEADS * V_ROWS
    blk3 = lambda r, c: pl.BlockSpec((1, r, c), lambda b, i: (b * nt + i, 0, 0))
    k_p, v_p, km, qt, ka, vt, mq, cact, utail = pl.pallas_call(
        _premix_kernel,
        grid=(n, nt),
        in_specs=[tile(d), _resident((1, d)), _resident((d, n_mix)), _resident((ATTN_W, ATTN_W)),
                  _resident((MEM_W, MEM_W)), _resident((1, ATTN_W)), _resident((1, ATTN_W)),
                  _resident((1, MEM_W)), _resident((CONV_WIDTH, CONV_CH)), _resident((1, CONV_CH)),
                  _resident((1, CONV_CH)), _resident((1, CONV_CH)),
                  pl.BlockSpec((1, TM, aw), lambda b, i: (i, 0, 0)),
                  pl.BlockSpec((1, TM, aw), lambda b, i: (i, 0, 0))],
        out_specs=[pl.BlockSpec((ATTN_W, TM), lambda b, i: (b, i)), pl.BlockSpec((ATTN_W, TM), lambda b, i: (b, i)),
                   pl.BlockSpec((1, 1, ATTN_W), lambda b, i: (b * nt + i, 0, 0)),
                   pl.BlockSpec((aw, TM), lambda b, i: (b, i)), blk3(TM, aw), blk3(vw, TM),
                   tile(MEM_W), tile(CONV_CH),
                   pl.BlockSpec((1, CONV_HALO, CONV_CH), lambda b, i: (b, 0, 0))],
        out_shape=[jax.ShapeDtypeStruct((n * ATTN_W, t), F32), jax.ShapeDtypeStruct((n * ATTN_W, t), F32),
                   jax.ShapeDtypeStruct((n * nt, 1, ATTN_W), F32),
                   jax.ShapeDtypeStruct((n * aw, t), BF16), jax.ShapeDtypeStruct((n * nt, TM, aw), BF16),
                   jax.ShapeDtypeStruct((n * nt, vw, TM), BF16), jax.ShapeDtypeStruct((n * t, MEM_W), BF16),
                   jax.ShapeDtypeStruct((n * t, CONV_CH), BF16),
                   jax.ShapeDtypeStruct((n, CONV_HALO, CONV_CH), F32)],
        scratch_shapes=[pltpu.VMEM((CONV_HALO + TM + SUBLANES, CONV_CH), F32)],
        compiler_params=_params(2),
        name="premix",
    )(xp, g1, w_mix, bd64, bd128, qg, kg, mqg, w_dw_a, bdw_a, lng, lnb, q_extra, k_extra)

    km4 = km.reshape(n, nt, N_HEADS, HEAD_DIM).transpose(0, 2, 1, 3)
    km4 = jnp.pad(km4, ((0, 0), (0, 0), (0, GROUP - nt), (0, LANES - HEAD_DIM)))
    kmt = (km4[:, :, :, None, :] * jnp.eye(N_HEADS, dtype=F32)[None, :, None, :, None])
    kmt = kmt.reshape(n, N_HEADS * GROUP, aw).astype(BF16)

    seq3 = lambda r, c: pl.BlockSpec((nt, r, c), lambda b, i: (b, 0, 0))
    ob = pl.pallas_call(
        _moba_kernel,
        grid=(n, nt),
        in_specs=[pl.BlockSpec((aw, TM), lambda b, i: (b, i)), seq3(TM, aw), seq3(vw, TM),
                  pl.BlockSpec((1, N_HEADS * GROUP, aw), lambda b, i: (b, 0, 0))],
        out_specs=tile(ATTN_W),
        out_shape=jax.ShapeDtypeStruct((n * t, ATTN_W), BF16),
        scratch_shapes=[pltpu.VMEM((N_HEADS, LANES, TM), BF16), pltpu.VMEM((N_HEADS, 1, TM), F32),
                        pltpu.VMEM((N_HEADS, V_ROWS, TM), F32)]
        + [pltpu.VMEM((TM, TM), F32)] * N_HEADS + [pltpu.VMEM((1, TM), F32)] * N_HEADS,
        compiler_params=_params(2),
        name="moba",
    )(qt, ka, vt, kmt)

    mem = mem_prompt.reshape(n * n_mem, d)
    mtile = lambda w: pl.BlockSpec((n_mem, w), lambda b: (b, 0))
    mk_p, mv_p, mkb, mvb = pl.pallas_call(
        _memkv_kernel,
        grid=(n,),
        in_specs=[mtile(d), _resident((1, d)), _resident((d, 2 * MEM_W)), _resident((MEM_W, MEM_W)),
                  _resident((1, MEM_W))],
        out_specs=[mtile(MEM_W)] * 4,
        out_shape=[jax.ShapeDtypeStruct((n * n_mem, MEM_W), F32)] * 2
        + [jax.ShapeDtypeStruct((n * n_mem, MEM_W), BF16)] * 2,
        compiler_params=_params(1),
        name="memkv",
    )(mem, row(mem_norm_g), w_mem_kv.astype(BF16), bd128, mkg)

    xs = x_sample.reshape(nd, d)
    st_conv = state_conv.transpose(1, 0, 2)
    st_ffn = state_ffn_conv.transpose(1, 0, 2)
    vm = pltpu.CompilerParams(vmem_limit_bytes=VMEM_LIMIT)
    u_s, q_s, k_s, v_s, mq_s, gl_s, cact_s = pl.pallas_call(
        _premix_s_kernel,
        out_shape=[jax.ShapeDtypeStruct((nd, CONV_CH), F32), jax.ShapeDtypeStruct((nd, ATTN_W), F32),
                   jax.ShapeDtypeStruct((nd, ATTN_W), F32), jax.ShapeDtypeStruct((nd, ATTN_W), F32),
                   jax.ShapeDtypeStruct((nd, MEM_W), F32), jax.ShapeDtypeStruct((nd, 3 * d), F32),
                   jax.ShapeDtypeStruct((nd, CONV_CH), BF16)],
        compiler_params=vm,
        name="premix_s",
    )(xs, g1, w_in_b, bd64, bd128, qg, kg, mqg, st_conv, w_dw_a, bdw_a, lng, lnb)

    ck = cache_k.transpose(0, 2, 3, 1)
    cv = cache_v.transpose(0, 2, 3, 1)
    pt_flat = page_table.reshape(-1).astype(jnp.int32)
    q_cols = jnp.broadcast_to(q_s.reshape(nd, N_HEADS, HEAD_DIM, 1), (nd, N_HEADS, HEAD_DIM, page_size))
    assert (n * nt) % nd == 0 and n_pages % ((n * nt) // nd) == 0 and n_blocks <= LANES
    steps_per_sample = (n * nt) // nd
    pages_per_step = n_pages // steps_per_sample
    assert pages_per_step % (MOBA_BLOCK // page_size) == 0

    memb = pl.BlockSpec((n_mem, MEM_W), lambda b, i, *_: (b, 0))
    x1, sel = pl.pallas_call(
        functools.partial(_postmix_kernel, pages_per_step=pages_per_step, n_blocks=n_blocks),
        grid_spec=pltpu.PrefetchScalarGridSpec(
            num_scalar_prefetch=1,
            grid=(n, nt),
            in_specs=[tile(d), tile(CONV_CH), tile(ATTN_W), tile(MEM_W), memb, memb, _resident((1, d)),
                      _resident((d, 3 * d)), _resident((1, 3 * d)), _resident((CONV_CH, d)),
                      _resident((ATTN_W, d)), _resident((MEM_W, d)), _resident((d, d)),
                      pl.BlockSpec((1, N_HEADS, HEAD_DIM, page_size),
                                   lambda b, i, pt: ((b * nt + i) // steps_per_sample, 0, 0, 0)),
                      pl.BlockSpec(memory_space=pl.ANY)],
            out_specs=[tile(d), pl.BlockSpec((1, N_HEADS, LANES),
                                             lambda b, i, pt: ((b * nt + i) // steps_per_sample, 0, 0))],
            scratch_shapes=[pltpu.VMEM((N_HEADS, LANES), F32),
                            pltpu.VMEM((2, pages_per_step, N_HEADS, HEAD_DIM, page_size), F32),
                            pltpu.SemaphoreType.DMA((2,))]),
        out_shape=[jax.ShapeDtypeStruct((n * t, d), F32), jax.ShapeDtypeStruct((nd, N_HEADS, LANES), jnp.int32)],
        compiler_params=_params(2),
        name="postmix",
    )(pt_flat, xp, cact, ob, mq, mkb, mvb, g1, w_gate, bg, wpa, wpb, wpc, wout, q_cols, ck)
    sel_flat = sel[:, :, :MOBA_TOPK].reshape(-1)

    chunk = dff // 2
    assert chunk % LANES == 0 and t % FFN_ROWS == 0
    nf = t // FFN_ROWS
    ftile = pl.BlockSpec((FFN_ROWS, d), lambda b, i: (b * nf + i, 0))
    y_p, ftail = pl.pallas_call(
        functools.partial(_ffn_kernel, chunk=chunk),
        grid=(n, nf),
        in_specs=[ftile, _resident((1, d)), _resident((d, 2 * dff)), _resident((FFN_CONV_WIDTH, 2 * dff)),
                  _resident((1, 2 * dff)), _resident((dff, d))],
        out_specs=[ftile, pl.BlockSpec((1, SUBLANES, 2 * dff), lambda b, i: (b, 0, 0))],
        out_shape=[jax.ShapeDtypeStruct((n * t, d), F32), jax.ShapeDtypeStruct((n, SUBLANES, 2 * dff), F32)],
        scratch_shapes=[pltpu.VMEM((SUBLANES + FFN_ROWS, 2 * dff), F32)],
        compiler_params=_params(2),
        name="ffn",
    )(x1, g2, wup, w_dw_f, bdw_f, wdown)

    row_spec = pl.BlockSpec((1, 1, ATTN_W), lambda b, *_: (b, 0, 0))
    rows3 = lambda a: a.reshape(nd, 1, -1)

    halves = jnp.arange(PAGES_PER_BLOCK, dtype=jnp.int32)
    sel_pages = PAGES_PER_BLOCK * sel[:, :, :MOBA_TOPK, None] + halves
    page_ids = jnp.take_along_axis(page_table.astype(jnp.int32), sel_pages.reshape(nd, -1), axis=1).reshape(-1)
    tiles_per_sample = N_HEADS * MOBA_TOPK * PAGES_PER_BLOCK
    tile_buf = pltpu.VMEM((2, tiles_per_sample, HEAD_DIM, page_size), F32)
    mem_row = pl.BlockSpec((1, 1, MEM_W), lambda b, *_: (b, 0, 0))
    cm_spec = pl.BlockSpec((1, n_mem * MEM_HEADS, MEM_HEAD_DIM), lambda b, *_: (b, 0, 0))
    ob_s, oc_s = pl.pallas_call(
        functools.partial(_decode_attn_kernel, past_len=past_len, n_mem=n_mem),
        grid_spec=pltpu.PrefetchScalarGridSpec(
            num_scalar_prefetch=2,
            grid=(nd,),
            in_specs=[pl.BlockSpec(memory_space=pltpu.SMEM), row_spec, row_spec, row_spec,
                      mem_row, cm_spec, cm_spec,
                      pl.BlockSpec(memory_space=pl.ANY), pl.BlockSpec(memory_space=pl.ANY)],
            out_specs=[row_spec, mem_row],
            scratch_shapes=[tile_buf, tile_buf, pltpu.SemaphoreType.DMA((2, 2))]),
        out_shape=[jax.ShapeDtypeStruct((nd, 1, ATTN_W), F32), jax.ShapeDtypeStruct((nd, 1, MEM_W), F32)],
        compiler_params=_params(1),
        name="decode_attn",
    )(page_ids, sel_flat, jnp.asarray(slopes, F32), rows3(q_s), rows3(k_s), rows3(v_s), rows3(mq_s),
      cache_mem_k.reshape(nd, n_mem * MEM_HEADS, MEM_HEAD_DIM),
      cache_mem_v.reshape(nd, n_mem * MEM_HEADS, MEM_HEAD_DIM), ck, cv)
    ob_s = ob_s.reshape(nd, ATTN_W)
    oc_s = oc_s.reshape(nd, MEM_W)

    y_s, up_s = pl.pallas_call(
        _post_s_kernel,
        out_shape=[jax.ShapeDtypeStruct((nd, d), F32), jax.ShapeDtypeStruct((nd, 2 * dff), F32)],
        compiler_params=vm,
        name="post_s",
    )(xs, cact_s, ob_s, oc_s, gl_s, bg, wpa, wpb, wpc, wout, g2, wup, st_ffn, w_dw_f, bdw_f, wdown)

    heads = lambda a, b, s: a.reshape(b, s, N_HEADS, HEAD_DIM)
    from_t = lambda a: a.reshape(n, N_HEADS, HEAD_DIM, t).transpose(0, 3, 1, 2)
    conv_p = utail[:, CONV_HALO - (CONV_WIDTH - 1):, :]
    conv_s = jnp.concatenate([state_conv[:, 1:, :], u_s[:, None, :]], axis=1)
    ffn_p = ftail[:, SUBLANES - (FFN_CONV_WIDTH - 1):, :]
    ffn_s = jnp.concatenate([state_ffn_conv[:, 1:, :], up_s[:, None, :]], axis=1)
    return (y_p.reshape(n, t, d), y_s.reshape(nd, td, d),
            from_t(k_p), from_t(v_p), heads(k_s, nd, td), heads(v_s, nd, td),
            conv_p, conv_s, ffn_p, ffn_s,
            mk_p.reshape(n, n_mem, MEM_HEADS, MEM_HEAD_DIM), mv_p.reshape(n, n_mem, MEM_HEADS, MEM_HEAD_DIM))
```

```python
import functools

import numpy as np
import jax
import jax.numpy as jnp
from jax import lax
from jax.experimental import pallas as pl
from jax.experimental.pallas import tpu as pltpu

F32 = jnp.float32
BF16 = jnp.bfloat16

EPS = 1e-6
CONV_CH = 512
CONV_WIDTH = 31
N_HEADS = 8
HEAD_DIM = 64
ATTN_W = N_HEADS * HEAD_DIM
MOBA_BLOCK = 256
MOBA_TOPK = 3
MEM_HEADS = 4
MEM_HEAD_DIM = 128
MEM_W = MEM_HEADS * MEM_HEAD_DIM
FFN_CONV_WIDTH = 3
LANES = 128
SUBLANES = 8
TM = MOBA_BLOCK
CONV_HALO = 32
MASK_NEG = -float(2 ** 30)
GROUP = 16
V_ROWS = HEAD_DIM + 16
BLOCKS_PER_TRIP = 4
FFN_ROWS = 512
VMEM_LIMIT = 56 * 1024 * 1024

X_SEL = HEAD_DIM
X_PARTS = 3
X_RQ = HEAD_DIM + GROUP
X_RK = X_RQ + X_PARTS
X_TQ = X_RK + X_PARTS
X_TK = X_TQ + X_PARTS
LOG2E = 1.4426950408889634


def _dot(a, b):
    return jnp.dot(a, b, preferred_element_type=F32)


def _dot_nt(a, b):
    return lax.dot_general(a, b, (((1,), (1,)), ((), ())), preferred_element_type=F32)


def _rms_rows(x, g):
    return x * lax.rsqrt(jnp.mean(x * x, axis=-1, keepdims=True) + EPS) * g


def _group_rms(z, bd, g):
    sq = z * z
    hi = sq.astype(BF16)
    lo = (sq - hi.astype(F32)).astype(BF16)
    ms = _dot(hi, bd) + _dot(lo, bd)
    return z * lax.rsqrt(ms + EPS) * g


def _exact_zero(v):
    bits = pltpu.bitcast(v, jnp.uint32)
    half = jnp.uint32(16)
    return pltpu.bitcast(lax.shift_right_logical(lax.shift_right_logical(bits, half), half), F32)


def _sigmoid(x):
    return 1.0 / (1.0 + jnp.exp(-x))


def _layernorm_silu(c, g, b):
    mu = jnp.mean(c, axis=-1, keepdims=True)
    xc = c - mu
    var = jnp.mean(xc * xc, axis=-1, keepdims=True)
    y = xc * lax.rsqrt(var + EPS) * g + b
    return y * _sigmoid(y)


def _premix_kernel(x_ref, g1_ref, w_ref, bd64_ref, bd128_ref, qg_ref, kg_ref, mqg_ref,
                   wdw_ref, bdw_ref, lng_ref, lnb_ref, qx_ref, kx_ref,
                   k_ref, v_ref, km_ref, qt_ref, ka_ref, vt_ref, mq_ref, cact_ref, utail_ref,
                   ubuf):
    t = pl.program_id(1)
    nt = pl.num_programs(1)
    c = CONV_CH

    @pl.when(t == 0)
    def _():
        ubuf[0:CONV_HALO, :] = jnp.zeros((CONV_HALO, c), F32)
        ubuf[CONV_HALO + TM:, :] = jnp.zeros((SUBLANES, c), F32)

    hn = _rms_rows(x_ref[...], g1_ref[...]).astype(BF16)

    a = _dot(hn, w_ref[:, 0:c])
    g = _dot(hn, w_ref[:, c:2 * c])
    u = a * _sigmoid(g)
    ubuf[CONV_HALO:CONV_HALO + TM, :] = u
    o = 2 * c
    zq = _dot(hn, w_ref[:, o:o + ATTN_W])
    zk = _dot(hn, w_ref[:, o + ATTN_W:o + 2 * ATTN_W])
    zv = _dot(hn, w_ref[:, o + 2 * ATTN_W:o + 3 * ATTN_W])
    zm = _dot(hn, w_ref[:, o + 3 * ATTN_W:o + 3 * ATTN_W + MEM_W])
    acc = jnp.broadcast_to(bdw_ref[...], (TM, c))
    base = CONV_HALO - (CONV_WIDTH - 1)
    span = TM + 2 * SUBLANES
    for b in range(SUBLANES):
        part = None
        for k in range(b, CONV_WIDTH, SUBLANES):
            term = wdw_ref[k:k + 1, :] * ubuf[k - b:k - b + span, :]
            part = term if part is None else part + term
        acc = acc + part[base + b:base + b + TM, :]
    ubuf[0:CONV_HALO, :] = ubuf[TM:TM + CONV_HALO, :]
    cact_ref[...] = _layernorm_silu(acc, lng_ref[...], lnb_ref[...]).astype(BF16)

    qn = _group_rms(zq, bd64_ref[...], qg_ref[...]) * (HEAD_DIM ** -0.5)
    kn = _group_rms(zk, bd64_ref[...], kg_ref[...])
    mq = _group_rms(zm, bd128_ref[...], mqg_ref[...])

    ones_rows = jnp.where(lax.broadcasted_iota(jnp.int32, (V_ROWS - HEAD_DIM, TM), 0) == 0, 1.0, 0.0).astype(BF16)
    for cb in range(ATTN_W // LANES):
        k_ref[LANES * cb:LANES * (cb + 1), :] = kn[:, LANES * cb:LANES * (cb + 1)].T
        vt = zv[:, LANES * cb:LANES * (cb + 1)].T
        v_ref[LANES * cb:LANES * (cb + 1), :] = vt
        for sub in range(2):
            r0 = V_ROWS * (2 * cb + sub)
            vt_ref[0, r0:r0 + HEAD_DIM, :] = vt[HEAD_DIM * sub:HEAD_DIM * (sub + 1), :].astype(BF16)
            vt_ref[0, r0 + HEAD_DIM:r0 + V_ROWS, :] = ones_rows
    mq_ref[...] = mq.astype(BF16)
    km_ref[0] = jnp.mean(kn, axis=0, keepdims=True)

    lane = lax.broadcasted_iota(jnp.int32, (TM, LANES), 1)
    for hp in range(N_HEADS // 2):
        xq = qn[:, LANES * hp:LANES * (hp + 1)]
        xk = kn[:, LANES * hp:LANES * (hp + 1)] * LOG2E
        for sub in range(2):
            h = 2 * hp + sub
            if sub == 1:
                xq = pltpu.roll(xq, HEAD_DIM, 1)
                xk = pltpu.roll(xk, HEAD_DIM, 1)
            eq = qx_ref[0, :, LANES * h:LANES * (h + 1)].astype(F32)
            ek = kx_ref[0, :, LANES * h:LANES * (h + 1)].astype(F32)
            qt_ref[LANES * h:LANES * (h + 1), :] = jnp.where(lane < HEAD_DIM, xq, eq).T.astype(BF16)
            ka_ref[0, :, LANES * h:LANES * (h + 1)] = jnp.where(lane < HEAD_DIM, xk, ek).astype(BF16)

    @pl.when(t == nt - 1)
    def _():
        utail_ref[0] = ubuf[0:CONV_HALO, :]


def _moba_kernel(qt_ref, ka_ref, vt_ref, kmt_ref, o_ref, qh_scr, m_scr, acc_scr, *stage):
    i = pl.program_id(1)
    s_scr, mx_scr = stage[:N_HEADS], stage[N_HEADS:]
    key = lax.broadcasted_iota(jnp.int32, (TM, TM), 0)
    qry = lax.broadcasted_iota(jnp.int32, (TM, TM), 1)
    causal = key <= qry

    def scores(j, h, diagonal):
        q = qt_ref[LANES * h:LANES * (h + 1), :] if diagonal else qh_scr[h]
        s = _dot(ka_ref[j, :, LANES * h:LANES * (h + 1)], q)
        if diagonal:
            s = jnp.where(causal, s, -jnp.inf)
        s_scr[h][...] = s
        mx_scr[h][...] = jnp.max(s, axis=0, keepdims=True)

    for h in range(N_HEADS):
        scores(i, h, True)

    blk = lax.broadcasted_iota(jnp.int32, (N_HEADS, GROUP, TM), 1)
    past = blk < i
    gate = _dot(kmt_ref[0], qt_ref[...]).reshape(N_HEADS, GROUP, TM)
    g = jnp.where(past, gate, -jnp.inf)
    unselected = jnp.where(past, 1.0, 0.0)
    for _ in range(MOBA_TOPK):
        best = jnp.max(g, axis=1, keepdims=True)
        first = jnp.min(jnp.where(g == best, blk, GROUP), axis=1, keepdims=True)
        taken = blk == first
        g = jnp.where(taken, -jnp.inf, g)
        unselected = jnp.where(taken, 0.0, unselected)
    nsel = unselected.astype(BF16).reshape(N_HEADS * GROUP, TM)
    for h in range(N_HEADS):
        r0 = LANES * h
        qh_scr[h, 0:X_SEL, :] = qt_ref[r0:r0 + X_SEL, :]
        qh_scr[h, X_SEL:X_RQ, :] = nsel[GROUP * h:GROUP * (h + 1), :]
        qh_scr[h, X_RQ:LANES, :] = qt_ref[r0 + X_RQ:r0 + LANES, :]

    def accumulate(j, h):
        m_old = m_scr[h]
        m_new = jnp.maximum(m_old, mx_scr[h][...])
        alpha = jnp.exp2(m_old - m_new)
        p = jnp.exp2(s_scr[h][...] - m_new).astype(BF16)
        acc_scr[h] = alpha * acc_scr[h] + _dot(vt_ref[j, V_ROWS * h:V_ROWS * (h + 1), :], p)
        m_scr[h] = m_new

    m_scr[...] = jnp.full(m_scr.shape, -jnp.inf, F32)
    acc_scr[...] = jnp.zeros(acc_scr.shape, F32)

    def step(prev, j):
        for h in range(N_HEADS):
            accumulate(prev, h)
            scores(j, h, False)

    def steps(first, count):
        step(jnp.where(first == 0, i, first - 1), first)
        for extra in range(1, count):
            step(first + extra - 1, first + extra)

    def body(trip, carry):
        steps(trip * BLOCKS_PER_TRIP, BLOCKS_PER_TRIP)
        return carry

    lax.fori_loop(0, i // BLOCKS_PER_TRIP, body, 0)
    size = BLOCKS_PER_TRIP // 2
    while size:
        done = (i // (2 * size)) * (2 * size)

        @pl.when((i // size) % 2 == 1)
        def _(done=done, size=size):
            steps(done, size)

        size //= 2
    last = jnp.where(i == 0, i, i - 1)
    for h in range(N_HEADS):
        accumulate(last, h)
    outs = []
    for h in range(N_HEADS):
        acc = acc_scr[h]
        outs.append(acc[0:HEAD_DIM, :] / acc[HEAD_DIM:HEAD_DIM + 1, :])
    o_ref[...] = jnp.concatenate(outs, axis=0).T.astype(BF16)


def _memkv_kernel(mem_ref, g_ref, w_ref, bd128_ref, mkg_ref, mk_ref, mv_ref, mkb_ref, mvb_ref):
    hn = _rms_rows(mem_ref[...], g_ref[...]).astype(BF16)
    mk = _group_rms(_dot(hn, w_ref[:, 0:MEM_W]), bd128_ref[...], mkg_ref[...])
    mv = _dot(hn, w_ref[:, MEM_W:2 * MEM_W])
    mk_ref[...] = mk
    mv_ref[...] = mv
    mkb_ref[...] = mk.astype(BF16)
    mvb_ref[...] = mv.astype(BF16)


def _mem_attend_rows(mq, mk, mv):
    heads = [slice(MEM_HEAD_DIM * hh, MEM_HEAD_DIM * (hh + 1)) for hh in range(MEM_HEADS)]
    scores = [_dot_nt(mq[:, sl], mk[:, sl]) * (MEM_HEAD_DIM ** -0.5) for sl in heads]
    probs = []
    for s in scores:
        p = jnp.exp(s - jnp.max(s, axis=-1, keepdims=True))
        probs.append((p.astype(BF16), jnp.sum(p, axis=-1, keepdims=True)))
    return jnp.concatenate([_dot(p, mv[:, sl]) / l for (p, l), sl in zip(probs, heads)], axis=-1)


def _merge_out(x, gl, ya, yb, yc, wout):
    d = x.shape[-1]
    merged = (_sigmoid(gl[:, 0:d]) * ya + _sigmoid(gl[:, d:2 * d]) * yb + _sigmoid(gl[:, 2 * d:3 * d]) * yc)
    return x + _dot(merged.astype(BF16), wout)


def _postmix_kernel(pt_ref, x_ref, cact_ref, ob_ref, mq_ref, mk_ref, mv_ref, g1_ref, wg_ref, bg_ref,
                    wpa_ref, wpb_ref, wpc_ref, wout_ref, q_ref, ck_hbm, x1_ref, sel_ref,
                    gate_scr, page_buf, page_sem, *, pages_per_step, n_blocks):
    step = pl.program_id(0) * pl.num_programs(1) + pl.program_id(1)
    n_steps = pl.num_programs(0) * pl.num_programs(1)
    slot = step % 2

    def page_copy(page, to_slot, r):
        return pltpu.make_async_copy(ck_hbm.at[page], page_buf.at[to_slot, r], page_sem.at[to_slot])

    def fetch(for_step, to_slot):
        for r in range(pages_per_step):
            page_copy(pt_ref[for_step * pages_per_step + r], to_slot, r).start()

    @pl.when(step == 0)
    def _():
        gate_scr[...] = jnp.zeros(gate_scr.shape, F32)
        fetch(0, 0)

    for r in range(pages_per_step):
        page_copy(0, slot, r).wait()
    page_refs = [page_buf.at[slot, r] for r in range(pages_per_step)]

    blocks_per_step = pages_per_step // PAGES_PER_BLOCK
    steps_per_sample = n_blocks // blocks_per_step
    share = step % steps_per_sample

    g = jnp.where(share == 0, 0.0, gate_scr[...])
    x = x_ref[...]
    d = x.shape[-1]
    hn = _rms_rows(x, g1_ref[...]).astype(BF16)
    oc = _mem_attend_rows(mq_ref[...], mk_ref[...], mv_ref[...]).astype(BF16)
    fetch(jnp.minimum(step + 1, n_steps - 1), 1 - slot)
    g = _gate_scores(g, share * blocks_per_step, q_ref, page_refs)
    tie = jnp.concatenate([_exact_zero(g)[0:1, :]] * (d // LANES), axis=1)
    sources = ((cact_ref[...], wpa_ref), (ob_ref[...], wpb_ref), (oc, wpc_ref))
    ys = [_dot(src, w_ref[...]) for src, w_ref in sources]
    merged = None
    for br in range(len(sources)):
        cols = slice(d * br, d * (br + 1))
        gl = _dot(hn, wg_ref[:, cols]) + (bg_ref[:, cols] + tie)
        term = _sigmoid(gl) * ys[br]
        merged = term if merged is None else merged + term
    x1_ref[...] = x + _dot(merged.astype(BF16), wout_ref[...])
    gate_scr[...] = g
    _gate_select(share == steps_per_sample - 1, gate_scr, sel_ref, n_blocks)

    @pl.when(step == n_steps - 1)
    def _():
        for r in range(pages_per_step):
            page_copy(0, 1 - slot, r).wait()


def _ffn_kernel(x1_ref, g2_ref, wup_ref, wdw_ref, bdw_ref, wdown_ref, y_ref, tail_ref, upbuf, *, chunk):
    rows = x1_ref.shape[0]
    t = pl.program_id(1)
    nt = pl.num_programs(1)
    dff = wdown_ref.shape[0]

    @pl.when(t == 0)
    def _():
        upbuf[0:SUBLANES, :] = jnp.zeros((SUBLANES, 2 * dff), F32)

    x1 = x1_ref[...]
    hn = _rms_rows(x1, g2_ref[...]).astype(BF16)

    for c in range(0, 2 * dff, chunk):
        upbuf[SUBLANES:SUBLANES + rows, c:c + chunk] = _dot(hn, wup_ref[:, c:c + chunk])

    def conv(c):
        out = bdw_ref[:, c:c + chunk]
        for k in range(FFN_CONV_WIDTH):
            r0 = SUBLANES - (FFN_CONV_WIDTH - 1) + k
            out = out + wdw_ref[k:k + 1, c:c + chunk] * upbuf[r0:r0 + rows, c:c + chunk]
        return out

    y = x1
    for c in range(0, dff, chunk):
        a = conv(c)
        b = conv(dff + c)
        act = (a * _sigmoid(a) * b).astype(BF16)
        y = y + _dot(act, wdown_ref[c:c + chunk, :])
    y_ref[...] = y

    @pl.when(t == nt - 1)
    def _():
        tail_ref[0] = upbuf[rows:rows + SUBLANES, :]

    upbuf[0:SUBLANES, :] = upbuf[rows:rows + SUBLANES, :]


def _premix_s_kernel(x_ref, g1_ref, w_ref, bd64_ref, bd128_ref, qg_ref, kg_ref, mqg_ref,
                     st_ref, wdw_ref, bdw_ref, lng_ref, lnb_ref,
                     u_ref, q_ref, k_ref, v_ref, mq_ref, gl_ref, cact_ref):
    c = CONV_CH
    hn = _rms_rows(x_ref[...], g1_ref[...]).astype(BF16)
    a = _dot(hn, w_ref[:, 0:c])
    g = _dot(hn, w_ref[:, c:2 * c])
    u = a * _sigmoid(g)
    u_ref[...] = u
    acc = bdw_ref[...] + wdw_ref[CONV_WIDTH - 1:CONV_WIDTH, :] * u
    for k in range(CONV_WIDTH - 1):
        acc = acc + wdw_ref[k:k + 1, :] * st_ref[k]
    cact_ref[...] = _layernorm_silu(acc, lng_ref[...], lnb_ref[...]).astype(BF16)

    o = 2 * c
    q_ref[...] = _group_rms(_dot(hn, w_ref[:, o:o + ATTN_W]), bd64_ref[...], qg_ref[...])
    o += ATTN_W
    k_ref[...] = _group_rms(_dot(hn, w_ref[:, o:o + ATTN_W]), bd64_ref[...], kg_ref[...])
    o += ATTN_W
    v_ref[...] = _dot(hn, w_ref[:, o:o + ATTN_W])
    o += ATTN_W
    mq_ref[...] = _group_rms(_dot(hn, w_ref[:, o:o + MEM_W]), bd128_ref[...], mqg_ref[...])
    o += MEM_W
    gl_ref[...] = _dot(hn, w_ref[:, o:])


PAGES_PER_BLOCK = MOBA_BLOCK // LANES


def _gate_scores(g, first_block, q_ref, page_refs):
    head = lax.broadcasted_iota(jnp.int32, (N_HEADS, LANES), 0)
    lane = lax.broadcasted_iota(jnp.int32, (HEAD_DIM, LANES), 1)
    for h in range(N_HEADS):
        kmean = jnp.zeros((HEAD_DIM, LANES), F32)
        for r in range(0, len(page_refs), PAGES_PER_BLOCK):
            ksum = page_refs[r][h]
            for extra in range(1, PAGES_PER_BLOCK):
                ksum = ksum + page_refs[r + extra][h]
            col = jnp.sum(ksum, axis=1, keepdims=True) * (1.0 / MOBA_BLOCK)
            kmean = jnp.where(lane == first_block + r // PAGES_PER_BLOCK, col, kmean)
        prod = q_ref[0, h].astype(BF16).astype(F32) * kmean.astype(BF16).astype(F32)
        g = g + jnp.where(head == h, jnp.sum(prod, axis=0, keepdims=True), 0.0)
    return g


def _gate_select(last, gate_scr, sel_ref, n_blocks):
    @pl.when(last)
    def _():
        g = gate_scr[:, 0:n_blocks]
        bl = lax.broadcasted_iota(jnp.int32, (N_HEADS, n_blocks), 1)
        rank = jnp.zeros((N_HEADS, n_blocks), F32)
        for b in range(n_blocks):
            other = g[:, b:b + 1]
            rank = rank + jnp.where(bl > b, jnp.where(other >= g, 1.0, 0.0), jnp.where(other > g, 1.0, 0.0))
        lane_o = lax.broadcasted_iota(jnp.int32, (N_HEADS, LANES), 1)
        out = jnp.zeros((N_HEADS, LANES), F32)
        blf = bl.astype(F32)
        for r in range(MOBA_TOPK):
            idx = jnp.sum(jnp.where(rank == float(r), blf, 0.0), axis=-1, keepdims=True)
            out = jnp.where(lane_o == r, idx, out)
        sel_ref[0] = out.astype(jnp.int32)


def _decode_attn_kernel(page_ref, sel_ref, slope_ref, q_ref, kown_ref, vown_ref, mq_ref, mk_ref, mv_ref,
                        ck_hbm, cv_hbm, o_ref, oc_ref, k_buf, v_buf, sem, *, past_len, n_mem):
    pages_per_block = MOBA_BLOCK // LANES
    tiles_per_head = MOBA_TOPK * pages_per_block
    n_tiles = N_HEADS * tiles_per_head
    n = pl.program_id(0)
    slot = n % 2

    def tile_copies(page, to_slot, tile):
        h = tile // tiles_per_head
        return (pltpu.make_async_copy(ck_hbm.at[page, h], k_buf.at[to_slot, tile], sem.at[0, to_slot]),
                pltpu.make_async_copy(cv_hbm.at[page, h], v_buf.at[to_slot, tile], sem.at[1, to_slot]))

    def fetch(sample, to_slot):
        for tile in range(n_tiles):
            for priority, copy in enumerate(tile_copies(page_ref[sample * n_tiles + tile], to_slot, tile)):
                copy.start(priority=priority)

    @pl.when(n == 0)
    def _():
        fetch(0, 0)

    @pl.when(n + 1 < pl.num_programs(0))
    def _():
        fetch(n + 1, 1 - slot)

    for tile in range(n_tiles):
        for copy in tile_copies(0, slot, tile):
            copy.wait()
    k_refs = [k_buf.at[slot, tile] for tile in range(n_tiles)]
    v_refs = [v_buf.at[slot, tile] for tile in range(n_tiles)]

    lane = lax.broadcasted_iota(jnp.int32, (1, LANES), 1)
    lanef = lane.astype(F32)
    zeros = jnp.zeros((HEAD_DIM, tiles_per_head * LANES), BF16)
    scale = HEAD_DIM ** -0.5

    def head_tiles(tile_refs, h):
        t = jnp.concatenate([tile_refs[h * tiles_per_head + idx][...] for idx in range(tiles_per_head)], axis=1)
        t = t.astype(BF16)
        return jnp.concatenate([t, zeros] if h % 2 == 0 else [zeros, t], axis=0)

    def own_half(h):
        return (lane < HEAD_DIM) if h % 2 == 0 else (lane >= HEAD_DIM)

    def pair_lanes(ref, h):
        return ref[0][:, LANES * (h // 2):LANES * (h // 2 + 1)]

    qz, raw = [], []
    for h in range(N_HEADS):
        qz.append(jnp.where(own_half(h), pair_lanes(q_ref, h) * scale, 0.0).astype(BF16))
        raw.append(_dot(jnp.broadcast_to(qz[h], (SUBLANES, LANES)), head_tiles(k_refs, h))[0:1, :])
    mem_scores = []
    for hh in range(MEM_HEADS):
        mq = jnp.broadcast_to(mq_ref[0][:, MEM_HEAD_DIM * hh:MEM_HEAD_DIM * (hh + 1)], (SUBLANES, MEM_HEAD_DIM))
        mk = mk_ref[0, pl.ds(hh, n_mem, stride=MEM_HEADS), :].astype(BF16)
        mem_scores.append(_dot_nt(mq.astype(BF16), mk)[0:1, :] * (MEM_HEAD_DIM ** -0.5))
    mem_probs = []
    for sc in mem_scores:
        p = jnp.exp(sc - jnp.max(sc, axis=-1, keepdims=True))
        mem_probs.append((p, jnp.sum(p, axis=-1, keepdims=True)))
    probs = []
    for h in range(N_HEADS):
        dist = []
        for r in range(MOBA_TOPK):
            blk = sel_ref[(n * N_HEADS + h) * MOBA_TOPK + r]
            for half in range(pages_per_block):
                pos0 = (blk * MOBA_BLOCK + half * LANES).astype(F32)
                dist.append(float(past_len) - (pos0 + lanef))
        logits = raw[h] - slope_ref[h] * jnp.concatenate(dist, axis=1)
        kown = pair_lanes(kown_ref, h).astype(BF16).astype(F32)
        s_own = jnp.sum(qz[h].astype(F32) * kown, axis=-1, keepdims=True)
        m = jnp.maximum(s_own, jnp.max(logits, axis=-1, keepdims=True))
        p_own = jnp.exp(s_own - m)
        p = jnp.exp(logits - m)
        probs.append((p, p_own, p_own + jnp.sum(p, axis=-1, keepdims=True)))
    outs = []
    for h in range(N_HEADS):
        p, p_own, l = probs[h]
        p8 = jnp.broadcast_to(p.astype(BF16), (SUBLANES, p.shape[1]))
        vown = jnp.where(own_half(h), pair_lanes(vown_ref, h).astype(BF16).astype(F32), 0.0)
        acc = p_own.astype(BF16).astype(F32) * vown + _dot_nt(p8, head_tiles(v_refs, h))[0:1, :]
        outs.append(acc / l)
    o_ref[0] = jnp.concatenate([outs[h] + outs[h + 1] for h in range(0, N_HEADS, 2)], axis=1)
    mem_outs = []
    for hh, (p, l) in enumerate(mem_probs):
        mv = mv_ref[0, pl.ds(hh, n_mem, stride=MEM_HEADS), :].astype(BF16)
        p8 = jnp.broadcast_to(p.astype(BF16), (SUBLANES, n_mem))
        mem_outs.append(_dot(p8, mv)[0:1, :] / l)
    oc_ref[0] = jnp.concatenate(mem_outs, axis=-1)


def _post_s_kernel(x_ref, cact_ref, ob_ref, oc_ref, gl_ref, bg_ref, wpa_ref, wpb_ref, wpc_ref, wout_ref,
                   g2_ref, wup_ref, st_ref, wdw_ref, bdw_ref, wdown_ref, y_ref, up_ref):
    x = x_ref[...]
    dff = wdown_ref.shape[0]
    ya = _dot(cact_ref[...], wpa_ref[...])
    yb = _dot(ob_ref[...].astype(BF16), wpb_ref[...])
    yc = _dot(oc_ref[...].astype(BF16), wpc_ref[...])
    x1 = _merge_out(x, gl_ref[...] + bg_ref[...], ya, yb, yc, wout_ref[...])
    up = _dot(_rms_rows(x1, g2_ref[...]).astype(BF16), wup_ref[...])
    up_ref[...] = up
    cv = bdw_ref[...] + wdw_ref[FFN_CONV_WIDTH - 1:FFN_CONV_WIDTH, :] * up
    for k in range(FFN_CONV_WIDTH - 1):
        cv = cv + wdw_ref[k:k + 1, :] * st_ref[k]
    a = cv[:, 0:dff]
    b = cv[:, dff:]
    act = (a * _sigmoid(a) * b).astype(BF16)
    y_ref[...] = x1 + _dot(act, wdown_ref[...])


def _round_to_bf16(x):
    bits = np.asarray(x, np.float32).view(np.uint32)
    bits = (bits + np.uint32(0x7FFF) + ((bits >> np.uint32(16)) & np.uint32(1))) & np.uint32(0xFFFF0000)
    return bits.view(np.float32)


def _resident(shape):
    nd = len(shape)
    return pl.BlockSpec(shape, lambda *_: (0,) * nd, pipeline_mode=pl.Buffered(1))


def _params(n_axes):
    return pltpu.CompilerParams(dimension_semantics=("arbitrary",) * n_axes, vmem_limit_bytes=VMEM_LIMIT)


def kernel(x_prompt, x_sample, mem_prompt, cache_k, cache_v, page_table, state_conv, state_ffn_conv, cache_mem_k, cache_mem_v, norm1_g, w_in, b_gate, w_dw_a, b_dw_a, ln_a_g, ln_a_b, w_proj_a, q_norm_g, k_norm_g, w_proj_b, mem_norm_g, w_mem_kv, mq_norm_g, mk_norm_g, w_proj_c, w_out, norm2_g, w_up, w_dw_f, b_dw_f, w_down):
    n, t, d = x_prompt.shape
    nd, td, _ = x_sample.shape
    n_mem = mem_prompt.shape[1]
    n_pool, page_size = cache_k.shape[:2]
    n_pages = page_table.shape[1]
    past_len = n_pages * page_size
    dff = w_down.shape[0]
    nt = t // TM
    n_mix = 2 * CONV_CH + 3 * ATTN_W + MEM_W
    assert t % TM == 0 and nt <= GROUP and td == 1
    assert past_len % MOBA_BLOCK == 0 and MOBA_BLOCK == 2 * page_size and page_size == LANES
    n_blocks = past_len // MOBA_BLOCK

    row = lambda v: v.reshape(1, -1).astype(F32)
    w_in_b = w_in.astype(BF16)
    w_mix, w_gate = w_in_b[:, :n_mix], w_in_b[:, n_mix:]
    wpa, wpb, wpc = w_proj_a.astype(BF16), w_proj_b.astype(BF16), w_proj_c.astype(BF16)
    wout, wup, wdown = w_out.astype(BF16), w_up.astype(BF16), w_down.astype(BF16)
    g1, g2 = row(norm1_g), row(norm2_g)
    qg = row(jnp.tile(q_norm_g, N_HEADS))
    kg = row(jnp.tile(k_norm_g, N_HEADS))
    mqg = row(jnp.tile(mq_norm_g, MEM_HEADS))
    mkg = row(jnp.tile(mk_norm_g, MEM_HEADS))
    bdw_a, lng, lnb, bg, bdw_f = row(b_dw_a), row(ln_a_g), row(ln_a_b), row(b_gate), row(b_dw_f)
    grp = np.arange(ATTN_W)
    bd64 = jnp.asarray((grp[:, None] // HEAD_DIM == grp[None, :] // HEAD_DIM) / HEAD_DIM, BF16)
    bd128 = jnp.asarray((grp[:, None] // MEM_HEAD_DIM == grp[None, :] // MEM_HEAD_DIM) / MEM_HEAD_DIM, BF16)
    slopes = 2.0 ** (-8.0 * np.arange(1, N_HEADS + 1) / N_HEADS)
    aw = N_HEADS * LANES
    coef = {name: np.zeros((N_HEADS, LANES), np.float32) for name in ("qc", "qr", "qt", "kc", "kr", "kt")}
    rest = (slopes * LOG2E).astype(np.float32)
    for p in range(X_PARTS):
        part = _round_to_bf16(rest)
        rest = rest - part
        coef["qr"][:, X_RQ + p] = 1.0
        coef["kc"][:, X_RQ + p] = -part
        coef["qc"][:, X_RK + p] = part
        coef["kr"][:, X_RK + p] = 1.0
        coef["qt"][:, X_TQ + p] = 1.0
        coef["kc"][:, X_TQ + p] = -part * MOBA_BLOCK
        coef["qc"][:, X_TK + p] = part
        coef["kt"][:, X_TK + p] = MOBA_BLOCK
    lanes3 = lambda name: jnp.asarray(coef[name].reshape(1, 1, aw))
    tile_f = jnp.arange(nt, dtype=F32)[:, None, None]
    row_f = jnp.arange(TM, dtype=F32)[None, :, None]
    col = jnp.arange(aw, dtype=jnp.int32)[None, None, :] % LANES
    q_extra = (lanes3("qc") + row_f * lanes3("qr") + tile_f * lanes3("qt")).astype(BF16)
    k_extra = (lanes3("kc") + row_f * lanes3("kr") + tile_f * lanes3("kt")
               + jnp.where(col == X_SEL + tile_f.astype(jnp.int32), MASK_NEG, 0.0)).astype(BF16)

    xp = x_prompt.reshape(n * t, d)
    tile = lambda w: pl.BlockSpec((TM, w), lambda b, i, *_: (b * nt + i, 0))
    aw = N_HEADS * LANES
    vw = N_HEADS * V_ROWS
    blk3 = lambda r, c: pl.BlockSpec((1, r, c), lambda b, i: (b * nt + i, 0, 0))
    k_p, v_p, km, qt, ka, vt, mq, cact, utail = pl.pallas_call(
        _premix_kernel,
        grid=(n, nt),
        in_specs=[tile(d), _resident((1, d)), _resident((d, n_mix)), _resident((ATTN_W, ATTN_W)),
                  _resident((MEM_W, MEM_W)), _resident((1, ATTN_W)), _resident((1, ATTN_W)),
                  _resident((1, MEM_W)), _resident((CONV_WIDTH, CONV_CH)), _resident((1, CONV_CH)),
                  _resident((1, CONV_CH)), _resident((1, CONV_CH)),
                  pl.BlockSpec((1, TM, aw), lambda b, i: (i, 0, 0)),
                  pl.BlockSpec((1, TM, aw), lambda b, i: (i, 0, 0))],
        out_specs=[pl.BlockSpec((ATTN_W, TM), lambda b, i: (b, i)), pl.BlockSpec((ATTN_W, TM), lambda b, i: (b, i)),
                   pl.BlockSpec((1, 1, ATTN_W), lambda b, i: (b * nt + i, 0, 0)),
                   pl.BlockSpec((aw, TM), lambda b, i: (b, i)), blk3(TM, aw), blk3(vw, TM),
                   tile(MEM_W), tile(CONV_CH),
                   pl.BlockSpec((1, CONV_HALO, CONV_CH), lambda b, i: (b, 0, 0))],
        out_shape=[jax.ShapeDtypeStruct((n * ATTN_W, t), F32), jax.ShapeDtypeStruct((n * ATTN_W, t), F32),
                   jax.ShapeDtypeStruct((n * nt, 1, ATTN_W), F32),
                   jax.ShapeDtypeStruct((n * aw, t), BF16), jax.ShapeDtypeStruct((n * nt, TM, aw), BF16),
                   jax.ShapeDtypeStruct((n * nt, vw, TM), BF16), jax.ShapeDtypeStruct((n * t, MEM_W), BF16),
                   jax.ShapeDtypeStruct((n * t, CONV_CH), BF16),
                   jax.ShapeDtypeStruct((n, CONV_HALO, CONV_CH), F32)],
        scratch_shapes=[pltpu.VMEM((CONV_HALO + TM + SUBLANES, CONV_CH), F32)],
        compiler_params=_params(2),
        name="premix",
    )(xp, g1, w_mix, bd64, bd128, qg, kg, mqg, w_dw_a, bdw_a, lng, lnb, q_extra, k_extra)

    km4 = km.reshape(n, nt, N_HEADS, HEAD_DIM).transpose(0, 2, 1, 3)
    km4 = jnp.pad(km4, ((0, 0), (0, 0), (0, GROUP - nt), (0, LANES - HEAD_DIM)))
    kmt = (km4[:, :, :, None, :] * jnp.eye(N_HEADS, dtype=F32)[None, :, None, :, None])
    kmt = kmt.reshape(n, N_HEADS * GROUP, aw).astype(BF16)

    seq3 = lambda r, c: pl.BlockSpec((nt, r, c), lambda b, i: (b, 0, 0))
    ob = pl.pallas_call(
        _moba_kernel,
        grid=(n, nt),
        in_specs=[pl.BlockSpec((aw, TM), lambda b, i: (b, i)), seq3(TM, aw), seq3(vw, TM),
                  pl.BlockSpec((1, N_HEADS * GROUP, aw), lambda b, i: (b, 0, 0))],
        out_specs=tile(ATTN_W),
        out_shape=jax.ShapeDtypeStruct((n * t, ATTN_W), BF16),
        scratch_shapes=[pltpu.VMEM((N_HEADS, LANES, TM), BF16), pltpu.VMEM((N_HEADS, 1, TM), F32),
                        pltpu.VMEM((N_HEADS, V_ROWS, TM), F32)]
        + [pltpu.VMEM((TM, TM), F32)] * N_HEADS + [pltpu.VMEM((1, TM), F32)] * N_HEADS,
        compiler_params=_params(2),
        name="moba",
    )(qt, ka, vt, kmt)

    mem = mem_prompt.reshape(n * n_mem, d)
    mtile = lambda w: pl.BlockSpec((n_mem, w), lambda b: (b, 0))
    mk_p, mv_p, mkb, mvb = pl.pallas_call(
        _memkv_kernel,
        grid=(n,),
        in_specs=[mtile(d), _resident((1, d)), _resident((d, 2 * MEM_W)), _resident((MEM_W, MEM_W)),
                  _resident((1, MEM_W))],
        out_specs=[mtile(MEM_W)] * 4,
        out_shape=[jax.ShapeDtypeStruct((n * n_mem, MEM_W), F32)] * 2
        + [jax.ShapeDtypeStruct((n * n_mem, MEM_W), BF16)] * 2,
        compiler_params=_params(1),
        name="memkv",
    )(mem, row(mem_norm_g), w_mem_kv.astype(BF16), bd128, mkg)

    xs = x_sample.reshape(nd, d)
    st_conv = state_conv.transpose(1, 0, 2)
    st_ffn = state_ffn_conv.transpose(1, 0, 2)
    vm = pltpu.CompilerParams(vmem_limit_bytes=VMEM_LIMIT)
    u_s, q_s, k_s, v_s, mq_s, gl_s, cact_s = pl.pallas_call(
        _premix_s_kernel,
        out_shape=[jax.ShapeDtypeStruct((nd, CONV_CH), F32), jax.ShapeDtypeStruct((nd, ATTN_W), F32),
                   jax.ShapeDtypeStruct((nd, ATTN_W), F32), jax.ShapeDtypeStruct((nd, ATTN_W), F32),
                   jax.ShapeDtypeStruct((nd, MEM_W), F32), jax.ShapeDtypeStruct((nd, 3 * d), F32),
                   jax.ShapeDtypeStruct((nd, CONV_CH), BF16)],
        compiler_params=vm,
        name="premix_s",
    )(xs, g1, w_in_b, bd64, bd128, qg, kg, mqg, st_conv, w_dw_a, bdw_a, lng, lnb)

    ck = cache_k.transpose(0, 2, 3, 1)
    cv = cache_v.transpose(0, 2, 3, 1)
    pt_flat = page_table.reshape(-1).astype(jnp.int32)
    q_cols = jnp.broadcast_to(q_s.reshape(nd, N_HEADS, HEAD_DIM, 1), (nd, N_HEADS, HEAD_DIM, page_size))
    assert (n * nt) % nd == 0 and n_pages % ((n * nt) // nd) == 0 and n_blocks <= LANES
    steps_per_sample = (n * nt) // nd
    pages_per_step = n_pages // steps_per_sample
    assert pages_per_step % (MOBA_BLOCK // page_size) == 0

    memb = pl.BlockSpec((n_mem, MEM_W), lambda b, i, *_: (b, 0))
    x1, sel = pl.pallas_call(
        functools.partial(_postmix_kernel, pages_per_step=pages_per_step, n_blocks=n_blocks),
        grid_spec=pltpu.PrefetchScalarGridSpec(
            num_scalar_prefetch=1,
            grid=(n, nt),
            in_specs=[tile(d), tile(CONV_CH), tile(ATTN_W), tile(MEM_W), memb, memb, _resident((1, d)),
                      _resident((d, 3 * d)), _resident((1, 3 * d)), _resident((CONV_CH, d)),
                      _resident((ATTN_W, d)), _resident((MEM_W, d)), _resident((d, d)),
                      pl.BlockSpec((1, N_HEADS, HEAD_DIM, page_size),
                                   lambda b, i, pt: ((b * nt + i) // steps_per_sample, 0, 0, 0)),
                      pl.BlockSpec(memory_space=pl.ANY)],
            out_specs=[tile(d), pl.BlockSpec((1, N_HEADS, LANES),
                                             lambda b, i, pt: ((b * nt + i) // steps_per_sample, 0, 0))],
            scratch_shapes=[pltpu.VMEM((N_HEADS, LANES), F32),
                            pltpu.VMEM((2, pages_per_step, N_HEADS, HEAD_DIM, page_size), F32),
                            pltpu.SemaphoreType.DMA((2,))]),
        out_shape=[jax.ShapeDtypeStruct((n * t, d), F32), jax.ShapeDtypeStruct((nd, N_HEADS, LANES), jnp.int32)],
        compiler_params=_params(2),
        name="postmix",
    )(pt_flat, xp, cact, ob, mq, mkb, mvb, g1, w_gate, bg, wpa, wpb, wpc, wout, q_cols, ck)
    sel_flat = sel[:, :, :MOBA_TOPK].reshape(-1)

    chunk = dff // 2
    assert chunk % LANES == 0 and t % FFN_ROWS == 0
    nf = t // FFN_ROWS
    ftile = pl.BlockSpec((FFN_ROWS, d), lambda b, i: (b * nf + i, 0))
    y_p, ftail = pl.pallas_call(
        functools.partial(_ffn_kernel, chunk=chunk),
        grid=(n, nf),
        in_specs=[ftile, _resident((1, d)), _resident((d, 2 * dff)), _resident((FFN_CONV_WIDTH, 2 * dff)),
                  _resident((1, 2 * dff)), _resident((dff, d))],
        out_specs=[ftile, pl.BlockSpec((1, SUBLANES, 2 * dff), lambda b, i: (b, 0, 0))],
        out_shape=[jax.ShapeDtypeStruct((n * t, d), F32), jax.ShapeDtypeStruct((n, SUBLANES, 2 * dff), F32)],
        scratch_shapes=[pltpu.VMEM((SUBLANES + FFN_ROWS, 2 * dff), F32)],
        compiler_params=_params(2),
        name="ffn",
    )(x1, g2, wup, w_dw_f, bdw_f, wdown)

    row_spec = pl.BlockSpec((1, 1, ATTN_W), lambda b, *_: (b, 0, 0))
    rows3 = lambda a: a.reshape(nd, 1, -1)

    halves = jnp.arange(PAGES_PER_BLOCK, dtype=jnp.int32)
    sel_pages = PAGES_PER_BLOCK * sel[:, :, :MOBA_TOPK, None] + halves
    page_ids = jnp.take_along_axis(page_table.astype(jnp.int32), sel_pages.reshape(nd, -1), axis=1).reshape(-1)
    tiles_per_sample = N_HEADS * MOBA_TOPK * PAGES_PER_BLOCK
    tile_buf = pltpu.VMEM((2, tiles_per_sample, HEAD_DIM, page_size), F32)
    mem_row = pl.BlockSpec((1, 1, MEM_W), lambda b, *_: (b, 0, 0))
    cm_spec = pl.BlockSpec((1, n_mem * MEM_HEADS, MEM_HEAD_DIM), lambda b, *_: (b, 0, 0))
    ob_s, oc_s = pl.pallas_call(
        functools.partial(_decode_attn_kernel, past_len=past_len, n_mem=n_mem),
        grid_spec=pltpu.PrefetchScalarGridSpec(
            num_scalar_prefetch=2,
            grid=(nd,),
            in_specs=[pl.BlockSpec(memory_space=pltpu.SMEM), row_spec, row_spec, row_spec,
                      mem_row, cm_spec, cm_spec,
                      pl.BlockSpec(memory_space=pl.ANY), pl.BlockSpec(memory_space=pl.ANY)],
            out_specs=[row_spec, mem_row],
            scratch_shapes=[tile_buf, tile_buf, pltpu.SemaphoreType.DMA((2, 2))]),
        out_shape=[jax.ShapeDtypeStruct((nd, 1, ATTN_W), F32), jax.ShapeDtypeStruct((nd, 1, MEM_W), F32)],
        compiler_params=_params(1),
        name="decode_attn",
    )(page_ids, sel_flat, jnp.asarray(slopes, F32), rows3(q_s), rows3(k_s), rows3(v_s), rows3(mq_s),
      cache_mem_k.reshape(nd, n_mem * MEM_HEADS, MEM_HEAD_DIM),
      cache_mem_v.reshape(nd, n_mem * MEM_HEADS, MEM_HEAD_DIM), ck, cv)
    ob_s = ob_s.reshape(nd, ATTN_W)
    oc_s = oc_s.reshape(nd, MEM_W)

    y_s, up_s = pl.pallas_call(
        _post_s_kernel,
        out_shape=[jax.ShapeDtypeStruct((nd, d), F32), jax.ShapeDtypeStruct((nd, 2 * dff), F32)],
        compiler_params=vm,
        name="post_s",
    )(xs, cact_s, ob_s, oc_s, gl_s, bg, wpa, wpb, wpc, wout, g2, wup, st_ffn, w_dw_f, bdw_f, wdown)

    heads = lambda a, b, s: a.reshape(b, s, N_HEADS, HEAD_DIM)
    from_t = lambda a: a.reshape(n, N_HEADS, HEAD_DIM, t).transpose(0, 3, 1, 2)
    conv_p = utail[:, CONV_HALO - (CONV_WIDTH - 1):, :]
    conv_s = jnp.concatenate([state_conv[:, 1:, :], u_s[:, None, :]], axis=1)
    ffn_p = ftail[:, SUBLANES - (FFN_CONV_WIDTH - 1):, :]
    ffn_s = jnp.concatenate([state_ffn_conv[:, 1:, :], up_s[:, None, :]], axis=1)
    return (y_p.reshape(n, t, d), y_s.reshape(nd, td, d),
            from_t(k_p), from_t(v_p), heads(k_s, nd, td), heads(v_s, nd, td),
            conv_p, conv_s, ffn_p, ffn_s,
            mk_p.reshape(n, n_mem, MEM_HEADS, MEM_HEAD_DIM), mv_p.reshape(n, n_mem, MEM_HEADS, MEM_HEAD_DIM))
```
